```python
import jax, jax.numpy as jnp
from jax import lax
import numpy as np

D_MODEL = 4096
BATCH = 2
SEQ = 4096
DEPTH = 2
DEC_BATCH = 2
DEC_SEQ = 8192
PAST_LEN = 128

D_FF = 11008
GLA_HEADS = 4
GLA_DK = D_MODEL // 16
GLA_DV = D_MODEL // 8
GLA_RANK = 16
GLA_TAU = 16.0
GLA_CHUNK = 64
ATTN_KV_HEADS = 8
ATTN_HEAD_DIM = 128
DIL_WINDOWS = (128, 512, 2048)
DIL_RATES = (1, 4, 16)
N_DIL_GROUPS = 3
ALIBI_MAX_BIAS = 8.0
NORM_EPS = 1e-6
NEG_INF = -1e30

GLA_QK_W = GLA_HEADS * GLA_DK
GLA_V_W = GLA_HEADS * GLA_DV
ATTN_Q_W = N_DIL_GROUPS * ATTN_KV_HEADS * ATTN_HEAD_DIM
ATTN_KV_W = ATTN_KV_HEADS * ATTN_HEAD_DIM
IN_SIZES = (GLA_QK_W, GLA_QK_W, GLA_V_W, GLA_V_W, GLA_RANK, GLA_RANK,
            ATTN_Q_W, ATTN_KV_W, ATTN_KV_W, 2 * D_MODEL)
IN_COLS = sum(IN_SIZES)

kernel_name = "hybrid_gla_dilated_alibi_macaron_encoder"


def rms_norm(x, g):
    xf = x.astype(jnp.float32)
    y = xf * lax.rsqrt(jnp.mean(xf * xf, axis=-1, keepdims=True) + NORM_EPS)
    return (y * g.astype(jnp.float32)).astype(x.dtype)


def swiglu(x, w_gate, w_up, w_down):
    return (jax.nn.silu(x @ w_gate) * (x @ w_up)) @ w_down


def alibi_slopes(n):
    return jnp.exp2(-ALIBI_MAX_BIAS * jnp.arange(1, n + 1, dtype=jnp.float32) / n)


def gla_direction(q, k, v, log_a, strict):
    B, T, H, DK = q.shape
    DV = v.shape[-1]
    N = T // GLA_CHUNK

    def chunks(x):
        return x.reshape(B, N, GLA_CHUNK, H, x.shape[-1]).transpose(1, 0, 3, 2, 4)

    mask = jnp.tril(jnp.ones((GLA_CHUNK, GLA_CHUNK), jnp.float32), -1 if strict else 0)

    def step(S, inp):
        qi, ki, vi, ai = inp
        b = jnp.cumsum(ai, axis=2)
        b_last = b[:, :, -1:, :]
        q_dec = qi * jnp.exp(b)
        scores = jnp.einsum('bhid,bhjd->bhij', q_dec, ki * jnp.exp(-b)) * mask
        o = (jnp.einsum('bhij,bhjv->bhiv', scores, vi)
             + jnp.einsum('bhid,bhdv->bhiv', q_dec, S))
        S = (jnp.exp(b_last[:, :, 0, :, None]) * S
             + jnp.einsum('bhjd,bhjv->bhdv', ki * jnp.exp(b_last - b), vi))
        return S, o

    S0 = jnp.zeros((B, H, DK, DV), jnp.float32)
    _, o = lax.scan(step, S0, (chunks(q), chunks(k), chunks(v), chunks(log_a)))
    return o.transpose(1, 0, 3, 2, 4).reshape(B, T, H, DV)


def gla_branch(q, k, v, r, lr_f, lr_b, up_f, bias_f, up_b, bias_b, norm_g):
    B, T, _ = q.shape
    f32 = jnp.float32
    qh = q.astype(f32).reshape(B, T, GLA_HEADS, GLA_DK) * (GLA_DK ** -0.5)
    kh = k.astype(f32).reshape(B, T, GLA_HEADS, GLA_DK)
    vh = v.astype(f32).reshape(B, T, GLA_HEADS, GLA_DV)
    la_f = (jax.nn.log_sigmoid((lr_f @ up_f + bias_f).astype(f32)) / GLA_TAU).reshape(B, T, GLA_HEADS, GLA_DK)
    la_b = (jax.nn.log_sigmoid((lr_b @ up_b + bias_b).astype(f32)) / GLA_TAU).reshape(B, T, GLA_HEADS, GLA_DK)
    flip = lambda a: a[:, ::-1]
    o = (gla_direction(qh, kh, vh, la_f, False)
         + flip(gla_direction(flip(qh), flip(kh), flip(vh), flip(la_b), True)))
    o = o * lax.rsqrt(jnp.mean(o * o, axis=-1, keepdims=True) + NORM_EPS)
    o = o.reshape(B, T, GLA_V_W) * norm_g.astype(f32)
    return (o * jax.nn.silu(r.astype(f32))).astype(q.dtype)


def dilated_group_attention(q, k, v, window, dilation, slopes):
    B, T, H, E = q.shape
    f32 = jnp.float32
    R = window // (2 * dilation)
    L = T // dilation
    nb = -(-L // R)
    Lp = nb * R

    def by_residue(x):
        return x.reshape(B, L, dilation, H, E).transpose(0, 2, 1, 3, 4)

    qs = jnp.pad(by_residue(q), ((0, 0), (0, 0), (0, Lp - L), (0, 0), (0, 0))).reshape(B, dilation, nb, R, H, E)

    def windows(x):
        xp = jnp.pad(by_residue(x), ((0, 0), (0, 0), (R, Lp - L + R), (0, 0), (0, 0)))
        xp = xp.reshape(B, dilation, nb + 2, R, H, E)
        return jnp.concatenate([xp[:, :, :-2], xp[:, :, 1:-1], xp[:, :, 2:]], axis=3)

    kw, vw = windows(k), windows(v)
    n_q = jnp.arange(Lp).reshape(nb, R)
    n_k = jnp.arange(nb)[:, None] * R - R + jnp.arange(3 * R)[None, :]
    rel = jnp.abs(n_k[:, None, :] - n_q[:, :, None])
    valid = (rel <= R) & ((n_k >= 0) & (n_k < L))[:, None, :]
    bias = -slopes.astype(f32)[None, :, None, None] * (dilation * rel).astype(f32)[:, None]
    scores = jnp.einsum('bgnqhe,bgnkhe->bgnhqk', qs.astype(f32), kw.astype(f32)) * (E ** -0.5) + bias
    scores = jnp.where(valid[:, None], scores, NEG_INF)
    m = jnp.max(scores, axis=-1, keepdims=True)
    p = jnp.exp(scores - m)
    s = jnp.sum(p, axis=-1, keepdims=True)
    lse = jnp.transpose((m + jnp.log(s))[..., 0], (0, 1, 2, 4, 3))
    o = jnp.einsum('bgnhqk,bgnkhe->bgnqhe', p, vw.astype(f32))
    o = o / jnp.transpose(s, (0, 1, 2, 4, 3, 5))

    def back(x):
        x = x.reshape((B, dilation, Lp) + x.shape[4:])[:, :, :L]
        return jnp.moveaxis(x, 1, 2).reshape((B, T) + x.shape[3:])

    return back(o), back(lse)


def dilated_attention_branch(q, k, v):
    B, T, _ = q.shape
    qh = q.reshape(B, T, N_DIL_GROUPS, ATTN_KV_HEADS, ATTN_HEAD_DIM)
    kh = k.reshape(B, T, ATTN_KV_HEADS, ATTN_HEAD_DIM)
    vh = v.reshape(B, T, ATTN_KV_HEADS, ATTN_HEAD_DIM)
    slopes = alibi_slopes(N_DIL_GROUPS * ATTN_KV_HEADS)
    outs, lses = [], []
    for g in range(N_DIL_GROUPS):
        o_g, lse_g = dilated_group_attention(qh[:, :, g], kh, vh, DIL_WINDOWS[g], DIL_RATES[g],
                                             slopes[g * ATTN_KV_HEADS:(g + 1) * ATTN_KV_HEADS])
        outs.append(o_g)
        lses.append(lse_g)
    wts = jax.nn.softmax(jnp.stack(lses, axis=0), axis=0)
    o = jnp.sum(wts[..., None] * jnp.stack(outs, axis=0), axis=0)
    return o.reshape(B, T, ATTN_KV_W).astype(q.dtype)


def mixer(h, w_in, up_f, bias_f, up_b, bias_b, gla_norm_g, proj_gla, proj_attn, w_out):
    split_points = tuple(int(c) for c in np.cumsum(IN_SIZES)[:-1])
    z = h @ w_in
    gq, gk, gv, gr, lr_f, lr_b, aq, ak, av, gate_logits = jnp.split(z, split_points, axis=-1)
    o_gla = gla_branch(gq, gk, gv, gr, lr_f, lr_b, up_f, bias_f, up_b, bias_b, gla_norm_g)
    o_att = dilated_attention_branch(aq, ak, av)
    g_gla, g_att = jnp.split(jax.nn.sigmoid(gate_logits), 2, axis=-1)
    merged = g_gla * (o_gla @ proj_gla) + g_att * (o_att @ proj_attn)
    return merged @ w_out


def encoder_trunk(x, p):
    (f1_pre, f1_wg, f1_wu, f1_wd, f1_post, mix_pre, w_in, up_f, b_f, up_b, b_b, gla_g,
     proj_gla, proj_attn, w_out, mix_post, f2_pre, f2_wg, f2_wu, f2_wd, f2_post) = p
    for l in range(DEPTH):
        x = x + 0.5 * rms_norm(swiglu(rms_norm(x, f1_pre[l]), f1_wg[l], f1_wu[l], f1_wd[l]), f1_post[l])
        x = x + rms_norm(mixer(rms_norm(x, mix_pre[l]), w_in[l], up_f[l], b_f[l], up_b[l], b_b[l],
                               gla_g[l], proj_gla[l], proj_attn[l], w_out[l]), mix_post[l])
        x = x + 0.5 * rms_norm(swiglu(rms_norm(x, f2_pre[l]), f2_wg[l], f2_wu[l], f2_wd[l]), f2_post[l])
    return x


def setup_inputs(seed: int = 0) -> dict:
    key = jax.random.key(seed)
    ks = jax.random.split(key, 24)
    f32 = jnp.float32
    L, D = DEPTH, D_MODEL

    def w(k, shape, fan_in):
        return jax.random.normal(k, shape, f32) * (fan_in ** -0.5)

    def gain(k, shape):
        return 1.0 + 0.02 * jax.random.normal(k, shape, f32)

    def small(k, shape):
        return 0.1 * jax.random.normal(k, shape, f32)

    return {
        "x_prompt": jax.random.normal(ks[0], (BATCH, SEQ, D), f32),
        "x_sample": jax.random.normal(ks[1], (DEC_BATCH, DEC_SEQ, D), f32),
        "ffn1_pre_g": gain(ks[2], (L, D)),
        "ffn1_w_gate": w(ks[3], (L, D, D_FF), D),
        "ffn1_w_up": w(ks[4], (L, D, D_FF), D),
        "ffn1_w_down": w(ks[5], (L, D_FF, D), D_FF),
        "ffn1_post_g": gain(ks[6], (L, D)),
        "mix_pre_g": gain(ks[7], (L, D)),
        "w_in": w(ks[8], (L, D, IN_COLS), D),
        "gla_decay_up_fwd": w(ks[9], (L, GLA_RANK, GLA_QK_W), GLA_RANK),
        "gla_decay_bias_fwd": small(ks[10], (L, GLA_QK_W)),
        "gla_decay_up_bwd": w(ks[11], (L, GLA_RANK, GLA_QK_W), GLA_RANK),
        "gla_decay_bias_bwd": small(ks[12], (L, GLA_QK_W)),
        "gla_norm_g": gain(ks[13], (L, GLA_V_W)),
        "proj_gla": w(ks[14], (L, GLA_V_W, D), GLA_V_W),
        "proj_attn": w(ks[15], (L, ATTN_KV_W, D), ATTN_KV_W),
        "w_out": w(ks[16], (L, D, D), D),
        "mix_post_g": gain(ks[17], (L, D)),
        "ffn2_pre_g": gain(ks[18], (L, D)),
        "ffn2_w_gate": w(ks[19], (L, D, D_FF), D),
        "ffn2_w_up": w(ks[20], (L, D, D_FF), D),
        "ffn2_w_down": w(ks[21], (L, D_FF, D), D_FF),
        "ffn2_post_g": gain(ks[22], (L, D)),
    }


def reference(x_prompt, x_sample, ffn1_pre_g, ffn1_w_gate, ffn1_w_up, ffn1_w_down, ffn1_post_g,
              mix_pre_g, w_in, gla_decay_up_fwd, gla_decay_bias_fwd, gla_decay_up_bwd,
              gla_decay_bias_bwd, gla_norm_g, proj_gla, proj_attn, w_out, mix_post_g,
              ffn2_pre_g, ffn2_w_gate, ffn2_w_up, ffn2_w_down, ffn2_post_g):
    params = (ffn1_pre_g, ffn1_w_gate, ffn1_w_up, ffn1_w_down, ffn1_post_g,
              mix_pre_g, w_in, gla_decay_up_fwd, gla_decay_bias_fwd, gla_decay_up_bwd,
              gla_decay_bias_bwd, gla_norm_g, proj_gla, proj_attn, w_out, mix_post_g,
              ffn2_pre_g, ffn2_w_gate, ffn2_w_up, ffn2_w_down, ffn2_post_g)
    y_prompt = encoder_trunk(x_prompt, params)
    y_sample = encoder_trunk(x_sample, params)
    return (y_prompt, y_sample)
```

```python
import functools

import jax
import jax.numpy as jnp
from jax import lax
from jax.experimental import pallas as pl
from jax.experimental.pallas import tpu as pltpu

F32 = jnp.float32
BF16 = jnp.bfloat16

NORM_EPS = 1e-6
NEG_INF = -1e30

GLA_HEADS = 4
GLA_RANK = 16
GLA_TAU = 16.0
GLA_CHUNK = 64
ATTN_KV_HEADS = 8
ATTN_HEAD_DIM = 128
DIL_WINDOWS = (128, 512, 2048)
DIL_RATES = (1, 4, 16)
N_DIL_GROUPS = 3
ALIBI_MAX_BIAS = 8.0

LANES = 128
VMEM_LIMIT_CAP = 60 * 1024 * 1024


def _cparams(dims, vmem_bytes):
    return pltpu.CompilerParams(dimension_semantics=dims,
                                vmem_limit_bytes=int(min(vmem_bytes * 5 // 4, VMEM_LIMIT_CAP)))


def _any_equal(idx, values):
    return functools.reduce(jnp.logical_or, [idx == v for v in values])


def _rmsnorm_kernel(x_ref, g_ref, o_ref):
    x = x_ref[...]
    ms = jnp.mean(x * x, axis=-1, keepdims=True)
    o_ref[...] = (x * lax.rsqrt(ms + NORM_EPS) * g_ref[...]).astype(o_ref.dtype)


def rmsnorm(x, g, tr=256):
    m, d = x.shape
    return pl.pallas_call(
        _rmsnorm_kernel,
        grid=(m // tr,),
        in_specs=[pl.BlockSpec((tr, d), lambda i: (i, 0)),
                  pl.BlockSpec((1, d), lambda i: (0, 0))],
        out_specs=pl.BlockSpec((tr, d), lambda i: (i, 0)),
        out_shape=jax.ShapeDtypeStruct((m, d), BF16),
        compiler_params=_cparams(("parallel",), 6 * tr * d * 4),
        name="rmsnorm",
    )(x, g.reshape(1, d))


def _norm_residual_kernel(y_ref, x_ref, gp_ref, gn_ref, xo_ref, xn_ref, *, scale):
    y = y_ref[...].astype(F32)
    ms = jnp.mean(y * y, axis=-1, keepdims=True)
    xo = x_ref[...] + scale * (y * lax.rsqrt(ms + NORM_EPS) * gp_ref[...])
    xo_ref[...] = xo
    ms2 = jnp.mean(xo * xo, axis=-1, keepdims=True)
    xn_ref[...] = (xo * lax.rsqrt(ms2 + NORM_EPS) * gn_ref[...]).astype(xn_ref.dtype)


def norm_residual(y, x, g_post, g_next, scale, tr=256):
    m, d = x.shape
    return pl.pallas_call(
        functools.partial(_norm_residual_kernel, scale=scale),
        grid=(m // tr,),
        in_specs=[pl.BlockSpec((tr, d), lambda i: (i, 0)),
                  pl.BlockSpec((tr, d), lambda i: (i, 0)),
                  pl.BlockSpec((1, d), lambda i: (0, 0)),
                  pl.BlockSpec((1, d), lambda i: (0, 0))],
        out_specs=[pl.BlockSpec((tr, d), lambda i: (i, 0)),
                   pl.BlockSpec((tr, d), lambda i: (i, 0))],
        out_shape=[jax.ShapeDtypeStruct((m, d), F32),
                   jax.ShapeDtypeStruct((m, d), BF16)],
        compiler_params=_cparams(("parallel",), 12 * tr * d * 4),
        name="norm_residual",
    )(y, x, g_post.reshape(1, d), g_next.reshape(1, d))


def _mm_kernel(a_ref, w_ref, o_ref):
    o_ref[...] = jnp.dot(a_ref[...], w_ref[...],
                         preferred_element_type=F32).astype(o_ref.dtype)


def _mm_acc_kernel(a_ref, w_ref, o_ref, acc_ref):
    k = pl.program_id(2)

    @pl.when(k == 0)
    def _():
        acc_ref[...] = jnp.zeros_like(acc_ref)

    acc_ref[...] += jnp.dot(a_ref[...], w_ref[...], preferred_element_type=F32)

    @pl.when(k == pl.num_programs(2) - 1)
    def _():
        o_ref[...] = acc_ref[...].astype(o_ref.dtype)


def matmul(a, w, out_dtype, tm=1024, tn=1024, tk=None, name="matmul"):
    m, kd = a.shape
    n = w.shape[1]
    tn = min(tn, n)
    osz = jnp.dtype(out_dtype).itemsize
    if tk is None or tk == kd:
        return pl.pallas_call(
            _mm_kernel,
            grid=(m // tm, n // tn),
            in_specs=[pl.BlockSpec((tm, kd), lambda i, j: (i, 0)),
                      pl.BlockSpec((kd, tn), lambda i, j: (0, j))],
            out_specs=pl.BlockSpec((tm, tn), lambda i, j: (i, j)),
            out_shape=jax.ShapeDtypeStruct((m, n), out_dtype),
            compiler_params=_cparams(("parallel", "arbitrary"),
                                     4 * (tm * kd + kd * tn) + 2 * tm * tn * osz + 3 * tm * tn * 4),
            name=name,
        )(a, w)
    return pl.pallas_call(
        _mm_acc_kernel,
        grid=(m // tm, n // tn, kd // tk),
        in_specs=[pl.BlockSpec((tm, tk), lambda i, j, k: (i, k)),
                  pl.BlockSpec((tk, tn), lambda i, j, k: (k, j))],
        out_specs=pl.BlockSpec((tm, tn), lambda i, j, k: (i, j)),
        out_shape=jax.ShapeDtypeStruct((m, n), out_dtype),
        scratch_shapes=[pltpu.VMEM((tm, tn), F32)],
        compiler_params=_cparams(("parallel", "arbitrary", "arbitrary"),
                                 4 * (tm * tk + tk * tn) + 2 * tm * tn * osz + 4 * tm * tn * 4),
        name=name,
    )(a, w)


def _ffn_up_kernel(x_ref, wg_ref, wu_ref, o_ref):
    x = x_ref[...]
    g = jnp.dot(x, wg_ref[...], preferred_element_type=F32)
    u = jnp.dot(x, wu_ref[...], preferred_element_type=F32)
    o_ref[...] = (g * jax.nn.sigmoid(g) * u).astype(o_ref.dtype)


def ffn_up(xn, wg, wu, tm=1024, tn=512):
    m, d = xn.shape
    f = wg.shape[1]
    return pl.pallas_call(
        _ffn_up_kernel,
        grid=(m // tm, f // tn),
        in_specs=[pl.BlockSpec((tm, d), lambda i, j: (i, 0)),
                  pl.BlockSpec((d, tn), lambda i, j: (0, j)),
                  pl.BlockSpec((d, tn), lambda i, j: (0, j))],
        out_specs=pl.BlockSpec((tm, tn), lambda i, j: (i, j)),
        out_shape=jax.ShapeDtypeStruct((m, f), BF16),
        compiler_params=_cparams(("parallel", "arbitrary"),
                                 4 * tm * d + 8 * d * tn + 4 * tm * tn + 16 * tm * tn),
        name="ffn_up",
    )(xn, wg, wu)


def _gla_kernel(q_ref, k_ref, v_ref, lr_ref, up_ref, bias_ref, o_ref, st_ref, *,
                chunks_per_block, start_blocks, end_blocks, dk):
    direction = pl.program_id(0)
    i = pl.program_id(2)
    nblk = pl.num_programs(2)
    fwd = direction == 0
    blk = jnp.where(fwd, i, nblk - 1 - i)

    reset = jnp.where(fwd, _any_equal(blk, start_blocks), _any_equal(blk + 1, end_blocks))

    @pl.when(reset)
    def _():
        st_ref[...] = jnp.zeros_like(st_ref)

    c = GLA_CHUNK
    row = lax.broadcasted_iota(jnp.int32, (c, c), 0)
    col = lax.broadcasted_iota(jnp.int32, (c, c), 1)
    diff = (row - col) * (1 - 2 * direction)
    tri = (diff >= 0).astype(BF16)
    mask = diff >= direction
    up = up_ref[...].astype(BF16)
    bias = bias_ref[...]
    nt = (((1,), (1,)), ((), ()))
    tn = (((0,), (0,)), ((), ()))

    for ci in range(chunks_per_block):
        c_eff = jnp.where(fwd, ci, chunks_per_block - 1 - ci)
        r0 = pl.multiple_of(c_eff * c, c)
        q = q_ref[pl.ds(r0, c), :].astype(F32)
        k = k_ref[pl.ds(r0, c), :].astype(F32)
        v = v_ref[pl.ds(r0, c), :]
        lr = lr_ref[pl.ds(r0, c), :].astype(BF16)
        x = jnp.dot(lr, up, preferred_element_type=F32) + bias
        la = (jnp.minimum(x, 0.0) - jnp.log1p(jnp.exp(-jnp.abs(x)))) * (1.0 / GLA_TAU)
        la_hi = la.astype(BF16)
        la_lo = (la - la_hi.astype(F32)).astype(BF16)
        b = (jnp.dot(tri, la_hi, preferred_element_type=F32)
             + jnp.dot(tri, la_lo, preferred_element_type=F32))
        tot = jnp.sum(la, axis=0, keepdims=True)
        qd = (q * (dk ** -0.5) * jnp.exp(b)).astype(BF16)
        kd = (k * jnp.exp(-b)).astype(BF16)
        ks = (k * jnp.exp(tot - b)).astype(BF16)
        scores = lax.dot_general(qd, kd, nt, preferred_element_type=F32)
        scores = jnp.where(mask, scores, 0.0).astype(BF16)
        st = st_ref[...]
        o = (jnp.dot(scores, v, preferred_element_type=F32)
             + lax.dot_general(qd, st.astype(BF16), nt, preferred_element_type=F32))
        o_ref[pl.ds(r0, c), :] = o
        st_ref[...] = st * jnp.exp(tot) + lax.dot_general(v, ks, tn, preferred_element_type=F32)


def gla_scan(z, lr2, up2, bias2, seq_bounds, dk, dv, tb=256):
    m = z.shape[0]
    h = GLA_HEADS
    nblk = m // tb
    starts = tuple(s // tb for s in seq_bounds[:-1])
    ends = tuple(s // tb for s in seq_bounds[1:])

    def tok(d, i):
        return jnp.where(d == 0, i, nblk - 1 - i)

    k_off = (h * dk) // dk
    v_off = (2 * h * dk) // dv
    kern = functools.partial(_gla_kernel, chunks_per_block=tb // GLA_CHUNK,
                             start_blocks=starts, end_blocks=ends, dk=dk)
    return pl.pallas_call(
        kern,
        grid=(2, h, nblk),
        in_specs=[pl.BlockSpec((tb, dk), lambda d, hh, i: (tok(d, i), hh)),
                  pl.BlockSpec((tb, dk), lambda d, hh, i: (tok(d, i), k_off + hh)),
                  pl.BlockSpec((tb, dv), lambda d, hh, i: (tok(d, i), v_off + hh)),
                  pl.BlockSpec((None, tb, GLA_RANK), lambda d, hh, i: (d, tok(d, i), 0)),
                  pl.BlockSpec((None, GLA_RANK, dk), lambda d, hh, i: (d, 0, hh)),
                  pl.BlockSpec((None, 1, dk), lambda d, hh, i: (d, 0, hh))],
        out_specs=pl.BlockSpec((None, tb, dv), lambda d, hh, i: (d, tok(d, i), hh)),
        out_shape=jax.ShapeDtypeStruct((2, m, h * dv), F32),
        scratch_shapes=[pltpu.VMEM((dv, dk), F32)],
        compiler_params=_cparams(("arbitrary", "arbitrary", "arbitrary"), 32 * 1024 * 1024),
        name="gla_scan",
    )(z, z, z, lr2, up2, bias2)


def _dil_kernel(q_ref, kp_ref, ko_ref, kn_ref, vp_ref, vo_ref, vn_ref, o_ref, l_ref, *,
                dilation, group, start_blocks, end_blocks):
    nb = pl.program_id(0)
    has_prev = jnp.logical_not(_any_equal(nb, start_blocks))
    has_next = jnp.logical_not(_any_equal(nb + 1, end_blocks))
    r = q_ref.shape[0]
    e = ATTN_HEAD_DIM
    qi = lax.broadcasted_iota(jnp.int32, (r, r), 0)
    kj = lax.broadcasted_iota(jnp.int32, (r, r), 1)
    rel_p = (qi + r - kj).astype(F32) * float(dilation)
    rel_o = jnp.abs(kj - qi).astype(F32) * float(dilation)
    rel_n = (kj + r - qi).astype(F32) * float(dilation)
    valid_p = jnp.logical_and(kj >= qi, has_prev)
    valid_n = jnp.logical_and(kj <= qi, has_next)
    nt = (((1,), (1,)), ((), ()))
    n_heads = N_DIL_GROUPS * ATTN_KV_HEADS
    scale = e ** -0.5

    for h in range(ATTN_KV_HEADS):
        slope = 2.0 ** (-ALIBI_MAX_BIAS * (group * ATTN_KV_HEADS + h + 1) / n_heads)
        sl = slice(h * e, (h + 1) * e)
        q = q_ref[:, sl]
        sp = lax.dot_general(q, kp_ref[:, sl], nt, preferred_element_type=F32) * scale - slope * rel_p
        so = lax.dot_general(q, ko_ref[:, sl], nt, preferred_element_type=F32) * scale - slope * rel_o
        sn = lax.dot_general(q, kn_ref[:, sl], nt, preferred_element_type=F32) * scale - slope * rel_n
        sp = jnp.where(valid_p, sp, NEG_INF)
        sn = jnp.where(valid_n, sn, NEG_INF)
        mx = jnp.maximum(jnp.maximum(jnp.max(sp, axis=-1, keepdims=True),
                                     jnp.max(so, axis=-1, keepdims=True)),
                         jnp.max(sn, axis=-1, keepdims=True))
        pp = jnp.exp(sp - mx)
        po = jnp.exp(so - mx)
        pn = jnp.exp(sn - mx)
        s = (jnp.sum(pp, axis=-1, keepdims=True) + jnp.sum(po, axis=-1, keepdims=True)
             + jnp.sum(pn, axis=-1, keepdims=True))
        o = (jnp.dot(pp.astype(BF16), vp_ref[:, sl], preferred_element_type=F32)
             + jnp.dot(po.astype(BF16), vo_ref[:, sl], preferred_element_type=F32)
             + jnp.dot(pn.astype(BF16), vn_ref[:, sl], preferred_element_type=F32))
        o_ref[:, sl] = (o / s).astype(o_ref.dtype)
        l_ref[:, sl] = jnp.broadcast_to(mx + jnp.log(s), (r, e))


def dilated_attention_group(z, group, seq_bounds, q_col, k_col, v_col):
    m, cols = z.shape
    dilation = DIL_RATES[group]
    r = DIL_WINDOWS[group] // (2 * dilation)
    kvw = ATTN_KV_HEADS * ATTN_HEAD_DIM
    cb = cols // kvw
    nblocks = m // dilation // r
    starts = tuple(s // (dilation * r) for s in seq_bounds[:-1])
    ends = tuple(s // (dilation * r) for s in seq_bounds[1:])
    zv = z.reshape(m // dilation, dilation * cols)

    def prev(nb):
        return jnp.maximum(nb - 1, 0)

    def nxt(nb):
        return jnp.minimum(nb + 1, nblocks - 1)

    blk = (r, kvw)
    kern = functools.partial(_dil_kernel, dilation=dilation, group=group,
                             start_blocks=starts, end_blocks=ends)
    o, lse = pl.pallas_call(
        kern,
        grid=(nblocks, dilation),
        in_specs=[pl.BlockSpec(blk, lambda nb, rr: (nb, rr * cb + q_col)),
                  pl.BlockSpec(blk, lambda nb, rr: (prev(nb), rr * cb + k_col)),
                  pl.BlockSpec(blk, lambda nb, rr: (nb, rr * cb + k_col)),
                  pl.BlockSpec(blk, lambda nb, rr: (nxt(nb), rr * cb + k_col)),
                  pl.BlockSpec(blk, lambda nb, rr: (prev(nb), rr * cb + v_col)),
                  pl.BlockSpec(blk, lambda nb, rr: (nb, rr * cb + v_col)),
                  pl.BlockSpec(blk, lambda nb, rr: (nxt(nb), rr * cb + v_col))],
        out_specs=[pl.BlockSpec(blk, lambda nb, rr: (nb, rr)),
                   pl.BlockSpec(blk, lambda nb, rr: (nb, rr))],
        out_shape=[jax.ShapeDtypeStruct((m // dilation, dilation * kvw), BF16),
                   jax.ShapeDtypeStruct((m // dilation, dilation * kvw), F32)],
        compiler_params=_cparams(("parallel", "parallel"), 16 * 1024 * 1024),
        name=f"dilated_attn_g{group}",
    )(zv, zv, zv, zv, zv, zv, zv)
    return o.reshape(m, kvw), lse.reshape(m, kvw)


def _mixer_post_kernel(of_ref, ob_ref, r_ref, g_ref, o0_ref, o1_ref, o2_ref,
                       l0_ref, l1_ref, l2_ref, ogla_ref, oatt_ref, *, dv):
    o = of_ref[...] + ob_ref[...]
    r = r_ref[...].astype(F32)
    gate = r * jax.nn.sigmoid(r) * g_ref[...]
    for h in range(GLA_HEADS):
        sl = slice(h * dv, (h + 1) * dv)
        oh = o[:, sl]
        ms = jnp.mean(oh * oh, axis=-1, keepdims=True)
        ogla_ref[:, sl] = (oh * lax.rsqrt(ms + NORM_EPS) * gate[:, sl]).astype(ogla_ref.dtype)
    l0, l1, l2 = l0_ref[...], l1_ref[...], l2_ref[...]
    mx = jnp.maximum(jnp.maximum(l0, l1), l2)
    w0, w1, w2 = jnp.exp(l0 - mx), jnp.exp(l1 - mx), jnp.exp(l2 - mx)
    num = (w0 * o0_ref[...].astype(F32) + w1 * o1_ref[...].astype(F32)
           + w2 * o2_ref[...].astype(F32))
    oatt_ref[...] = (num / (w0 + w1 + w2)).astype(oatt_ref.dtype)


def mixer_post(o_fb, z, r_col, gla_norm_g, outs, lses, dv, tr=256):
    m = z.shape[0]
    vw = GLA_HEADS * dv
    kvw = ATTN_KV_HEADS * ATTN_HEAD_DIM
    row = lambda w: pl.BlockSpec((tr, w), lambda i: (i, 0))
    return pl.pallas_call(
        functools.partial(_mixer_post_kernel, dv=dv),
        grid=(m // tr,),
        in_specs=[pl.BlockSpec((None, tr, vw), lambda i: (0, i, 0)),
                  pl.BlockSpec((None, tr, vw), lambda i: (1, i, 0)),
                  pl.BlockSpec((tr, vw), lambda i: (i, r_col)),
                  pl.BlockSpec((1, vw), lambda i: (0, 0)),
                  row(kvw), row(kvw), row(kvw), row(kvw), row(kvw), row(kvw)],
        out_specs=[row(vw), row(kvw)],
        out_shape=[jax.ShapeDtypeStruct((m, vw), BF16),
                   jax.ShapeDtypeStruct((m, kvw), BF16)],
        compiler_params=_cparams(("parallel",), 32 * 1024 * 1024),
        name="mixer_post",
    )(o_fb, o_fb, z, gla_norm_g.reshape(1, vw), *outs, *lses)


def _merge_proj_kernel(og_ref, oa_ref, gg_ref, ga_ref, pg_ref, pa_ref, o_ref):
    yg = jnp.dot(og_ref[...], pg_ref[...], preferred_element_type=F32)
    ya = jnp.dot(oa_ref[...], pa_ref[...], preferred_element_type=F32)
    gg = jax.nn.sigmoid(gg_ref[...].astype(F32))
    ga = jax.nn.sigmoid(ga_ref[...].astype(F32))
    o_ref[...] = (gg * yg + ga * ya).astype(o_ref.dtype)


def merge_proj(o_gla, o_att, z, gate_start, proj_gla, proj_attn, tm=1024, tn=512):
    m, vw = o_gla.shape
    kvw = o_att.shape[1]
    d = proj_gla.shape[1]
    nj = d // tn
    gate_col = gate_start // tn
    return pl.pallas_call(
        _merge_proj_kernel,
        grid=(m // tm, nj),
        in_specs=[pl.BlockSpec((tm, vw), lambda i, j: (i, 0)),
                  pl.BlockSpec((tm, kvw), lambda i, j: (i, 0)),
                  pl.BlockSpec((tm, tn), lambda i, j: (i, gate_col + j)),
                  pl.BlockSpec((tm, tn), lambda i, j: (i, gate_col + nj + j)),
                  pl.BlockSpec((vw, tn), lambda i, j: (0, j)),
                  pl.BlockSpec((kvw, tn), lambda i, j: (0, j))],
        out_specs=pl.BlockSpec((tm, tn), lambda i, j: (i, j)),
        out_shape=jax.ShapeDtypeStruct((m, d), BF16),
        compiler_params=_cparams(("parallel", "arbitrary"), 48 * 1024 * 1024),
        name="merge_proj",
    )(o_gla, o_att, z, z, proj_gla, proj_attn)


def _pad_to(x, axis, size):
    pad = [(0, 0)] * x.ndim
    pad[axis] = (0, size - x.shape[axis])
    return jnp.pad(x, pad)


def kernel(x_prompt, x_sample, ffn1_pre_g, ffn1_w_gate, ffn1_w_up, ffn1_w_down, ffn1_post_g,
           mix_pre_g, w_in, gla_decay_up_fwd, gla_decay_bias_fwd, gla_decay_up_bwd,
           gla_decay_bias_bwd, gla_norm_g, proj_gla, proj_attn, w_out, mix_post_g,
           ffn2_pre_g, ffn2_w_gate, ffn2_w_up, ffn2_w_down, ffn2_post_g):
    depth, d, d_ff = ffn1_w_gate.shape
    bp, tp, _ = x_prompt.shape
    bs, ts, _ = x_sample.shape
    mp, ms = bp * tp, bs * ts
    seq_bounds = tuple(b * tp for b in range(bp)) + tuple(mp + b * ts for b in range(bs + 1))

    dk = d // 16
    dv = d // 8
    qk_w = GLA_HEADS * dk
    v_w = GLA_HEADS * dv
    q_w = N_DIL_GROUPS * ATTN_KV_HEADS * ATTN_HEAD_DIM
    kv_w = ATTN_KV_HEADS * ATTN_HEAD_DIM
    lr0 = 2 * qk_w + 2 * v_w
    a0 = lr0 + 2 * GLA_RANK
    ff_pad = -(-d_ff // 1024) * 1024

    x = jnp.concatenate([x_prompt.reshape(mp, d), x_sample.reshape(ms, d)], axis=0)
    xn = rmsnorm(x, ffn1_pre_g[0])

    def ffn(x, xn, wg, wu, wd, g_post, g_next):
        wg = _pad_to(wg, 1, ff_pad).astype(BF16)
        wu = _pad_to(wu, 1, ff_pad).astype(BF16)
        wd = _pad_to(wd, 0, ff_pad).astype(BF16)
        hmid = ffn_up(xn, wg, wu)
        y = matmul(hmid, wd, BF16, tk=ff_pad // 4, name="ffn_down")
        return norm_residual(y, x, g_post, g_next, 0.5)

    for l in range(depth):
        x, xn = ffn(x, xn, ffn1_w_gate[l], ffn1_w_up[l], ffn1_w_down[l],
                    ffn1_post_g[l], mix_pre_g[l])

        w_main = jnp.concatenate([w_in[l][:, :lr0], w_in[l][:, a0:]], axis=1).astype(BF16)
        w_lr = _pad_to(w_in[l][:, lr0:a0], 1, LANES).astype(BF16)
        z = matmul(xn, w_main, BF16, name="mixer_in")
        lr = matmul(xn, w_lr, F32, name="mixer_in_lr")
        lr2 = jnp.stack([lr[:, :GLA_RANK], lr[:, GLA_RANK:2 * GLA_RANK]])
        up2 = jnp.stack([gla_decay_up_fwd[l], gla_decay_up_bwd[l]])
        bias2 = jnp.stack([gla_decay_bias_fwd[l], gla_decay_bias_bwd[l]]).reshape(2, 1, qk_w)
        o_fb = gla_scan(z, lr2, up2, bias2, seq_bounds, dk, dv)

        att0 = lr0 // kv_w
        outs, lses = [], []
        for g in range(N_DIL_GROUPS):
            o_g, l_g = dilated_attention_group(z, g, seq_bounds, att0 + g,
                                               att0 + N_DIL_GROUPS, att0 + N_DIL_GROUPS + 1)
            outs.append(o_g)
            lses.append(l_g)

        o_gla, o_att = mixer_post(o_fb, z, (2 * qk_w + v_w) // v_w, gla_norm_g[l], outs, lses, dv)
        merged = merge_proj(o_gla, o_att, z, lr0 + q_w + 2 * kv_w, proj_gla[l].astype(BF16),
                            proj_attn[l].astype(BF16))
        y = matmul(merged, w_out[l].astype(BF16), BF16, name="mixer_out")
        x, xn = norm_residual(y, x, mix_post_g[l], ffn2_pre_g[l], 1.0)

        g_next = ffn1_pre_g[l + 1] if l + 1 < depth else jnp.ones((d,), F32)
        x, xn = ffn(x, xn, ffn2_w_gate[l], ffn2_w_up[l], ffn2_w_down[l], ffn2_post_g[l], g_next)

    return (x[:mp].reshape(bp, tp, d), x[mp:].reshape(bs, ts, d))
```

```python
import functools

import jax
import jax.numpy as jnp
from jax import lax
from jax.experimental import pallas as pl
from jax.experimental.pallas import tpu as pltpu

F32 = jnp.float32
BF16 = jnp.bfloat16

NORM_EPS = 1e-6
NEG_INF = -1e30

GLA_HEADS = 4
GLA_RANK = 16
GLA_TAU = 16.0
GLA_CHUNK = 64
ATTN_KV_HEADS = 8
ATTN_HEAD_DIM = 128
ATTN_KV_W = ATTN_KV_HEADS * ATTN_HEAD_DIM
DIL_WINDOWS = (128, 512, 2048)
DIL_RATES = (1, 4, 16)
N_DIL_GROUPS = 3
ALIBI_MAX_BIAS = 8.0

LANES = 128
VMEM_LIMIT_CAP = 60 * 1024 * 1024


def _cparams(dims, vmem_bytes):
    return pltpu.CompilerParams(dimension_semantics=dims,
                                vmem_limit_bytes=int(min(vmem_bytes * 5 // 4, VMEM_LIMIT_CAP)))


def _any_equal(idx, values):
    return functools.reduce(jnp.logical_or, [idx == v for v in values])


def _rmsnorm_kernel(x_ref, g_ref, o_ref):
    x = x_ref[...]
    ms = jnp.mean(x * x, axis=-1, keepdims=True)
    o_ref[...] = (x * lax.rsqrt(ms + NORM_EPS) * g_ref[...]).astype(o_ref.dtype)


def rmsnorm(x, g, tr=256):
    m, d = x.shape
    return pl.pallas_call(
        _rmsnorm_kernel,
        grid=(m // tr,),
        in_specs=[pl.BlockSpec((tr, d), lambda i: (i, 0)),
                  pl.BlockSpec((1, d), lambda i: (0, 0))],
        out_specs=pl.BlockSpec((tr, d), lambda i: (i, 0)),
        out_shape=jax.ShapeDtypeStruct((m, d), BF16),
        compiler_params=_cparams(("parallel",), 6 * tr * d * 4),
        name="rmsnorm",
    )(x, g.reshape(1, d))


def _norm_residual_kernel(y_ref, x_ref, gp_ref, *rest, scale, with_next):
    y = y_ref[...].astype(F32)
    ms = jnp.mean(y * y, axis=-1, keepdims=True)
    xo = x_ref[...] + scale * (y * lax.rsqrt(ms + NORM_EPS) * gp_ref[...])
    if with_next:
        gn_ref, xo_ref, xn_ref = rest
        ms2 = jnp.mean(xo * xo, axis=-1, keepdims=True)
        xn_ref[...] = (xo * lax.rsqrt(ms2 + NORM_EPS) * gn_ref[...]).astype(xn_ref.dtype)
    else:
        (xo_ref,) = rest
    xo_ref[...] = xo


def norm_residual(y, x, g_post, g_next, scale, tr=256):
    m, d = x.shape
    with_next = g_next is not None
    row = pl.BlockSpec((tr, d), lambda i: (i, 0))
    vec = pl.BlockSpec((1, d), lambda i: (0, 0))
    gains = [g_post.reshape(1, d)] + ([g_next.reshape(1, d)] if with_next else [])
    out = pl.pallas_call(
        functools.partial(_norm_residual_kernel, scale=scale, with_next=with_next),
        grid=(m // tr,),
        in_specs=[row, row] + [vec] * len(gains),
        out_specs=[row, row] if with_next else [row],
        out_shape=([jax.ShapeDtypeStruct((m, d), F32)]
                   + ([jax.ShapeDtypeStruct((m, d), BF16)] if with_next else [])),
        compiler_params=_cparams(("parallel",), 12 * tr * d * 4),
        name="norm_residual",
    )(y, x, *gains)
    return (out[0], out[1]) if with_next else (out[0], None)


def _mm_kernel(a_ref, w_ref, o_ref):
    o_ref[...] = jnp.dot(a_ref[...], w_ref[...],
                         preferred_element_type=F32).astype(o_ref.dtype)


def _mm_acc_kernel(a_ref, w_ref, o_ref, acc_ref):
    k = pl.program_id(2)

    @pl.when(k == 0)
    def _():
        acc_ref[...] = jnp.zeros_like(acc_ref)

    acc_ref[...] += jnp.dot(a_ref[...], w_ref[...], preferred_element_type=F32)

    @pl.when(k == pl.num_programs(2) - 1)
    def _():
        o_ref[...] = acc_ref[...].astype(o_ref.dtype)


def matmul(a, w, out_dtype, tm=1024, tn=1024, tk=None, name="matmul"):
    m, kd = a.shape
    n = w.shape[1]
    tn = min(tn, n)
    osz = jnp.dtype(out_dtype).itemsize
    if tk is None or tk == kd:
        return pl.pallas_call(
            _mm_kernel,
            grid=(m // tm, n // tn),
            in_specs=[pl.BlockSpec((tm, kd), lambda i, j: (i, 0)),
                      pl.BlockSpec((kd, tn), lambda i, j: (0, j))],
            out_specs=pl.BlockSpec((tm, tn), lambda i, j: (i, j)),
            out_shape=jax.ShapeDtypeStruct((m, n), out_dtype),
            compiler_params=_cparams(("parallel", "arbitrary"),
                                     4 * (tm * kd + kd * tn) + 2 * tm * tn * osz + 3 * tm * tn * 4),
            name=name,
        )(a, w)
    return pl.pallas_call(
        _mm_acc_kernel,
        grid=(m // tm, n // tn, kd // tk),
        in_specs=[pl.BlockSpec((tm, tk), lambda i, j, k: (i, k)),
                  pl.BlockSpec((tk, tn), lambda i, j, k: (k, j))],
        out_specs=pl.BlockSpec((tm, tn), lambda i, j, k: (i, j)),
        out_shape=jax.ShapeDtypeStruct((m, n), out_dtype),
        scratch_shapes=[pltpu.VMEM((tm, tn), F32)],
        compiler_params=_cparams(("parallel", "arbitrary", "arbitrary"),
                                 4 * (tm * tk + tk * tn) + 2 * tm * tn * osz + 4 * tm * tn * 4),
        name=name,
    )(a, w)


def _ffn_up_kernel(x_ref, wg_ref, wu_ref, o_ref):
    x = x_ref[...]
    g = jnp.dot(x, wg_ref[...], preferred_element_type=F32)
    u = jnp.dot(x, wu_ref[...], preferred_element_type=F32)
    o_ref[...] = (g * jax.nn.sigmoid(g) * u).astype(o_ref.dtype)


def ffn_up(xn, wg, wu, tm=1024, tn=512):
    m, d = xn.shape
    f = wg.shape[1]
    return pl.pallas_call(
        _ffn_up_kernel,
        grid=(m // tm, f // tn),
        in_specs=[pl.BlockSpec((tm, d), lambda i, j: (i, 0)),
                  pl.BlockSpec((d, tn), lambda i, j: (0, j)),
                  pl.BlockSpec((d, tn), lambda i, j: (0, j))],
        out_specs=pl.BlockSpec((tm, tn), lambda i, j: (i, j)),
        out_shape=jax.ShapeDtypeStruct((m, f), BF16),
        compiler_params=_cparams(("parallel", "arbitrary"),
                                 4 * tm * d + 8 * d * tn + 4 * tm * tn + 16 * tm * tn),
        name="ffn_up",
    )(xn, wg, wu)


def _gla_kernel(q_ref, k_ref, v_ref, lr_ref, up_ref, bias_ref, o_ref, st_ref, *,
                chunks_per_block, start_blocks, end_blocks, dk):
    direction = pl.program_id(0)
    i = pl.program_id(2)
    nblk = pl.num_programs(2)
    fwd = direction == 0
    blk = jnp.where(fwd, i, nblk - 1 - i)

    reset = jnp.where(fwd, _any_equal(blk, start_blocks), _any_equal(blk + 1, end_blocks))

    @pl.when(reset)
    def _():
        st_ref[...] = jnp.zeros_like(st_ref)

    c = GLA_CHUNK
    row = lax.broadcasted_iota(jnp.int32, (c, c), 0)
    col = lax.broadcasted_iota(jnp.int32, (c, c), 1)
    diff = (row - col) * (1 - 2 * direction)
    tri = (diff >= 0).astype(BF16)
    mask = diff >= direction
    up = up_ref[...].astype(BF16)
    bias = bias_ref[...]
    nt = (((1,), (1,)), ((), ()))
    tn = (((0,), (0,)), ((), ()))

    for ci in range(chunks_per_block):
        c_eff = jnp.where(fwd, ci, chunks_per_block - 1 - ci)
        r0 = pl.multiple_of(c_eff * c, c)
        q = q_ref[pl.ds(r0, c), :].astype(F32)
        k = k_ref[pl.ds(r0, c), :].astype(F32)
        v = v_ref[pl.ds(r0, c), :]
        lr = lr_ref[pl.ds(r0, c), :].astype(BF16)
        x = jnp.dot(lr, up, preferred_element_type=F32) + bias
        la = (jnp.minimum(x, 0.0) - jnp.log1p(jnp.exp(-jnp.abs(x)))) * (1.0 / GLA_TAU)
        la_hi = la.astype(BF16)
        la_lo = (la - la_hi.astype(F32)).astype(BF16)
        b = (jnp.dot(tri, la_hi, preferred_element_type=F32)
             + jnp.dot(tri, la_lo, preferred_element_type=F32))
        tot = jnp.sum(la, axis=0, keepdims=True)
        qd = (q * (dk ** -0.5) * jnp.exp(b)).astype(BF16)
        kd = (k * jnp.exp(-b)).astype(BF16)
        ks = (k * jnp.exp(tot - b)).astype(BF16)
        scores = lax.dot_general(qd, kd, nt, preferred_element_type=F32)
        scores = jnp.where(mask, scores, 0.0).astype(BF16)
        st = st_ref[...]
        o = (jnp.dot(scores, v, preferred_element_type=F32)
             + lax.dot_general(qd, st.astype(BF16), nt, preferred_element_type=F32))
        o_ref[pl.ds(r0, c), :] = o
        st_ref[...] = st * jnp.exp(tot) + lax.dot_general(v, ks, tn, preferred_element_type=F32)


def gla_scan(z, lr2, up2, bias2, seq_bounds, dk, dv, tb=512):
    m = z.shape[0]
    h = GLA_HEADS
    nblk = m // tb
    starts = tuple(s // tb for s in seq_bounds[:-1])
    ends = tuple(s // tb for s in seq_bounds[1:])

    def tok(d, i):
        return jnp.where(d == 0, i, nblk - 1 - i)

    k_off = (h * dk) // dk
    v_off = (2 * h * dk) // dv
    kern = functools.partial(_gla_kernel, chunks_per_block=tb // GLA_CHUNK,
                             start_blocks=starts, end_blocks=ends, dk=dk)
    return pl.pallas_call(
        kern,
        grid=(2, h, nblk),
        in_specs=[pl.BlockSpec((tb, dk), lambda d, hh, i: (tok(d, i), hh)),
                  pl.BlockSpec((tb, dk), lambda d, hh, i: (tok(d, i), k_off + hh)),
                  pl.BlockSpec((tb, dv), lambda d, hh, i: (tok(d, i), v_off + hh)),
                  pl.BlockSpec((None, tb, GLA_RANK), lambda d, hh, i: (d, tok(d, i), 0)),
                  pl.BlockSpec((None, GLA_RANK, dk), lambda d, hh, i: (d, 0, hh)),
                  pl.BlockSpec((None, 1, dk), lambda d, hh, i: (d, 0, hh))],
        out_specs=pl.BlockSpec((None, tb, dv), lambda d, hh, i: (d, tok(d, i), hh)),
        out_shape=jax.ShapeDtypeStruct((2, m, h * dv), F32),
        scratch_shapes=[pltpu.VMEM((dv, dk), F32)],
        compiler_params=_cparams(("arbitrary", "arbitrary", "arbitrary"), 32 * 1024 * 1024),
        name="gla_scan",
    )(z, z, z, lr2, up2, bias2)


def _attn_in_kernel(x_ref, w_ref, *refs):
    out_refs, acc_ref = refs[:-1], refs[-1]
    j = pl.program_id(1)
    w = ATTN_KV_W
    n_chunks, tm, _ = acc_ref.shape
    res = jnp.dot(x_ref[...], w_ref[...], preferred_element_type=F32)
    for c in range(n_chunks):
        acc_ref[c] = res[:, c * LANES:(c + 1) * LANES]
    for g, (o_ref, d) in enumerate(zip(out_refs, DIL_RATES)):
        @pl.when(jnp.logical_or(j == g, j >= N_DIL_GROUPS))
        def _(o_ref=o_ref, d=d):
            for r in range(d):
                for c in range(n_chunks):
                    rows = pl.ds(r, tm // d, stride=d) if d > 1 else slice(None)
                    o_ref[:, r * w + c * LANES:r * w + (c + 1) * LANES] = (
                        acc_ref[c, rows, :].astype(o_ref.dtype))


def attn_in(xn, w_att, tm=512):
    m, kd = xn.shape
    w = ATTN_KV_W
    nj = w_att.shape[1] // w
    sect = lambda i, j: (i, jnp.clip(j - (N_DIL_GROUPS - 1), 0, 2))
    return pl.pallas_call(
        _attn_in_kernel,
        grid=(m // tm, nj),
        in_specs=[pl.BlockSpec((tm, kd), lambda i, j: (i, 0)),
                  pl.BlockSpec((kd, w), lambda i, j: (0, j))],
        out_specs=[pl.BlockSpec((tm // d, d * w), sect) for d in DIL_RATES],
        out_shape=[jax.ShapeDtypeStruct((m // d, 3 * d * w), BF16) for d in DIL_RATES],
        scratch_shapes=[pltpu.VMEM((w // LANES, tm, LANES), F32)],
        compiler_params=_cparams(("parallel", "arbitrary"),
                                 4 * (tm * kd + kd * w) + 12 * tm * w + 8 * tm * w),
        name="attn_in",
    )(xn, w_att)


def _dil_kernel(q_ref, kp_ref, ko_ref, kn_ref, vp_ref, vo_ref, vn_ref, o_ref, l_ref, *,
                dilation, group, qb, start_tiles, end_tiles):
    t = pl.program_id(0)
    res = pl.program_id(1)
    has_prev = jnp.logical_not(_any_equal(t, start_tiles))
    has_next = jnp.logical_not(_any_equal(t + 1, end_tiles))
    r = DIL_WINDOWS[group] // (2 * dilation)
    e = ATTN_HEAD_DIM
    qi = lax.broadcasted_iota(jnp.int32, (r, 3 * r), 0)
    kj = lax.broadcasted_iota(jnp.int32, (r, 3 * r), 1)
    rel = jnp.abs(kj - r - qi)
    in_window = rel <= r
    rel_f = rel.astype(F32) * float(dilation)
    prev_ok = jnp.logical_or(kj >= r, has_prev)
    next_ok = jnp.logical_or(kj < 2 * r, has_next)

    def valid(j):
        v = in_window
        if j == 0:
            v = jnp.logical_and(v, prev_ok)
        if j == qb - 1:
            v = jnp.logical_and(v, next_ok)
        return v

    valids = {j: valid(j) for j in {0, qb - 1, min(1, qb - 1)}}
    nt = (((1,), (1,)), ((), ()))
    n_heads = N_DIL_GROUPS * ATTN_KV_HEADS
    scale = e ** -0.5

    for h in range(ATTN_KV_HEADS):
        slope = 2.0 ** (-ALIBI_MAX_BIAS * (group * ATTN_KV_HEADS + h + 1) / n_heads)
        sl = slice(h * e, (h + 1) * e)
        k_all = jnp.concatenate([kp_ref[:, sl], ko_ref[:, sl], kn_ref[:, sl]], axis=0)
        v_all = jnp.concatenate([vp_ref[:, sl], vo_ref[:, sl], vn_ref[:, sl]], axis=0)
        bias = slope * rel_f
        for j in range(qb):
            rows = slice(j * r, (j + 1) * r)
            q = q_ref[rows, sl]
            s = lax.dot_general(q, k_all[j * r:(j + 3) * r], nt, preferred_element_type=F32)
            s = jnp.where(valids.get(j, in_window), s * scale - bias, NEG_INF)
            mx = jnp.max(s, axis=-1, keepdims=True)
            p = jnp.exp(s - mx)
            den = jnp.sum(p, axis=-1, keepdims=True)
            o = jnp.dot(p.astype(BF16), v_all[j * r:(j + 3) * r], preferred_element_type=F32) / den
            lse = jnp.broadcast_to(mx + jnp.log(den), (r, e))
            if dilation == 1:
                dst = pl.ds(j * r, r)
            else:
                dst = pl.ds(j * r * dilation + res, r, stride=dilation)
            o_ref[h, dst, :] = o
            l_ref[h, dst, :] = lse


def dilated_attention_group(a, group, seq_bounds, qb):
    d = DIL_RATES[group]
    r = DIL_WINDOWS[group] // (2 * d)
    w = ATTN_KV_W
    rows = a.shape[0]
    m = rows * d
    tile = qb * r * d
    n_tiles = m // tile
    n_blocks = rows // r
    starts = tuple(s // tile for s in seq_bounds[:-1])
    ends = tuple(s // tile for s in seq_bounds[1:])

    own = lambda sec: pl.BlockSpec((qb * r, w), lambda t, rr: (t, sec * d + rr))
    prev = lambda sec: pl.BlockSpec((r, w), lambda t, rr: (jnp.maximum(t * qb - 1, 0), sec * d + rr))
    nxt = lambda sec: pl.BlockSpec(
        (r, w), lambda t, rr: (jnp.minimum((t + 1) * qb, n_blocks - 1), sec * d + rr))
    hm = (ATTN_KV_HEADS, m, ATTN_HEAD_DIM)
    out = pl.BlockSpec((ATTN_KV_HEADS, tile, ATTN_HEAD_DIM), lambda t, rr: (0, t, 0))
    kern = functools.partial(_dil_kernel, dilation=d, group=group, qb=qb,
                             start_tiles=starts, end_tiles=ends)
    return pl.pallas_call(
        kern,
        grid=(n_tiles, d),
        in_specs=[own(0), prev(1), own(1), nxt(1), prev(2), own(2), nxt(2)],
        out_specs=[out, out],
        out_shape=[jax.ShapeDtypeStruct(hm, F32), jax.ShapeDtypeStruct(hm, F32)],
        compiler_params=_cparams(("parallel", "arbitrary"), 16 * tile * w + 24 * qb * r * w),
        name=f"dilated_attn_g{group}",
    )(a, a, a, a, a, a, a)


def _mixer_post_kernel(of_ref, ob_ref, r_ref, g_ref, o0_ref, o1_ref, o2_ref,
                       l0_ref, l1_ref, l2_ref, ogla_ref, oatt_ref, *, dv):
    o = of_ref[...] + ob_ref[...]
    r = r_ref[...].astype(F32)
    gate = r * jax.nn.sigmoid(r) * g_ref[...]
    for h in range(GLA_HEADS):
        sl = slice(h * dv, (h + 1) * dv)
        oh = o[:, sl]
        ms = jnp.mean(oh * oh, axis=-1, keepdims=True)
        ogla_ref[:, sl] = (oh * lax.rsqrt(ms + NORM_EPS) * gate[:, sl]).astype(ogla_ref.dtype)
    e = ATTN_HEAD_DIM
    for h in range(ATTN_KV_HEADS):
        l0, l1, l2 = l0_ref[h], l1_ref[h], l2_ref[h]
        mx = jnp.maximum(jnp.maximum(l0, l1), l2)
        w0, w1, w2 = jnp.exp(l0 - mx), jnp.exp(l1 - mx), jnp.exp(l2 - mx)
        num = w0 * o0_ref[h] + w1 * o1_ref[h] + w2 * o2_ref[h]
        oatt_ref[:, h * e:(h + 1) * e] = (num / (w0 + w1 + w2)).astype(oatt_ref.dtype)


def mixer_post(o_fb, z, r_col, gla_norm_g, outs, lses, dv, tr=256):
    m = z.shape[0]
    vw = GLA_HEADS * dv
    kvw = ATTN_KV_W
    row = lambda w: pl.BlockSpec((tr, w), lambda i: (i, 0))
    heads = pl.BlockSpec((ATTN_KV_HEADS, tr, ATTN_HEAD_DIM), lambda i: (0, i, 0))
    return pl.pallas_call(
        functools.partial(_mixer_post_kernel, dv=dv),
        grid=(m // tr,),
        in_specs=[pl.BlockSpec((None, tr, vw), lambda i: (0, i, 0)),
                  pl.BlockSpec((None, tr, vw), lambda i: (1, i, 0)),
                  pl.BlockSpec((tr, vw), lambda i: (i, r_col)),
                  pl.BlockSpec((1, vw), lambda i: (0, 0))] + [heads] * 6,
        out_specs=[row(vw), row(kvw)],
        out_shape=[jax.ShapeDtypeStruct((m, vw), BF16),
                   jax.ShapeDtypeStruct((m, kvw), BF16)],
        compiler_params=_cparams(("parallel",), 36 * 1024 * 1024),
        name="mixer_post",
    )(o_fb, o_fb, z, gla_norm_g.reshape(1, vw), *outs, *lses)


def _merge_proj_kernel(og_ref, oa_ref, gg_ref, ga_ref, pg_ref, pa_ref, o_ref):
    yg = jnp.dot(og_ref[...], pg_ref[...], preferred_element_type=F32)
    ya = jnp.dot(oa_ref[...], pa_ref[...], preferred_element_type=F32)
    gg = jax.nn.sigmoid(gg_ref[...].astype(F32))
    ga = jax.nn.sigmoid(ga_ref[...].astype(F32))
    o_ref[...] = (gg * yg + ga * ya).astype(o_ref.dtype)


def merge_proj(o_gla, o_att, z, gate_start, proj_gla, proj_attn, tm=1024, tn=512):
    m, vw = o_gla.shape
    kvw = o_att.shape[1]
    d = proj_gla.shape[1]
    nj = d // tn
    gate_col = gate_start // tn
    return pl.pallas_call(
        _merge_proj_kernel,
        grid=(m // tm, nj),
        in_specs=[pl.BlockSpec((tm, vw), lambda i, j: (i, 0)),
                  pl.BlockSpec((tm, kvw), lambda i, j: (i, 0)),
                  pl.BlockSpec((tm, tn), lambda i, j: (i, gate_col + j)),
                  pl.BlockSpec((tm, tn), lambda i, j: (i, gate_col + nj + j)),
                  pl.BlockSpec((vw, tn), lambda i, j: (0, j)),
                  pl.BlockSpec((kvw, tn), lambda i, j: (0, j))],
        out_specs=pl.BlockSpec((tm, tn), lambda i, j: (i, j)),
        out_shape=jax.ShapeDtypeStruct((m, d), BF16),
        compiler_params=_cparams(("parallel", "arbitrary"), 40 * 1024 * 1024),
        name="merge_proj",
    )(o_gla, o_att, z, z, proj_gla, proj_attn)


def _pad_to(x, axis, size):
    pad = [(0, 0)] * x.ndim
    pad[axis] = (0, size - x.shape[axis])
    return jnp.pad(x, pad)


def kernel(x_prompt, x_sample, ffn1_pre_g, ffn1_w_gate, ffn1_w_up, ffn1_w_down, ffn1_post_g,
           mix_pre_g, w_in, gla_decay_up_fwd, gla_decay_bias_fwd, gla_decay_up_bwd,
           gla_decay_bias_bwd, gla_norm_g, proj_gla, proj_attn, w_out, mix_post_g,
           ffn2_pre_g, ffn2_w_gate, ffn2_w_up, ffn2_w_down, ffn2_post_g):
    depth, d, d_ff = ffn1_w_gate.shape
    bp, tp, _ = x_prompt.shape
    bs, ts, _ = x_sample.shape
    mp, ms = bp * tp, bs * ts
    seq_bounds = tuple(b * tp for b in range(bp)) + tuple(mp + b * ts for b in range(bs + 1))

    dk = d // 16
    dv = d // 8
    qk_w = GLA_HEADS * dk
    v_w = GLA_HEADS * dv
    q_w = N_DIL_GROUPS * ATTN_KV_W
    lr0 = 2 * qk_w + 2 * v_w
    a0 = lr0 + 2 * GLA_RANK
    g0 = a0 + q_w + 2 * ATTN_KV_W
    ff_pad = -(-d_ff // 1024) * 1024

    x = jnp.concatenate([x_prompt.reshape(mp, d), x_sample.reshape(ms, d)], axis=0)
    xn = rmsnorm(x, ffn1_pre_g[0])

    def ffn(x, xn, wg, wu, wd, g_post, g_next):
        wg = _pad_to(wg, 1, ff_pad).astype(BF16)
        wu = _pad_to(wu, 1, ff_pad).astype(BF16)
        wd = _pad_to(wd, 0, ff_pad).astype(BF16)
        hmid = ffn_up(xn, wg, wu)
        y = matmul(hmid, wd, BF16, tk=ff_pad // 4, name="ffn_down")
        return norm_residual(y, x, g_post, g_next, 0.5)

    for l in range(depth):
        x, xn = ffn(x, xn, ffn1_w_gate[l], ffn1_w_up[l], ffn1_w_down[l],
                    ffn1_post_g[l], mix_pre_g[l])

        w_main = jnp.concatenate([w_in[l][:, :lr0], w_in[l][:, g0:]], axis=1).astype(BF16)
        w_lr = _pad_to(w_in[l][:, lr0:a0], 1, LANES).astype(BF16)
        z = matmul(xn, w_main, BF16, name="mixer_in")
        lr = matmul(xn, w_lr, F32, name="mixer_in_lr")
        lr2 = jnp.stack([lr[:, :GLA_RANK], lr[:, GLA_RANK:2 * GLA_RANK]])
        up2 = jnp.stack([gla_decay_up_fwd[l], gla_decay_up_bwd[l]])
        bias2 = jnp.stack([gla_decay_bias_fwd[l], gla_decay_bias_bwd[l]]).reshape(2, 1, qk_w)
        o_fb = gla_scan(z, lr2, up2, bias2, seq_bounds, dk, dv)

        qkv = attn_in(xn, w_in[l][:, a0:g0].astype(BF16))
        outs, lses = [], []
        for g, qb in enumerate((8, 4, 1)):
            o_g, l_g = dilated_attention_group(qkv[g], g, seq_bounds, qb)
            outs.append(o_g)
            lses.append(l_g)

        o_gla, o_att = mixer_post(o_fb, z, (2 * qk_w + v_w) // v_w, gla_norm_g[l], outs, lses, dv)
        merged = merge_proj(o_gla, o_att, z, lr0, proj_gla[l].astype(BF16),
                            proj_attn[l].astype(BF16))
        y = matmul(merged, w_out[l].astype(BF16), BF16, name="mixer_out")
        x, xn = norm_residual(y, x, mix_post_g[l], ffn2_pre_g[l], 1.0)

        g_next = ffn1_pre_g[l + 1] if l + 1 < depth else None
        x, xn = ffn(x, xn, ffn2_w_gate[l], ffn2_w_up[l], ffn2_w_down[l], ffn2_post_g[l], g_next)

    return (x[:mp].reshape(bp, tp, d), x[mp:].reshape(bs, ts, d))
```

```python
import functools

import jax
import jax.numpy as jnp
from jax import lax
from jax.experimental import pallas as pl
from jax.experimental.pallas import tpu as pltpu

F32 = jnp.float32
BF16 = jnp.bfloat16

NORM_EPS = 1e-6
NEG_INF = -1e30

GLA_HEADS = 4
GLA_RANK = 16
GLA_TAU = 16.0
GLA_CHUNK = 64
ATTN_KV_HEADS = 8
ATTN_HEAD_DIM = 128
ATTN_KV_W = ATTN_KV_HEADS * ATTN_HEAD_DIM
DIL_WINDOWS = (128, 512, 2048)
DIL_RATES = (1, 4, 16)
N_DIL_GROUPS = 3
ALIBI_MAX_BIAS = 8.0

LANES = 128
VMEM_LIMIT_CAP = 60 * 1024 * 1024


def _cparams(dims, vmem_bytes):
    return pltpu.CompilerParams(dimension_semantics=dims,
                                vmem_limit_bytes=int(min(vmem_bytes * 5 // 4, VMEM_LIMIT_CAP)))


def _any_equal(idx, values):
    return functools.reduce(jnp.logical_or, [idx == v for v in values])


def _part_specs(n_first, tr, d):
    return [pl.BlockSpec((tr, d), lambda i: (jnp.minimum(i, n_first - 1), 0)),
            pl.BlockSpec((tr, d), lambda i: (jnp.maximum(i - n_first, 0), 0))]


def _rmsnorm_kernel(xa_ref, xb_ref, g_ref, o_ref, *, n_first):
    x = jnp.where(pl.program_id(0) < n_first, xa_ref[...], xb_ref[...])
    ms = jnp.mean(x * x, axis=-1, keepdims=True)
    o_ref[...] = (x * lax.rsqrt(ms + NORM_EPS) * g_ref[...]).astype(o_ref.dtype)


def rmsnorm(x_parts, g, tr=256):
    d = x_parts[0].shape[1]
    n_first = x_parts[0].shape[0] // tr
    m = x_parts[0].shape[0] + x_parts[1].shape[0]
    return pl.pallas_call(
        functools.partial(_rmsnorm_kernel, n_first=n_first),
        grid=(m // tr,),
        in_specs=_part_specs(n_first, tr, d) + [pl.BlockSpec((1, d), lambda i: (0, 0))],
        out_specs=pl.BlockSpec((tr, d), lambda i: (i, 0)),
        out_shape=jax.ShapeDtypeStruct((m, d), BF16),
        compiler_params=_cparams(("arbitrary",), 10 * tr * d * 4),
        name="rmsnorm",
    )(*x_parts, g.reshape(1, d))


def _norm_residual_kernel(y_ref, *refs, scale, with_next, n_in_first, n_out_first):
    i = pl.program_id(0)
    refs = list(refs)
    if n_in_first is None:
        x = refs.pop(0)[...]
    else:
        xa_ref, xb_ref = refs.pop(0), refs.pop(0)
        x = jnp.where(i < n_in_first, xa_ref[...], xb_ref[...])
    gp_ref = refs.pop(0)
    y = y_ref[...].astype(F32)
    ms = jnp.mean(y * y, axis=-1, keepdims=True)
    xo = x + scale * (y * lax.rsqrt(ms + NORM_EPS) * gp_ref[...])
    if with_next:
        gn_ref, xn_ref = refs.pop(0), refs.pop()
        ms2 = jnp.mean(xo * xo, axis=-1, keepdims=True)
        xn_ref[...] = (xo * lax.rsqrt(ms2 + NORM_EPS) * gn_ref[...]).astype(xn_ref.dtype)
    if n_out_first is None:
        refs[0][...] = xo
    else:
        @pl.when(i < n_out_first)
        def _():
            refs[0][...] = xo

        @pl.when(i >= n_out_first)
        def _():
            refs[1][...] = xo


def norm_residual(y, x, g_post, g_next, scale, out_rows=None, tr=256):
    m, d = y.shape
    with_next = g_next is not None
    row = pl.BlockSpec((tr, d), lambda i: (i, 0))
    vec = pl.BlockSpec((1, d), lambda i: (0, 0))
    gains = [g_post.reshape(1, d)] + ([g_next.reshape(1, d)] if with_next else [])
    x_parts = x if isinstance(x, (tuple, list)) else None
    n_in_first = None if x_parts is None else x_parts[0].shape[0] // tr
    n_out_first = None if out_rows is None else out_rows[0] // tr
    x_specs = [row] if x_parts is None else _part_specs(n_in_first, tr, d)
    if out_rows is None:
        xo_specs, xo_shapes = [row], [jax.ShapeDtypeStruct((m, d), F32)]
    else:
        xo_specs = _part_specs(n_out_first, tr, d)
        xo_shapes = [jax.ShapeDtypeStruct((rows, d), F32) for rows in out_rows]
    out = pl.pallas_call(
        functools.partial(_norm_residual_kernel, scale=scale, with_next=with_next,
                          n_in_first=n_in_first, n_out_first=n_out_first),
        grid=(m // tr,),
        in_specs=[row] + x_specs + [vec] * len(gains),
        out_specs=xo_specs + ([row] if with_next else []),
        out_shape=xo_shapes + ([jax.ShapeDtypeStruct((m, d), BF16)] if with_next else []),
        compiler_params=_cparams(("arbitrary",), 14 * tr * d * 4),
        name="norm_residual",
    )(y, *(x_parts if x_parts is not None else [x]), *gains)
    xo = out[0] if out_rows is None else tuple(out[:2])
    return xo, (out[-1] if with_next else None)


def _mm_kernel(a_ref, w_ref, o_ref):
    o_ref[...] = jnp.dot(a_ref[...], w_ref[...],
                         preferred_element_type=F32).astype(o_ref.dtype)


def _mm_acc_kernel(a_ref, w_ref, o_ref, acc_ref):
    k = pl.program_id(2)

    @pl.when(k == 0)
    def _():
        acc_ref[...] = jnp.zeros_like(acc_ref)

    acc_ref[...] += jnp.dot(a_ref[...], w_ref[...], preferred_element_type=F32)

    @pl.when(k == pl.num_programs(2) - 1)
    def _():
        o_ref[...] = acc_ref[...].astype(o_ref.dtype)


def matmul(a, w, out_dtype, tm=1024, tn=1024, tk=None, name="matmul"):
    m, kd = a.shape
    n = w.shape[1]
    tn = min(tn, n)
    osz = jnp.dtype(out_dtype).itemsize
    if tk is None or tk == kd:
        return pl.pallas_call(
            _mm_kernel,
            grid=(m // tm, n // tn),
            in_specs=[pl.BlockSpec((tm, kd), lambda i, j: (i, 0)),
                      pl.BlockSpec((kd, tn), lambda i, j: (0, j))],
            out_specs=pl.BlockSpec((tm, tn), lambda i, j: (i, j)),
            out_shape=jax.ShapeDtypeStruct((m, n), out_dtype),
            compiler_params=_cparams(("parallel", "arbitrary"),
                                     4 * (tm * kd + kd * tn) + 2 * tm * tn * osz + 3 * tm * tn * 4),
            name=name,
        )(a, w)
    return pl.pallas_call(
        _mm_acc_kernel,
        grid=(m // tm, n // tn, kd // tk),
        in_specs=[pl.BlockSpec((tm, tk), lambda i, j, k: (i, k)),
                  pl.BlockSpec((tk, tn), lambda i, j, k: (k, j))],
        out_specs=pl.BlockSpec((tm, tn), lambda i, j, k: (i, j)),
        out_shape=jax.ShapeDtypeStruct((m, n), out_dtype),
        scratch_shapes=[pltpu.VMEM((tm, tn), F32)],
        compiler_params=_cparams(("parallel", "arbitrary", "arbitrary"),
                                 4 * (tm * tk + tk * tn) + 2 * tm * tn * osz + 4 * tm * tn * 4),
        name=name,
    )(a, w)


def _ffn_up_kernel(x_ref, wg_ref, wu_ref, o_ref):
    x = x_ref[...]
    g = jnp.dot(x, wg_ref[...], preferred_element_type=F32)
    u = jnp.dot(x, wu_ref[...], preferred_element_type=F32)
    o_ref[...] = (g * jax.nn.sigmoid(g) * u).astype(o_ref.dtype)


def ffn_up(xn, wg, wu, tm=1024, tn=512):
    m, d = xn.shape
    f = wg.shape[1]
    return pl.pallas_call(
        _ffn_up_kernel,
        grid=(m // tm, f // tn),
        in_specs=[pl.BlockSpec((tm, d), lambda i, j: (i, 0)),
                  pl.BlockSpec((d, tn), lambda i, j: (0, j)),
                  pl.BlockSpec((d, tn), lambda i, j: (0, j))],
        out_specs=pl.BlockSpec((tm, tn), lambda i, j: (i, j)),
        out_shape=jax.ShapeDtypeStruct((m, f), BF16),
        compiler_params=_cparams(("parallel", "arbitrary"),
                                 4 * tm * d + 8 * d * tn + 4 * tm * tn + 16 * tm * tn),
        name="ffn_up",
    )(xn, wg, wu)


def _gla_kernel(q_ref, k_ref, v_ref, lr_ref, up_ref, bias_ref, o_ref, st_ref, *,
                backward, n_chunks, reset_blocks, dk):
    i = pl.program_id(1)
    blk = pl.num_programs(1) - 1 - i if backward else i

    @pl.when(_any_equal(blk, reset_blocks))
    def _():
        st_ref[...] = jnp.zeros_like(st_ref)

    c = GLA_CHUNK
    nc = n_chunks
    row = lax.broadcasted_iota(jnp.int32, (c, c), 0)
    col = lax.broadcasted_iota(jnp.int32, (c, c), 1)
    if backward:
        tri, mask = row <= col, row < col
    else:
        tri, mask = row >= col, row >= col
    tri = jnp.broadcast_to(tri.astype(BF16)[None], (nc, c, c))

    def bmm(a, b, ca, cb):
        return lax.dot_general(a, b, (((ca,), (cb,)), ((0,), (0,))), preferred_element_type=F32)

    x = jnp.dot(lr_ref[...].astype(BF16), up_ref[...].astype(BF16),
                preferred_element_type=F32) + bias_ref[...]
    la = (jnp.minimum(x, 0.0) - jnp.log1p(jnp.exp(-jnp.abs(x)))) * (1.0 / GLA_TAU)
    la = la.reshape(nc, c, dk)
    la_hi = la.astype(BF16)
    la_lo = (la - la_hi.astype(F32)).astype(BF16)
    b = bmm(tri, la_hi, 2, 1) + bmm(tri, la_lo, 2, 1)
    tot = jnp.sum(la, axis=1, keepdims=True)
    q = q_ref[...].astype(F32).reshape(nc, c, dk)
    k = k_ref[...].astype(F32).reshape(nc, c, dk)
    v = v_ref[...].reshape(nc, c, v_ref.shape[1])
    qd = (q * (dk ** -0.5) * jnp.exp(b)).astype(BF16)
    kd = (k * jnp.exp(-b)).astype(BF16)
    ks = (k * jnp.exp(tot - b)).astype(BF16)
    scores = jnp.where(mask[None], bmm(qd, kd, 2, 2), 0.0).astype(BF16)
    o_intra = bmm(scores, v, 2, 1)
    st_inc = bmm(v, ks, 1, 1)
    chunk_decay = jnp.exp(tot)

    st = st_ref[...]
    nt = (((1,), (1,)), ((), ()))
    for ci in (reversed(range(nc)) if backward else range(nc)):
        o_ref[ci * c:(ci + 1) * c, :] = o_intra[ci] + lax.dot_general(
            qd[ci], st.astype(BF16), nt, preferred_element_type=F32)
        st = st * chunk_decay[ci] + st_inc[ci]
    st_ref[...] = st


def gla_scan(z, lr, up, bias, seq_bounds, dk, dv, backward, tb=512):
    m = z.shape[0]
    h = GLA_HEADS
    nblk = m // tb
    if backward:
        resets = tuple(s // tb - 1 for s in seq_bounds[1:])
        tok = lambda i: nblk - 1 - i
    else:
        resets = tuple(s // tb for s in seq_bounds[:-1])
        tok = lambda i: i
    k_off = (h * dk) // dk
    v_off = (2 * h * dk) // dv
    lr_col = 1 if backward else 0
    kern = functools.partial(_gla_kernel, backward=backward, n_chunks=tb // GLA_CHUNK,
                             reset_blocks=resets, dk=dk)
    return pl.pallas_call(
        kern,
        grid=(h, nblk),
        in_specs=[pl.BlockSpec((tb, dk), lambda hh, i: (tok(i), hh)),
                  pl.BlockSpec((tb, dk), lambda hh, i: (tok(i), k_off + hh)),
                  pl.BlockSpec((tb, dv), lambda hh, i: (tok(i), v_off + hh)),
                  pl.BlockSpec((tb, LANES), lambda hh, i: (tok(i), lr_col)),
                  pl.BlockSpec((LANES, dk), lambda hh, i: (0, hh)),
                  pl.BlockSpec((1, dk), lambda hh, i: (0, hh))],
        out_specs=pl.BlockSpec((tb, dv), lambda hh, i: (tok(i), hh)),
        out_shape=jax.ShapeDtypeStruct((m, h * dv), F32),
        scratch_shapes=[pltpu.VMEM((dv, dk), F32)],
        compiler_params=_cparams(("arbitrary", "arbitrary"), 40 * 1024 * 1024),
        name="gla_scan_bwd" if backward else "gla_scan_fwd",
    )(z, z, z, lr, up, bias)


def _attn_in_kernel(x_ref, w_ref, *refs):
    out_refs, acc_ref = refs[:-1], refs[-1]
    j = pl.program_id(1)
    w = ATTN_KV_W
    n_chunks, tm, _ = acc_ref.shape
    res = jnp.dot(x_ref[...], w_ref[...], preferred_element_type=F32)
    for c in range(n_chunks):
        acc_ref[c] = res[:, c * LANES:(c + 1) * LANES]
    for g, (o_ref, d) in enumerate(zip(out_refs, DIL_RATES)):
        @pl.when(jnp.logical_or(j == g, j >= N_DIL_GROUPS))
        def _(o_ref=o_ref, d=d):
            for r in range(d):
                for c in range(n_chunks):
                    rows = pl.ds(r, tm // d, stride=d) if d > 1 else slice(None)
                    o_ref[:, r * w + c * LANES:r * w + (c + 1) * LANES] = (
                        acc_ref[c, rows, :].astype(o_ref.dtype))


def attn_in(xn, w_att, tm=512):
    m, kd = xn.shape
    w = ATTN_KV_W
    nj = w_att.shape[1] // w
    sect = lambda i, j: (i, jnp.clip(j - (N_DIL_GROUPS - 1), 0, 2))
    return pl.pallas_call(
        _attn_in_kernel,
        grid=(m // tm, nj),
        in_specs=[pl.BlockSpec((tm, kd), lambda i, j: (i, 0)),
                  pl.BlockSpec((kd, w), lambda i, j: (0, j))],
        out_specs=[pl.BlockSpec((tm // d, d * w), sect) for d in DIL_RATES],
        out_shape=[jax.ShapeDtypeStruct((m // d, 3 * d * w), BF16) for d in DIL_RATES],
        scratch_shapes=[pltpu.VMEM((w // LANES, tm, LANES), F32)],
        compiler_params=_cparams(("parallel", "arbitrary"),
                                 4 * (tm * kd + kd * w) + 12 * tm * w + 8 * tm * w),
        name="attn_in",
    )(xn, w_att)


def _dil_kernel(q_ref, kp_ref, ko_ref, kn_ref, vp_ref, vo_ref, vn_ref, o_ref, l_ref, *,
                dilation, group, qb, start_tiles, end_tiles):
    t = pl.program_id(0)
    res = pl.program_id(1)
    has_prev = jnp.logical_not(_any_equal(t, start_tiles))
    has_next = jnp.logical_not(_any_equal(t + 1, end_tiles))
    r = DIL_WINDOWS[group] // (2 * dilation)
    e = ATTN_HEAD_DIM
    qi = lax.broadcasted_iota(jnp.int32, (r, 3 * r), 0)
    kj = lax.broadcasted_iota(jnp.int32, (r, 3 * r), 1)
    rel = jnp.abs(kj - r - qi)
    in_window = rel <= r
    rel_f = rel.astype(F32) * float(dilation)
    prev_ok = jnp.logical_or(kj >= r, has_prev)
    next_ok = jnp.logical_or(kj < 2 * r, has_next)

    def valid(j):
        v = in_window
        if j == 0:
            v = jnp.logical_and(v, prev_ok)
        if j == qb - 1:
            v = jnp.logical_and(v, next_ok)
        return v

    valids = {j: valid(j) for j in {0, qb - 1, min(1, qb - 1)}}
    n_heads = N_DIL_GROUPS * ATTN_KV_HEADS
    scale = e ** -0.5
    nh = ATTN_KV_HEADS

    def heads(ref, rows):
        return jnp.stack([ref[rows, h * e:(h + 1) * e] for h in range(nh)])

    slopes = [2.0 ** (-ALIBI_MAX_BIAS * (group * nh + h + 1) / n_heads) for h in range(nh)]
    bias = jnp.stack([slope * rel_f for slope in slopes])
    every = slice(None)
    k_all = jnp.concatenate([heads(kp_ref, every), heads(ko_ref, every), heads(kn_ref, every)],
                            axis=1)
    v_all = jnp.concatenate([heads(vp_ref, every), heads(vo_ref, every), heads(vn_ref, every)],
                            axis=1)
    qk_dims = (((2,), (2,)), ((0,), (0,)))
    pv_dims = (((2,), (1,)), ((0,), (0,)))
    for j in range(qb):
        q = heads(q_ref, slice(j * r, (j + 1) * r))
        s = lax.dot_general(q, k_all[:, j * r:(j + 3) * r], qk_dims, preferred_element_type=F32)
        s = jnp.where(valids.get(j, in_window)[None], s * scale - bias, NEG_INF)
        mx = jnp.max(s, axis=-1, keepdims=True)
        p = jnp.exp(s - mx)
        den = jnp.sum(p, axis=-1, keepdims=True)
        o = lax.dot_general(p.astype(BF16), v_all[:, j * r:(j + 3) * r], pv_dims,
                            preferred_element_type=F32) / den
        lse = jnp.broadcast_to(mx + jnp.log(den), (nh, r, e))
        if dilation == 1:
            dst = pl.ds(j * r, r)
        else:
            dst = pl.ds(j * r * dilation + res, r, stride=dilation)
        for h in range(nh):
            o_ref[h, dst, :] = o[h]
            l_ref[h, dst, :] = lse[h]


def dilated_attention_group(a, group, seq_bounds, qb):
    d = DIL_RATES[group]
    r = DIL_WINDOWS[group] // (2 * d)
    w = ATTN_KV_W
    rows = a.shape[0]
    m = rows * d
    tile = qb * r * d
    n_tiles = m // tile
    n_blocks = rows // r
    starts = tuple(s // tile for s in seq_bounds[:-1])
    ends = tuple(s // tile for s in seq_bounds[1:])

    own = lambda sec: pl.BlockSpec((qb * r, w), lambda t, rr: (t, sec * d + rr))
    prev = lambda sec: pl.BlockSpec((r, w), lambda t, rr: (jnp.maximum(t * qb - 1, 0), sec * d + rr))
    nxt = lambda sec: pl.BlockSpec(
        (r, w), lambda t, rr: (jnp.minimum((t + 1) * qb, n_blocks - 1), sec * d + rr))
    hm = (ATTN_KV_HEADS, m, ATTN_HEAD_DIM)
    out = pl.BlockSpec((ATTN_KV_HEADS, tile, ATTN_HEAD_DIM), lambda t, rr: (0, t, 0))
    kern = functools.partial(_dil_kernel, dilation=d, group=group, qb=qb,
                             start_tiles=starts, end_tiles=ends)
    return pl.pallas_call(
        kern,
        grid=(n_tiles, d),
        in_specs=[own(0), prev(1), own(1), nxt(1), prev(2), own(2), nxt(2)],
        out_specs=[out, out],
        out_shape=[jax.ShapeDtypeStruct(hm, F32), jax.ShapeDtypeStruct(hm, F32)],
        compiler_params=_cparams(("parallel", "arbitrary"), 16 * tile * w + 24 * qb * r * w),
        name=f"dilated_attn_g{group}",
    )(a, a, a, a, a, a, a)


def _mixer_post_kernel(of_ref, ob_ref, r_ref, g_ref, o0_ref, o1_ref, o2_ref,
                       l0_ref, l1_ref, l2_ref, ogla_ref, oatt_ref, *, dv):
    o = of_ref[...] + ob_ref[...]
    r = r_ref[...].astype(F32)
    gate = r * jax.nn.sigmoid(r) * g_ref[...]
    for h in range(GLA_HEADS):
        sl = slice(h * dv, (h + 1) * dv)
        oh = o[:, sl]
        ms = jnp.mean(oh * oh, axis=-1, keepdims=True)
        ogla_ref[:, sl] = (oh * lax.rsqrt(ms + NORM_EPS) * gate[:, sl]).astype(ogla_ref.dtype)
    e = ATTN_HEAD_DIM
    for h in range(ATTN_KV_HEADS):
        l0, l1, l2 = l0_ref[h], l1_ref[h], l2_ref[h]
        mx = jnp.maximum(jnp.maximum(l0, l1), l2)
        w0, w1, w2 = jnp.exp(l0 - mx), jnp.exp(l1 - mx), jnp.exp(l2 - mx)
        num = w0 * o0_ref[h] + w1 * o1_ref[h] + w2 * o2_ref[h]
        oatt_ref[:, h * e:(h + 1) * e] = (num / (w0 + w1 + w2)).astype(oatt_ref.dtype)


def mixer_post(o_f, o_b, z, r_col, gla_norm_g, outs, lses, dv, tr=256):
    m = z.shape[0]
    vw = GLA_HEADS * dv
    kvw = ATTN_KV_W
    row = lambda w: pl.BlockSpec((tr, w), lambda i: (i, 0))
    heads = pl.BlockSpec((ATTN_KV_HEADS, tr, ATTN_HEAD_DIM), lambda i: (0, i, 0))
    return pl.pallas_call(
        functools.partial(_mixer_post_kernel, dv=dv),
        grid=(m // tr,),
        in_specs=[row(vw), row(vw),
                  pl.BlockSpec((tr, vw), lambda i: (i, r_col)),
                  pl.BlockSpec((1, vw), lambda i: (0, 0))] + [heads] * 6,
        out_specs=[row(vw), row(kvw)],
        out_shape=[jax.ShapeDtypeStruct((m, vw), BF16),
                   jax.ShapeDtypeStruct((m, kvw), BF16)],
        compiler_params=_cparams(("parallel",), 36 * 1024 * 1024),
        name="mixer_post",
    )(o_f, o_b, z, gla_norm_g.reshape(1, vw), *outs, *lses)


def _merge_proj_kernel(og_ref, oa_ref, gg_ref, ga_ref, pg_ref, pa_ref, o_ref):
    yg = jnp.dot(og_ref[...], pg_ref[...], preferred_element_type=F32)
    ya = jnp.dot(oa_ref[...], pa_ref[...], preferred_element_type=F32)
    gg = jax.nn.sigmoid(gg_ref[...].astype(F32))
    ga = jax.nn.sigmoid(ga_ref[...].astype(F32))
    o_ref[...] = (gg * yg + ga * ya).astype(o_ref.dtype)


def merge_proj(o_gla, o_att, z, gate_start, proj_gla, proj_attn, tm=1024, tn=512):
    m, vw = o_gla.shape
    kvw = o_att.shape[1]
    d = proj_gla.shape[1]
    nj = d // tn
    gate_col = gate_start // tn
    return pl.pallas_call(
        _merge_proj_kernel,
        grid=(m // tm, nj),
        in_specs=[pl.BlockSpec((tm, vw), lambda i, j: (i, 0)),
                  pl.BlockSpec((tm, kvw), lambda i, j: (i, 0)),
                  pl.BlockSpec((tm, tn), lambda i, j: (i, gate_col + j)),
                  pl.BlockSpec((tm, tn), lambda i, j: (i, gate_col + nj + j)),
                  pl.BlockSpec((vw, tn), lambda i, j: (0, j)),
                  pl.BlockSpec((kvw, tn), lambda i, j: (0, j))],
        out_specs=pl.BlockSpec((tm, tn), lambda i, j: (i, j)),
        out_shape=jax.ShapeDtypeStruct((m, d), BF16),
        compiler_params=_cparams(("parallel", "arbitrary"), 40 * 1024 * 1024),
        name="merge_proj",
    )(o_gla, o_att, z, z, proj_gla, proj_attn)


def _cast_pad_kernel(w_ref, o_ref, *, rows, cols):
    tr = o_ref.shape[0]
    i = pl.program_id(0)

    @pl.when(i * tr < rows)
    def _():
        o_ref[:, :cols] = w_ref[...].astype(o_ref.dtype)
        if o_ref.shape[1] > cols:
            o_ref[:, cols:] = jnp.zeros((tr, o_ref.shape[1] - cols), o_ref.dtype)

    @pl.when(i * tr >= rows)
    def _():
        o_ref[...] = jnp.zeros_like(o_ref)


def cast_pad(w_stack, layer, rows_out, cols_out, tr):
    _, rows, cols = w_stack.shape
    last = rows // tr - 1
    return pl.pallas_call(
        functools.partial(_cast_pad_kernel, rows=rows, cols=cols),
        grid=(rows_out // tr,),
        in_specs=[pl.BlockSpec((None, tr, cols), lambda i: (layer, jnp.minimum(i, last), 0))],
        out_specs=pl.BlockSpec((tr, cols_out), lambda i: (i, 0)),
        out_shape=jax.ShapeDtypeStruct((rows_out, cols_out), BF16),
        compiler_params=_cparams(("parallel",), tr * (8 * cols + 4 * cols_out) + 4 * tr * cols),
        name="cast_pad",
    )(w_stack)


def _pad_to(x, axis, size):
    pad = [(0, 0)] * x.ndim
    pad[axis] = (0, size - x.shape[axis])
    return jnp.pad(x, pad)


def kernel(x_prompt, x_sample, ffn1_pre_g, ffn1_w_gate, ffn1_w_up, ffn1_w_down, ffn1_post_g,
           mix_pre_g, w_in, gla_decay_up_fwd, gla_decay_bias_fwd, gla_decay_up_bwd,
           gla_decay_bias_bwd, gla_norm_g, proj_gla, proj_attn, w_out, mix_post_g,
           ffn2_pre_g, ffn2_w_gate, ffn2_w_up, ffn2_w_down, ffn2_post_g):
    depth, d, d_ff = ffn1_w_gate.shape
    bp, tp, _ = x_prompt.shape
    bs, ts, _ = x_sample.shape
    mp, ms = bp * tp, bs * ts
    seq_bounds = tuple(b * tp for b in range(bp)) + tuple(mp + b * ts for b in range(bs + 1))

    dk = d // 16
    dv = d // 8
    qk_w = GLA_HEADS * dk
    v_w = GLA_HEADS * dv
    q_w = N_DIL_GROUPS * ATTN_KV_W
    lr0 = 2 * qk_w + 2 * v_w
    a0 = lr0 + 2 * GLA_RANK
    g0 = a0 + q_w + 2 * ATTN_KV_W
    ff_pad = -(-d_ff // 1024) * 1024

    x = (x_prompt.reshape(mp, d), x_sample.reshape(ms, d))
    xn = rmsnorm(x, ffn1_pre_g[0])

    def ffn(x, xn, l, wg, wu, wd, g_post, g_next, out_rows=None):
        wg = cast_pad(wg, l, d, ff_pad, 128)
        wu = cast_pad(wu, l, d, ff_pad, 128)
        wd = cast_pad(wd, l, ff_pad, d, 256)
        hmid = ffn_up(xn, wg, wu)
        y = matmul(hmid, wd, BF16, tk=ff_pad // 4, name="ffn_down")
        return norm_residual(y, x, g_post, g_next, 0.5, out_rows=out_rows)

    for l in range(depth):
        x, xn = ffn(x, xn, l, ffn1_w_gate, ffn1_w_up, ffn1_w_down, ffn1_post_g[l], mix_pre_g[l])

        w_main = jnp.concatenate([w_in[l][:, :lr0], w_in[l][:, g0:]], axis=1).astype(BF16)
        w_lr = jnp.concatenate([_pad_to(w_in[l][:, lr0:lr0 + GLA_RANK], 1, LANES),
                                _pad_to(w_in[l][:, lr0 + GLA_RANK:a0], 1, LANES)],
                               axis=1).astype(BF16)
        z = matmul(xn, w_main, BF16, name="mixer_in")
        lr = matmul(xn, w_lr, F32, name="mixer_in_lr")
        o_f = gla_scan(z, lr, _pad_to(gla_decay_up_fwd[l], 0, LANES),
                       gla_decay_bias_fwd[l].reshape(1, qk_w), seq_bounds, dk, dv, False)
        o_b = gla_scan(z, lr, _pad_to(gla_decay_up_bwd[l], 0, LANES),
                       gla_decay_bias_bwd[l].reshape(1, qk_w), seq_bounds, dk, dv, True)

        qkv = attn_in(xn, w_in[l][:, a0:g0].astype(BF16))
        outs, lses = [], []
        for g, qb in enumerate((8, 4, 1)):
            o_g, l_g = dilated_attention_group(qkv[g], g, seq_bounds, qb)
            outs.append(o_g)
            lses.append(l_g)

        o_gla, o_att = mixer_post(o_f, o_b, z, (2 * qk_w + v_w) // v_w, gla_norm_g[l], outs, lses, dv)
        merged = merge_proj(o_gla, o_att, z, lr0, cast_pad(proj_gla, l, v_w, d, 256),
                            cast_pad(proj_attn, l, ATTN_KV_W, d, 256))
        y = matmul(merged, cast_pad(w_out, l, d, d, 256), BF16, name="mixer_out")
        x, xn = norm_residual(y, x, mix_post_g[l], ffn2_pre_g[l], 1.0)

        last = l + 1 == depth
        x, xn = ffn(x, xn, l, ffn2_w_gate, ffn2_w_up, ffn2_w_down, ffn2_post_g[l],
                    None if last else ffn1_pre_g[l + 1], out_rows=(mp, ms) if last else None)

    return (x[0].reshape(bp, tp, d), x[1].reshape(bs, ts, d))
```

```python
import functools

import jax
import jax.numpy as jnp
from jax import lax
from jax.experimental import pallas as pl
from jax.experimental.pallas import tpu as pltpu

F32 = jnp.float32
BF16 = jnp.bfloat16

NORM_EPS = 1e-6
NEG_INF = -1e30

GLA_HEADS = 4
GLA_RANK = 16
GLA_TAU = 16.0
GLA_CHUNK = 64
ATTN_KV_HEADS = 8
ATTN_HEAD_DIM = 128
ATTN_KV_W = ATTN_KV_HEADS * ATTN_HEAD_DIM
DIL_WINDOWS = (128, 512, 2048)
DIL_RATES = (1, 4, 16)
N_DIL_GROUPS = 3
ALIBI_MAX_BIAS = 8.0

LANES = 128
VMEM_LIMIT_CAP = 60 * 1024 * 1024


def _cparams(dims, vmem_bytes):
    return pltpu.CompilerParams(dimension_semantics=dims,
                                vmem_limit_bytes=int(min(vmem_bytes * 5 // 4, VMEM_LIMIT_CAP)))


def _any_equal(idx, values):
    return functools.reduce(jnp.logical_or, [idx == v for v in values])


def _part_specs(n_first, tr, d):
    return [pl.BlockSpec((tr, d), lambda i: (jnp.minimum(i, n_first - 1), 0)),
            pl.BlockSpec((tr, d), lambda i: (jnp.maximum(i - n_first, 0), 0))]


def _rmsnorm_kernel(xa_ref, xb_ref, g_ref, o_ref, *, n_first):
    x = jnp.where(pl.program_id(0) < n_first, xa_ref[...], xb_ref[...])
    ms = jnp.mean(x * x, axis=-1, keepdims=True)
    o_ref[...] = (x * lax.rsqrt(ms + NORM_EPS) * g_ref[...]).astype(o_ref.dtype)


def rmsnorm(x_parts, g, tr=256):
    d = x_parts[0].shape[1]
    n_first = x_parts[0].shape[0] // tr
    m = x_parts[0].shape[0] + x_parts[1].shape[0]
    return pl.pallas_call(
        functools.partial(_rmsnorm_kernel, n_first=n_first),
        grid=(m // tr,),
        in_specs=_part_specs(n_first, tr, d) + [pl.BlockSpec((1, d), lambda i: (0, 0))],
        out_specs=pl.BlockSpec((tr, d), lambda i: (i, 0)),
        out_shape=jax.ShapeDtypeStruct((m, d), BF16),
        compiler_params=_cparams(("arbitrary",), 10 * tr * d * 4),
        name="rmsnorm",
    )(*x_parts, g.reshape(1, d))


def _norm_residual_kernel(y_ref, *refs, scale, with_next, n_in_first, n_out_first):
    i = pl.program_id(0)
    refs = list(refs)
    if n_in_first is None:
        x = refs.pop(0)[...]
    else:
        xa_ref, xb_ref = refs.pop(0), refs.pop(0)
        x = jnp.where(i < n_in_first, xa_ref[...], xb_ref[...])
    gp_ref = refs.pop(0)
    y = y_ref[...].astype(F32)
    ms = jnp.mean(y * y, axis=-1, keepdims=True)
    xo = x + scale * (y * lax.rsqrt(ms + NORM_EPS) * gp_ref[...])
    if with_next:
        gn_ref, xn_ref = refs.pop(0), refs.pop()
        ms2 = jnp.mean(xo * xo, axis=-1, keepdims=True)
        xn_ref[...] = (xo * lax.rsqrt(ms2 + NORM_EPS) * gn_ref[...]).astype(xn_ref.dtype)
    if n_out_first is None:
        refs[0][...] = xo
    else:
        @pl.when(i < n_out_first)
        def _():
            refs[0][...] = xo

        @pl.when(i >= n_out_first)
        def _():
            refs[1][...] = xo


def norm_residual(y, x, g_post, g_next, scale, out_rows=None, tr=256):
    m, d = y.shape
    with_next = g_next is not None
    row = pl.BlockSpec((tr, d), lambda i: (i, 0))
    vec = pl.BlockSpec((1, d), lambda i: (0, 0))
    gains = [g_post.reshape(1, d)] + ([g_next.reshape(1, d)] if with_next else [])
    x_parts = x if isinstance(x, (tuple, list)) else None
    n_in_first = None if x_parts is None else x_parts[0].shape[0] // tr
    n_out_first = None if out_rows is None else out_rows[0] // tr
    x_specs = [row] if x_parts is None else _part_specs(n_in_first, tr, d)
    if out_rows is None:
        xo_specs, xo_shapes = [row], [jax.ShapeDtypeStruct((m, d), F32)]
    else:
        xo_specs = _part_specs(n_out_first, tr, d)
        xo_shapes = [jax.ShapeDtypeStruct((rows, d), F32) for rows in out_rows]
    out = pl.pallas_call(
        functools.partial(_norm_residual_kernel, scale=scale, with_next=with_next,
                          n_in_first=n_in_first, n_out_first=n_out_first),
        grid=(m // tr,),
        in_specs=[row] + x_specs + [vec] * len(gains),
        out_specs=xo_specs + ([row] if with_next else []),
        out_shape=xo_shapes + ([jax.ShapeDtypeStruct((m, d), BF16)] if with_next else []),
        compiler_params=_cparams(("arbitrary",), 14 * tr * d * 4),
        name="norm_residual",
    )(y, *(x_parts if x_parts is not None else [x]), *gains)
    xo = out[0] if out_rows is None else tuple(out[:2])
    return xo, (out[-1] if with_next else None)


def _mm_kernel(a_ref, w_ref, o_ref):
    o_ref[...] = jnp.dot(a_ref[...], w_ref[...],
                         preferred_element_type=F32).astype(o_ref.dtype)


def _mm_acc_kernel(a_ref, w_ref, o_ref, acc_ref, *, k_tail):
    k = pl.program_id(2)
    last = pl.num_programs(2) - 1
    tk = a_ref.shape[1]

    @pl.when(k == 0)
    def _():
        acc_ref[...] = jnp.zeros_like(acc_ref)

    @pl.when(k < last)
    def _():
        acc_ref[...] += jnp.dot(a_ref[...], w_ref[...], preferred_element_type=F32)

    @pl.when(k == last)
    def _():
        if k_tail == tk:
            part = jnp.dot(a_ref[...], w_ref[...], preferred_element_type=F32)
        else:
            part = jnp.dot(a_ref[:, :k_tail], w_ref[:k_tail, :], preferred_element_type=F32)
        o_ref[...] = (acc_ref[...] + part).astype(o_ref.dtype)


def matmul(a, w, out_dtype, tm=1024, tn=1024, tk=None, name="matmul"):
    m, kd = a.shape
    n = w.shape[1]
    tn = min(tn, n)
    osz = jnp.dtype(out_dtype).itemsize
    if tk is None or tk == kd:
        return pl.pallas_call(
            _mm_kernel,
            grid=(m // tm, n // tn),
            in_specs=[pl.BlockSpec((tm, kd), lambda i, j: (i, 0)),
                      pl.BlockSpec((kd, tn), lambda i, j: (0, j))],
            out_specs=pl.BlockSpec((tm, tn), lambda i, j: (i, j)),
            out_shape=jax.ShapeDtypeStruct((m, n), out_dtype),
            compiler_params=_cparams(("parallel", "arbitrary"),
                                     4 * (tm * kd + kd * tn) + 2 * tm * tn * osz + 3 * tm * tn * 4),
            name=name,
        )(a, w)
    nk = pl.cdiv(kd, tk)
    return pl.pallas_call(
        functools.partial(_mm_acc_kernel, k_tail=kd - (nk - 1) * tk),
        grid=(m // tm, n // tn, nk),
        in_specs=[pl.BlockSpec((tm, tk), lambda i, j, k: (i, k)),
                  pl.BlockSpec((tk, tn), lambda i, j, k: (k, j))],
        out_specs=pl.BlockSpec((tm, tn), lambda i, j, k: (i, j)),
        out_shape=jax.ShapeDtypeStruct((m, n), out_dtype),
        scratch_shapes=[pltpu.VMEM((tm, tn), F32)],
        compiler_params=_cparams(("parallel", "arbitrary", "arbitrary"),
                                 4 * (tm * tk + tk * tn) + 2 * tm * tn * osz + 4 * tm * tn * 4),
        name=name,
    )(a, w)


def _ffn_up_kernel(x_ref, wg_ref, wu_ref, o_ref, *, n_tail):
    j = pl.program_id(1)
    last = pl.num_programs(1) - 1
    tn = o_ref.shape[1]

    def tile(width):
        x = x_ref[...]
        g = jnp.dot(x, wg_ref[:, :width], preferred_element_type=F32)
        u = jnp.dot(x, wu_ref[:, :width], preferred_element_type=F32)
        o_ref[:, :width] = (g * jax.nn.sigmoid(g) * u).astype(o_ref.dtype)

    if n_tail == tn:
        tile(tn)
    else:
        pl.when(j < last)(lambda: tile(tn))
        pl.when(j == last)(lambda: tile(n_tail))


def ffn_up(xn, wg, wu, tm=1024, tn=512):
    m, d = xn.shape
    f = wg.shape[1]
    nj = pl.cdiv(f, tn)
    return pl.pallas_call(
        functools.partial(_ffn_up_kernel, n_tail=f - (nj - 1) * tn),
        grid=(m // tm, nj),
        in_specs=[pl.BlockSpec((tm, d), lambda i, j: (i, 0)),
                  pl.BlockSpec((d, tn), lambda i, j: (0, j)),
                  pl.BlockSpec((d, tn), lambda i, j: (0, j))],
        out_specs=pl.BlockSpec((tm, tn), lambda i, j: (i, j)),
        out_shape=jax.ShapeDtypeStruct((m, f), BF16),
        compiler_params=_cparams(("parallel", "arbitrary"),
                                 4 * tm * d + 8 * d * tn + 4 * tm * tn + 16 * tm * tn),
        name="ffn_up",
    )(xn, wg, wu)


def _gla_kernel(q_ref, k_ref, v_ref, lr_ref, up_ref, bias_ref, o_ref, st_ref, *,
                backward, n_chunks, reset_blocks, dk):
    i = pl.program_id(1)
    blk = pl.num_programs(1) - 1 - i if backward else i

    @pl.when(_any_equal(blk, reset_blocks))
    def _():
        st_ref[...] = jnp.zeros_like(st_ref)

    c = GLA_CHUNK
    nc = n_chunks
    row = lax.broadcasted_iota(jnp.int32, (c, c), 0)
    col = lax.broadcasted_iota(jnp.int32, (c, c), 1)
    if backward:
        tri, mask = row <= col, row < col
    else:
        tri, mask = row >= col, row >= col
    tri = jnp.broadcast_to(tri.astype(BF16)[None], (nc, c, c))

    def bmm(a, b, ca, cb):
        return lax.dot_general(a, b, (((ca,), (cb,)), ((0,), (0,))), preferred_element_type=F32)

    x = jnp.dot(lr_ref[...].astype(BF16), up_ref[...].astype(BF16),
                preferred_element_type=F32) + bias_ref[...]
    la = (jnp.minimum(x, 0.0) - jnp.log1p(jnp.exp(-jnp.abs(x)))) * (1.0 / GLA_TAU)
    la = la.reshape(nc, c, dk)
    la_hi = la.astype(BF16)
    la_lo = (la - la_hi.astype(F32)).astype(BF16)
    b = bmm(tri, la_hi, 2, 1) + bmm(tri, la_lo, 2, 1)
    tot = jnp.sum(la, axis=1, keepdims=True)
    q = q_ref[...].astype(F32).reshape(nc, c, dk)
    k = k_ref[...].astype(F32).reshape(nc, c, dk)
    v = v_ref[...].reshape(nc, c, v_ref.shape[1])
    qd = (q * (dk ** -0.5) * jnp.exp(b)).astype(BF16)
    kd = (k * jnp.exp(-b)).astype(BF16)
    ks = (k * jnp.exp(tot - b)).astype(BF16)
    scores = jnp.where(mask[None], bmm(qd, kd, 2, 2), 0.0).astype(BF16)
    o_intra = bmm(scores, v, 2, 1)
    st_inc = bmm(v, ks, 1, 1)
    chunk_decay = jnp.exp(tot)

    st = st_ref[...]
    nt = (((1,), (1,)), ((), ()))
    for ci in (reversed(range(nc)) if backward else range(nc)):
        o_ref[ci * c:(ci + 1) * c, :] = o_intra[ci] + lax.dot_general(
            qd[ci], st.astype(BF16), nt, preferred_element_type=F32)
        st = st * chunk_decay[ci] + st_inc[ci]
    st_ref[...] = st


def gla_scan(z, lr, up, bias, seq_bounds, dk, dv, backward, tb=512):
    m = z.shape[0]
    h = GLA_HEADS
    nblk = m // tb
    if backward:
        resets = tuple(s // tb - 1 for s in seq_bounds[1:])
        tok = lambda i: nblk - 1 - i
    else:
        resets = tuple(s // tb for s in seq_bounds[:-1])
        tok = lambda i: i
    k_off = (h * dk) // dk
    v_off = (2 * h * dk) // dv
    lr_col = 1 if backward else 0
    kern = functools.partial(_gla_kernel, backward=backward, n_chunks=tb // GLA_CHUNK,
                             reset_blocks=resets, dk=dk)
    return pl.pallas_call(
        kern,
        grid=(h, nblk),
        in_specs=[pl.BlockSpec((tb, dk), lambda hh, i: (tok(i), hh)),
                  pl.BlockSpec((tb, dk), lambda hh, i: (tok(i), k_off + hh)),
                  pl.BlockSpec((tb, dv), lambda hh, i: (tok(i), v_off + hh)),
                  pl.BlockSpec((tb, LANES), lambda hh, i: (tok(i), lr_col)),
                  pl.BlockSpec((LANES, dk), lambda hh, i: (0, hh)),
                  pl.BlockSpec((1, dk), lambda hh, i: (0, hh))],
        out_specs=pl.BlockSpec((tb, dv), lambda hh, i: (tok(i), hh)),
        out_shape=jax.ShapeDtypeStruct((m, h * dv), F32),
        scratch_shapes=[pltpu.VMEM((dv, dk), F32)],
        compiler_params=_cparams(("arbitrary", "arbitrary"), 40 * 1024 * 1024),
        name="gla_scan_bwd" if backward else "gla_scan_fwd",
    )(z, z, z, lr, up, bias)


def _attn_in_kernel(x_ref, w_ref, *refs):
    out_refs, acc_ref = refs[:-1], refs[-1]
    j = pl.program_id(1)
    n_chunks, tm, _ = acc_ref.shape
    res = jnp.dot(x_ref[...], w_ref[...], preferred_element_type=F32)
    for c in range(n_chunks):
        acc_ref[c] = res[:, c * LANES:(c + 1) * LANES]
    for g, (o_ref, d) in enumerate(zip(out_refs, DIL_RATES)):
        @pl.when(jnp.logical_or(j == g, j >= N_DIL_GROUPS))
        def _(o_ref=o_ref, d=d):
            for r in range(d):
                for c in range(n_chunks):
                    rows = pl.ds(r, tm // d, stride=d) if d > 1 else slice(None)
                    o_ref[r, :, c * LANES:(c + 1) * LANES] = (
                        acc_ref[c, rows, :].astype(o_ref.dtype))


def attn_in(xn, w_att, tm=512):
    m, kd = xn.shape
    w = ATTN_KV_W
    nj = w_att.shape[1] // w
    sect = lambda i, j: (jnp.clip(j - (N_DIL_GROUPS - 1), 0, 2), 0, i, 0)
    return pl.pallas_call(
        _attn_in_kernel,
        grid=(m // tm, nj),
        in_specs=[pl.BlockSpec((tm, kd), lambda i, j: (i, 0)),
                  pl.BlockSpec((kd, w), lambda i, j: (0, j))],
        out_specs=[pl.BlockSpec((None, d, tm // d, w), sect) for d in DIL_RATES],
        out_shape=[jax.ShapeDtypeStruct((3, d, m // d, w), BF16) for d in DIL_RATES],
        scratch_shapes=[pltpu.VMEM((w // LANES, tm, LANES), F32)],
        compiler_params=_cparams(("parallel", "arbitrary"),
                                 4 * (tm * kd + kd * w) + 12 * tm * w + 8 * tm * w),
        name="attn_in",
    )(xn, w_att)


def _dil_kernel(q_ref, kp_ref, ko_ref, kn_ref, vp_ref, vo_ref, vn_ref, o_ref, l_ref, *,
                dilation, group, qb, start_tiles, end_tiles):
    t = pl.program_id(0)
    n_res = q_ref.shape[0]
    res0 = pl.program_id(1) * n_res
    has_prev = jnp.logical_not(_any_equal(t, start_tiles))
    has_next = jnp.logical_not(_any_equal(t + 1, end_tiles))
    r = DIL_WINDOWS[group] // (2 * dilation)
    e = ATTN_HEAD_DIM
    qi = lax.broadcasted_iota(jnp.int32, (r, 3 * r), 0)
    kj = lax.broadcasted_iota(jnp.int32, (r, 3 * r), 1)
    rel = jnp.abs(kj - r - qi)
    in_window = rel <= r
    rel_f = rel.astype(F32) * float(dilation)
    prev_ok = jnp.logical_or(kj >= r, has_prev)
    next_ok = jnp.logical_or(kj < 2 * r, has_next)

    def valid(j):
        v = in_window
        if j == 0:
            v = jnp.logical_and(v, prev_ok)
        if j == qb - 1:
            v = jnp.logical_and(v, next_ok)
        return v

    valids = {j: valid(j) for j in {0, qb - 1, min(1, qb - 1)}}
    n_heads = N_DIL_GROUPS * ATTN_KV_HEADS
    scale = e ** -0.5
    nh = ATTN_KV_HEADS

    def heads(ref, ri, rows):
        return jnp.stack([ref[ri, rows, h * e:(h + 1) * e] for h in range(nh)])

    slopes = [2.0 ** (-ALIBI_MAX_BIAS * (group * nh + h + 1) / n_heads) for h in range(nh)]
    bias = jnp.stack([slope * rel_f for slope in slopes])
    every = slice(None)
    qk_dims = (((2,), (2,)), ((0,), (0,)))
    pv_dims = (((2,), (1,)), ((0,), (0,)))
    for ri in range(n_res):
        k_all = jnp.concatenate([heads(kp_ref, ri, every), heads(ko_ref, ri, every),
                                 heads(kn_ref, ri, every)], axis=1)
        v_all = jnp.concatenate([heads(vp_ref, ri, every), heads(vo_ref, ri, every),
                                 heads(vn_ref, ri, every)], axis=1)
        for j in range(qb):
            q = heads(q_ref, ri, slice(j * r, (j + 1) * r))
            s = lax.dot_general(q, k_all[:, j * r:(j + 3) * r], qk_dims,
                                preferred_element_type=F32)
            s = jnp.where(valids.get(j, in_window)[None], s * scale - bias, NEG_INF)
            mx = jnp.max(s, axis=-1, keepdims=True)
            p = jnp.exp(s - mx)
            den = jnp.sum(p, axis=-1, keepdims=True)
            o = lax.dot_general(p.astype(BF16), v_all[:, j * r:(j + 3) * r], pv_dims,
                                preferred_element_type=F32) / den
            lse = jnp.broadcast_to(mx + jnp.log(den), (nh, r, e))
            if dilation == 1:
                dst = pl.ds(j * r, r)
            else:
                dst = pl.ds(j * r * dilation + res0 + ri, r, stride=dilation)
            for h in range(nh):
                o_ref[h, dst, :] = o[h]
                l_ref[h, dst, :] = lse[h]


def dilated_attention_group(a, group, seq_bounds, qb, n_res):
    d = DIL_RATES[group]
    r = DIL_WINDOWS[group] // (2 * d)
    w = ATTN_KV_W
    rows = a.shape[2]
    m = rows * d
    tile = qb * r * d
    n_tiles = m // tile
    n_blocks = rows // r
    starts = tuple(s // tile for s in seq_bounds[:-1])
    ends = tuple(s // tile for s in seq_bounds[1:])

    own = lambda sec: pl.BlockSpec((None, n_res, qb * r, w), lambda t, rr: (sec, rr, t, 0))
    prev = lambda sec: pl.BlockSpec(
        (None, n_res, r, w), lambda t, rr: (sec, rr, jnp.maximum(t * qb - 1, 0), 0))
    nxt = lambda sec: pl.BlockSpec(
        (None, n_res, r, w), lambda t, rr: (sec, rr, jnp.minimum((t + 1) * qb, n_blocks - 1), 0))
    hm = (ATTN_KV_HEADS, m, ATTN_HEAD_DIM)
    out = pl.BlockSpec((ATTN_KV_HEADS, tile, ATTN_HEAD_DIM), lambda t, rr: (0, t, 0))
    kern = functools.partial(_dil_kernel, dilation=d, group=group, qb=qb,
                             start_tiles=starts, end_tiles=ends)
    return pl.pallas_call(
        kern,
        grid=(n_tiles, d // n_res),
        in_specs=[own(0), prev(1), own(1), nxt(1), prev(2), own(2), nxt(2)],
        out_specs=[out, out],
        out_shape=[jax.ShapeDtypeStruct(hm, F32), jax.ShapeDtypeStruct(hm, F32)],
        compiler_params=_cparams(("parallel", "arbitrary"),
                                 16 * tile * w + n_res * (12 * qb + 16) * r * w + 16 * r * w * 4),
        name=f"dilated_attn_g{group}",
    )(a, a, a, a, a, a, a)


def _mixer_post_kernel(of_ref, ob_ref, r_ref, g_ref, o0_ref, o1_ref, o2_ref,
                       l0_ref, l1_ref, l2_ref, ogla_ref, oatt_ref, *, dv):
    o = of_ref[...] + ob_ref[...]
    r = r_ref[...].astype(F32)
    gate = r * jax.nn.sigmoid(r) * g_ref[...]
    for h in range(GLA_HEADS):
        sl = slice(h * dv, (h + 1) * dv)
        oh = o[:, sl]
        ms = jnp.mean(oh * oh, axis=-1, keepdims=True)
        ogla_ref[:, sl] = (oh * lax.rsqrt(ms + NORM_EPS) * gate[:, sl]).astype(ogla_ref.dtype)
    e = ATTN_HEAD_DIM
    for h in range(ATTN_KV_HEADS):
        l0, l1, l2 = l0_ref[h], l1_ref[h], l2_ref[h]
        mx = jnp.maximum(jnp.maximum(l0, l1), l2)
        w0, w1, w2 = jnp.exp(l0 - mx), jnp.exp(l1 - mx), jnp.exp(l2 - mx)
        num = w0 * o0_ref[h] + w1 * o1_ref[h] + w2 * o2_ref[h]
        oatt_ref[:, h * e:(h + 1) * e] = (num / (w0 + w1 + w2)).astype(oatt_ref.dtype)


def mixer_post(o_f, o_b, z, r_col, gla_norm_g, outs, lses, dv, tr=256):
    m = z.shape[0]
    vw = GLA_HEADS * dv
    kvw = ATTN_KV_W
    row = lambda w: pl.BlockSpec((tr, w), lambda i: (i, 0))
    heads = pl.BlockSpec((ATTN_KV_HEADS, tr, ATTN_HEAD_DIM), lambda i: (0, i, 0))
    return pl.pallas_call(
        functools.partial(_mixer_post_kernel, dv=dv),
        grid=(m // tr,),
        in_specs=[row(vw), row(vw),
                  pl.BlockSpec((tr, vw), lambda i: (i, r_col)),
                  pl.BlockSpec((1, vw), lambda i: (0, 0))] + [heads] * 6,
        out_specs=[row(vw), row(kvw)],
        out_shape=[jax.ShapeDtypeStruct((m, vw), BF16),
                   jax.ShapeDtypeStruct((m, kvw), BF16)],
        compiler_params=_cparams(("parallel",), 36 * 1024 * 1024),
        name="mixer_post",
    )(o_f, o_b, z, gla_norm_g.reshape(1, vw), *outs, *lses)


def _merge_proj_kernel(og_ref, oa_ref, gg_ref, ga_ref, pg_ref, pa_ref, o_ref):
    yg = jnp.dot(og_ref[...], pg_ref[...], preferred_element_type=F32)
    ya = jnp.dot(oa_ref[...], pa_ref[...], preferred_element_type=F32)
    gg = jax.nn.sigmoid(gg_ref[...].astype(F32))
    ga = jax.nn.sigmoid(ga_ref[...].astype(F32))
    o_ref[...] = (gg * yg + ga * ya).astype(o_ref.dtype)


def merge_proj(o_gla, o_att, z, gate_start, proj_gla, proj_attn, tm=1024, tn=512):
    m, vw = o_gla.shape
    kvw = o_att.shape[1]
    d = proj_gla.shape[1]
    nj = d // tn
    gate_col = gate_start // tn
    return pl.pallas_call(
        _merge_proj_kernel,
        grid=(m // tm, nj),
        in_specs=[pl.BlockSpec((tm, vw), lambda i, j: (i, 0)),
                  pl.BlockSpec((tm, kvw), lambda i, j: (i, 0)),
                  pl.BlockSpec((tm, tn), lambda i, j: (i, gate_col + j)),
                  pl.BlockSpec((tm, tn), lambda i, j: (i, gate_col + nj + j)),
                  pl.BlockSpec((vw, tn), lambda i, j: (0, j)),
                  pl.BlockSpec((kvw, tn), lambda i, j: (0, j))],
        out_specs=pl.BlockSpec((tm, tn), lambda i, j: (i, j)),
        out_shape=jax.ShapeDtypeStruct((m, d), BF16),
        compiler_params=_cparams(("parallel", "arbitrary"), 40 * 1024 * 1024),
        name="merge_proj",
    )(o_gla, o_att, z, z, proj_gla, proj_attn)


def _cast_kernel(w_ref, o_ref):
    o_ref[...] = w_ref[...].astype(o_ref.dtype)


def cast_layer(w_stack, layer, tr=128):
    _, rows, cols = w_stack.shape
    return pl.pallas_call(
        _cast_kernel,
        grid=(rows // tr,),
        in_specs=[pl.BlockSpec((None, tr, cols), lambda i: (layer, i, 0))],
        out_specs=pl.BlockSpec((tr, cols), lambda i: (i, 0)),
        out_shape=jax.ShapeDtypeStruct((rows, cols), BF16),
        compiler_params=_cparams(("parallel",), 16 * tr * cols),
        name="cast_layer",
    )(w_stack)


def _pad_to(x, axis, size):
    pad = [(0, 0)] * x.ndim
    pad[axis] = (0, size - x.shape[axis])
    return jnp.pad(x, pad)


def kernel(x_prompt, x_sample, ffn1_pre_g, ffn1_w_gate, ffn1_w_up, ffn1_w_down, ffn1_post_g,
           mix_pre_g, w_in, gla_decay_up_fwd, gla_decay_bias_fwd, gla_decay_up_bwd,
           gla_decay_bias_bwd, gla_norm_g, proj_gla, proj_attn, w_out, mix_post_g,
           ffn2_pre_g, ffn2_w_gate, ffn2_w_up, ffn2_w_down, ffn2_post_g):
    depth, d, d_ff = ffn1_w_gate.shape
    bp, tp, _ = x_prompt.shape
    bs, ts, _ = x_sample.shape
    mp, ms = bp * tp, bs * ts
    seq_bounds = tuple(b * tp for b in range(bp)) + tuple(mp + b * ts for b in range(bs + 1))

    dk = d // 16
    dv = d // 8
    qk_w = GLA_HEADS * dk
    v_w = GLA_HEADS * dv
    q_w = N_DIL_GROUPS * ATTN_KV_W
    lr0 = 2 * qk_w + 2 * v_w
    a0 = lr0 + 2 * GLA_RANK
    g0 = a0 + q_w + 2 * ATTN_KV_W
    ff_tk = -(-d_ff // (4 * 256)) * 256

    x = (x_prompt.reshape(mp, d), x_sample.reshape(ms, d))
    xn = rmsnorm(x, ffn1_pre_g[0])

    def ffn(x, xn, l, wg, wu, wd, g_post, g_next, out_rows=None):
        hmid = ffn_up(xn, cast_layer(wg, l), cast_layer(wu, l))
        y = matmul(hmid, cast_layer(wd, l, tr=256), BF16, tk=ff_tk, name="ffn_down")
        return norm_residual(y, x, g_post, g_next, 0.5, out_rows=out_rows)

    for l in range(depth):
        x, xn = ffn(x, xn, l, ffn1_w_gate, ffn1_w_up, ffn1_w_down, ffn1_post_g[l], mix_pre_g[l])

        w_main = jnp.concatenate([w_in[l][:, :lr0], w_in[l][:, g0:]], axis=1).astype(BF16)
        w_lr = jnp.concatenate([_pad_to(w_in[l][:, lr0:lr0 + GLA_RANK], 1, LANES),
                                _pad_to(w_in[l][:, lr0 + GLA_RANK:a0], 1, LANES)],
                               axis=1).astype(BF16)
        z = matmul(xn, w_main, BF16, name="mixer_in")
        lr = matmul(xn, w_lr, F32, name="mixer_in_lr")
        o_f = gla_scan(z, lr, _pad_to(gla_decay_up_fwd[l], 0, LANES),
                       gla_decay_bias_fwd[l].reshape(1, qk_w), seq_bounds, dk, dv, False)
        o_b = gla_scan(z, lr, _pad_to(gla_decay_up_bwd[l], 0, LANES),
                       gla_decay_bias_bwd[l].reshape(1, qk_w), seq_bounds, dk, dv, True)

        qkv = attn_in(xn, w_in[l][:, a0:g0].astype(BF16))
        outs, lses = [], []
        for g, (qb, n_res) in enumerate(((8, 1), (4, 4), (1, 4))):
            o_g, l_g = dilated_attention_group(qkv[g], g, seq_bounds, qb, n_res)
            outs.append(o_g)
            lses.append(l_g)

        o_gla, o_att = mixer_post(o_f, o_b, z, (2 * qk_w + v_w) // v_w, gla_norm_g[l], outs, lses, dv)
        merged = merge_proj(o_gla, o_att, z, lr0, cast_layer(proj_gla, l, tr=256),
                            cast_layer(proj_attn, l, tr=256))
        y = matmul(merged, cast_layer(w_out, l, tr=256), BF16, name="mixer_out")
        x, xn = norm_residual(y, x, mix_post_g[l], ffn2_pre_g[l], 1.0)

        last = l + 1 == depth
        x, xn = ffn(x, xn, l, ffn2_w_gate, ffn2_w_up, ffn2_w_down, ffn2_post_g[l],
                    None if last else ffn1_pre_g[l + 1], out_rows=(mp, ms) if last else None)

    return (x[0].reshape(bp, tp, d), x[1].reshape(bs, ts, d))
```

```python
import functools

import jax
import jax.numpy as jnp
from jax import lax
from jax.experimental import pallas as pl
from jax.experimental.pallas import tpu as pltpu

F32 = jnp.float32
BF16 = jnp.bfloat16

NORM_EPS = 1e-6
NEG_INF = -1e30

GLA_HEADS = 4
GLA_RANK = 16
GLA_TAU = 16.0
GLA_CHUNK = 64
ATTN_KV_HEADS = 8
ATTN_HEAD_DIM = 128
ATTN_KV_W = ATTN_KV_HEADS * ATTN_HEAD_DIM
DIL_WINDOWS = (128, 512, 2048)
DIL_RATES = (1, 4, 16)
N_DIL_GROUPS = 3
ALIBI_MAX_BIAS = 8.0

LANES = 128
VMEM_LIMIT_CAP = 60 * 1024 * 1024


def _cparams(dims, vmem_bytes):
    return pltpu.CompilerParams(dimension_semantics=dims,
                                vmem_limit_bytes=int(min(vmem_bytes * 5 // 4, VMEM_LIMIT_CAP)))


def _any_equal(idx, values):
    return functools.reduce(jnp.logical_or, [idx == v for v in values])


def _part_specs(n_first, tr, d):
    return [pl.BlockSpec((tr, d), lambda i: (jnp.minimum(i, n_first - 1), 0)),
            pl.BlockSpec((tr, d), lambda i: (jnp.maximum(i - n_first, 0), 0))]


def _rmsnorm_kernel(xa_ref, xb_ref, g_ref, o_ref, *, n_first):
    x = jnp.where(pl.program_id(0) < n_first, xa_ref[...], xb_ref[...])
    ms = jnp.mean(x * x, axis=-1, keepdims=True)
    o_ref[...] = (x * lax.rsqrt(ms + NORM_EPS) * g_ref[...]).astype(o_ref.dtype)


def rmsnorm(x_parts, g, tr=256):
    d = x_parts[0].shape[1]
    n_first = x_parts[0].shape[0] // tr
    m = x_parts[0].shape[0] + x_parts[1].shape[0]
    return pl.pallas_call(
        functools.partial(_rmsnorm_kernel, n_first=n_first),
        grid=(m // tr,),
        in_specs=_part_specs(n_first, tr, d) + [pl.BlockSpec((1, d), lambda i: (0, 0))],
        out_specs=pl.BlockSpec((tr, d), lambda i: (i, 0)),
        out_shape=jax.ShapeDtypeStruct((m, d), BF16),
        compiler_params=_cparams(("arbitrary",), 10 * tr * d * 4),
        name="rmsnorm",
    )(*x_parts, g.reshape(1, d))


def _norm_residual_kernel(y_ref, *refs, scale, with_next, n_in_first, n_out_first):
    i = pl.program_id(0)
    refs = list(refs)
    if n_in_first is None:
        x = refs.pop(0)[...]
    else:
        xa_ref, xb_ref = refs.pop(0), refs.pop(0)
        x = jnp.where(i < n_in_first, xa_ref[...], xb_ref[...])
    gp_ref = refs.pop(0)
    y = y_ref[...].astype(F32)
    ms = jnp.mean(y * y, axis=-1, keepdims=True)
    xo = x + scale * (y * lax.rsqrt(ms + NORM_EPS) * gp_ref[...])
    if with_next:
        gn_ref, xn_ref = refs.pop(0), refs.pop()
        ms2 = jnp.mean(xo * xo, axis=-1, keepdims=True)
        xn_ref[...] = (xo * lax.rsqrt(ms2 + NORM_EPS) * gn_ref[...]).astype(xn_ref.dtype)
    if n_out_first is None:
        refs[0][...] = xo
    else:
        @pl.when(i < n_out_first)
        def _():
            refs[0][...] = xo

        @pl.when(i >= n_out_first)
        def _():
            refs[1][...] = xo


def norm_residual(y, x, g_post, g_next, scale, out_rows=None, tr=256):
    m, d = y.shape
    with_next = g_next is not None
    row = pl.BlockSpec((tr, d), lambda i: (i, 0))
    vec = pl.BlockSpec((1, d), lambda i: (0, 0))
    gains = [g_post.reshape(1, d)] + ([g_next.reshape(1, d)] if with_next else [])
    x_parts = x if isinstance(x, (tuple, list)) else None
    n_in_first = None if x_parts is None else x_parts[0].shape[0] // tr
    n_out_first = None if out_rows is None else out_rows[0] // tr
    x_specs = [row] if x_parts is None else _part_specs(n_in_first, tr, d)
    if out_rows is None:
        xo_specs, xo_shapes = [row], [jax.ShapeDtypeStruct((m, d), F32)]
    else:
        xo_specs = _part_specs(n_out_first, tr, d)
        xo_shapes = [jax.ShapeDtypeStruct((rows, d), F32) for rows in out_rows]
    out = pl.pallas_call(
        functools.partial(_norm_residual_kernel, scale=scale, with_next=with_next,
                          n_in_first=n_in_first, n_out_first=n_out_first),
        grid=(m // tr,),
        in_specs=[row] + x_specs + [vec] * len(gains),
        out_specs=xo_specs + ([row] if with_next else []),
        out_shape=xo_shapes + ([jax.ShapeDtypeStruct((m, d), BF16)] if with_next else []),
        compiler_params=_cparams(("arbitrary",), 14 * tr * d * 4),
        name="norm_residual",
    )(y, *(x_parts if x_parts is not None else [x]), *gains)
    xo = out[0] if out_rows is None else tuple(out[:2])
    return xo, (out[-1] if with_next else None)


def _mm_kernel(a_ref, w_ref, o_ref):
    o_ref[...] = jnp.dot(a_ref[...], w_ref[...],
                         preferred_element_type=F32).astype(o_ref.dtype)


def matmul(a, w, out_dtype, tm=1024, tn=1024, name="matmul"):
    m, kd = a.shape
    n = w.shape[1]
    tn = min(tn, n)
    osz = jnp.dtype(out_dtype).itemsize
    return pl.pallas_call(
        _mm_kernel,
        grid=(m // tm, n // tn),
        in_specs=[pl.BlockSpec((tm, kd), lambda i, j: (i, 0)),
                  pl.BlockSpec((kd, tn), lambda i, j: (0, j))],
        out_specs=pl.BlockSpec((tm, tn), lambda i, j: (i, j)),
        out_shape=jax.ShapeDtypeStruct((m, n), out_dtype),
        compiler_params=_cparams(("parallel", "arbitrary"),
                                 4 * (tm * kd + kd * tn) + 2 * tm * tn * osz + 3 * tm * tn * 4),
        name=name,
    )(a, w)


def _ffn_split(f, tn):
    n_full = f // tn
    tail = f - n_full * tn
    assert tail == 0 or (tail % LANES == 0 and (n_full * tn) % tail == 0)
    return n_full, tail


def _ffn_up_kernel(x_ref, wg_ref, wu_ref, *rest, n_full):
    j = pl.program_id(1)

    def tile(wg, wu, o_ref):
        x = x_ref[...]
        g = jnp.dot(x, wg[...], preferred_element_type=F32)
        u = jnp.dot(x, wu[...], preferred_element_type=F32)
        o_ref[...] = (g * jax.nn.sigmoid(g) * u).astype(o_ref.dtype)

    if len(rest) == 1:
        tile(wg_ref, wu_ref, rest[0])
    else:
        wg_tail_ref, wu_tail_ref, o_ref, o_tail_ref = rest
        pl.when(j < n_full)(lambda: tile(wg_ref, wu_ref, o_ref))
        pl.when(j == n_full)(lambda: tile(wg_tail_ref, wu_tail_ref, o_tail_ref))


def ffn_up(xn, wg, wu, tm=1024, tn=512):
    m, d = xn.shape
    n_full, tail = _ffn_split(wg.shape[1], tn)
    col = lambda i, j: (0, jnp.minimum(j, n_full - 1))
    in_specs = [pl.BlockSpec((tm, d), lambda i, j: (i, 0)),
                pl.BlockSpec((d, tn), col), pl.BlockSpec((d, tn), col)]
    out_specs = [pl.BlockSpec((tm, tn), lambda i, j: (i, jnp.minimum(j, n_full - 1)))]
    out_shape = [jax.ShapeDtypeStruct((m, n_full * tn), BF16)]
    args = [xn, wg, wu]
    if tail:
        tail_col = (n_full * tn) // tail
        in_specs += [pl.BlockSpec((d, tail), lambda i, j: (0, tail_col))] * 2
        out_specs.append(pl.BlockSpec((tm, tail), lambda i, j: (i, 0)))
        out_shape.append(jax.ShapeDtypeStruct((m, tail), BF16))
        args += [wg, wu]
    out = pl.pallas_call(
        functools.partial(_ffn_up_kernel, n_full=n_full),
        grid=(m // tm, n_full + (1 if tail else 0)),
        in_specs=in_specs,
        out_specs=out_specs,
        out_shape=out_shape,
        compiler_params=_cparams(("parallel", "arbitrary"),
                                 4 * tm * d + 8 * d * (tn + tail) + 4 * tm * (tn + tail)
                                 + 16 * tm * tn),
        name="ffn_up",
    )(*args)
    return out[0], (out[1] if tail else None)


def _ffn_down_kernel(a_ref, w_ref, *rest):
    k = pl.program_id(2)
    acc_ref = rest[-1]

    @pl.when(k == 0)
    def _():
        acc_ref[...] = jnp.zeros_like(acc_ref)

    @pl.when(k < pl.num_programs(2) - 1)
    def _():
        acc_ref[...] += jnp.dot(a_ref[...], w_ref[...], preferred_element_type=F32)

    @pl.when(k == pl.num_programs(2) - 1)
    def _():
        y = acc_ref[...] + jnp.dot(a_ref[...], w_ref[...], preferred_element_type=F32)
        if len(rest) == 4:
            a_tail_ref, w_tail_ref = rest[:2]
            y += jnp.dot(a_tail_ref[...], w_tail_ref[...], preferred_element_type=F32)
        o_ref = rest[-2]
        o_ref[...] = y.astype(o_ref.dtype)


def ffn_down(h_main, h_tail, wd, tm=1024, tn=1024, nk=4):
    m, k_main = h_main.shape
    n = wd.shape[1]
    tk = k_main // nk
    assert tk * nk == k_main and tk % LANES == 0
    in_specs = [pl.BlockSpec((tm, tk), lambda i, j, k: (i, k)),
                pl.BlockSpec((tk, tn), lambda i, j, k: (k, j))]
    args = [h_main, wd]
    tail = 0 if h_tail is None else h_tail.shape[1]
    if tail:
        tail_row = k_main // tail
        in_specs += [pl.BlockSpec((tm, tail), lambda i, j, k: (i, 0)),
                     pl.BlockSpec((tail, tn), lambda i, j, k: (tail_row, j))]
        args += [h_tail, wd]
    return pl.pallas_call(
        _ffn_down_kernel,
        grid=(m // tm, n // tn, nk),
        in_specs=in_specs,
        out_specs=pl.BlockSpec((tm, tn), lambda i, j, k: (i, j)),
        out_shape=jax.ShapeDtypeStruct((m, n), BF16),
        scratch_shapes=[pltpu.VMEM((tm, tn), F32)],
        compiler_params=_cparams(("parallel", "arbitrary", "arbitrary"),
                                 4 * (tm + tn) * (tk + tail) + 4 * tm * tn + 16 * tm * tn),
        name="ffn_down",
    )(*args)


def _gla_kernel(q_ref, k_ref, v_ref, lr_ref, up_ref, bias_ref, o_ref, st_ref, *,
                backward, n_chunks, reset_blocks, dk):
    i = pl.program_id(1)
    blk = pl.num_programs(1) - 1 - i if backward else i

    @pl.when(_any_equal(blk, reset_blocks))
    def _():
        st_ref[...] = jnp.zeros_like(st_ref)

    c = GLA_CHUNK
    nc = n_chunks
    row = lax.broadcasted_iota(jnp.int32, (c, c), 0)
    col = lax.broadcasted_iota(jnp.int32, (c, c), 1)
    if backward:
        tri, mask = row <= col, row < col
    else:
        tri, mask = row >= col, row >= col
    tri = jnp.broadcast_to(tri.astype(BF16)[None], (nc, c, c))

    def bmm(a, b, ca, cb):
        return lax.dot_general(a, b, (((ca,), (cb,)), ((0,), (0,))), preferred_element_type=F32)

    x = jnp.dot(lr_ref[...].astype(BF16), up_ref[...].astype(BF16),
                preferred_element_type=F32) + bias_ref[...]
    la = (jnp.minimum(x, 0.0) - jnp.log1p(jnp.exp(-jnp.abs(x)))) * (1.0 / GLA_TAU)
    la = la.reshape(nc, c, dk)
    la_hi = la.astype(BF16)
    la_lo = (la - la_hi.astype(F32)).astype(BF16)
    b = bmm(tri, la_hi, 2, 1) + bmm(tri, la_lo, 2, 1)
    tot = jnp.sum(la, axis=1, keepdims=True)
    q = q_ref[...].astype(F32).reshape(nc, c, dk)
    k = k_ref[...].astype(F32).reshape(nc, c, dk)
    v = v_ref[...].reshape(nc, c, v_ref.shape[1])
    qd = (q * (dk ** -0.5) * jnp.exp(b)).astype(BF16)
    kd = (k * jnp.exp(-b)).astype(BF16)
    ks = (k * jnp.exp(tot - b)).astype(BF16)
    scores = jnp.where(mask[None], bmm(qd, kd, 2, 2), 0.0).astype(BF16)
    o_intra = bmm(scores, v, 2, 1)
    st_inc = bmm(v, ks, 1, 1)
    chunk_decay = jnp.exp(tot)

    st = st_ref[...]
    nt = (((1,), (1,)), ((), ()))
    for ci in (reversed(range(nc)) if backward else range(nc)):
        o_ref[ci * c:(ci + 1) * c, :] = o_intra[ci] + lax.dot_general(
            qd[ci], st.astype(BF16), nt, preferred_element_type=F32)
        st = st * chunk_decay[ci] + st_inc[ci]
    st_ref[...] = st


def gla_scan(z, lr, up, bias, seq_bounds, dk, dv, backward, tb=512):
    m = z.shape[0]
    h = GLA_HEADS
    nblk = m // tb
    if backward:
        resets = tuple(s // tb - 1 for s in seq_bounds[1:])
        tok = lambda i: nblk - 1 - i
    else:
        resets = tuple(s // tb for s in seq_bounds[:-1])
        tok = lambda i: i
    k_off = (h * dk) // dk
    v_off = (2 * h * dk) // dv
    lr_col = 1 if backward else 0
    kern = functools.partial(_gla_kernel, backward=backward, n_chunks=tb // GLA_CHUNK,
                             reset_blocks=resets, dk=dk)
    return pl.pallas_call(
        kern,
        grid=(h, nblk),
        in_specs=[pl.BlockSpec((tb, dk), lambda hh, i: (tok(i), hh)),
                  pl.BlockSpec((tb, dk), lambda hh, i: (tok(i), k_off + hh)),
                  pl.BlockSpec((tb, dv), lambda hh, i: (tok(i), v_off + hh)),
                  pl.BlockSpec((tb, LANES), lambda hh, i: (tok(i), lr_col)),
                  pl.BlockSpec((LANES, dk), lambda hh, i: (0, hh)),
                  pl.BlockSpec((1, dk), lambda hh, i: (0, hh))],
        out_specs=pl.BlockSpec((tb, dv), lambda hh, i: (tok(i), hh)),
        out_shape=jax.ShapeDtypeStruct((m, h * dv), F32),
        scratch_shapes=[pltpu.VMEM((dv, dk), F32)],
        compiler_params=_cparams(("arbitrary", "arbitrary"), 40 * 1024 * 1024),
        name="gla_scan_bwd" if backward else "gla_scan_fwd",
    )(z, z, z, lr, up, bias)


def _attn_in_kernel(x_ref, w_ref, *refs):
    out_refs, acc_ref = refs[:-1], refs[-1]
    j = pl.program_id(1)
    n_chunks, tm, _ = acc_ref.shape
    res = jnp.dot(x_ref[...], w_ref[...], preferred_element_type=F32)
    for c in range(n_chunks):
        acc_ref[c] = res[:, c * LANES:(c + 1) * LANES]
    for g, (o_ref, d) in enumerate(zip(out_refs, DIL_RATES)):
        @pl.when(jnp.logical_or(j == g, j >= N_DIL_GROUPS))
        def _(o_ref=o_ref, d=d):
            for r in range(d):
                for c in range(n_chunks):
                    rows = pl.ds(r, tm // d, stride=d) if d > 1 else slice(None)
                    o_ref[r, :, c * LANES:(c + 1) * LANES] = (
                        acc_ref[c, rows, :].astype(o_ref.dtype))


def attn_in(xn, w_att, tm=512):
    m, kd = xn.shape
    w = ATTN_KV_W
    nj = w_att.shape[1] // w
    sect = lambda i, j: (jnp.clip(j - (N_DIL_GROUPS - 1), 0, 2), 0, i, 0)
    return pl.pallas_call(
        _attn_in_kernel,
        grid=(m // tm, nj),
        in_specs=[pl.BlockSpec((tm, kd), lambda i, j: (i, 0)),
                  pl.BlockSpec((kd, w), lambda i, j: (0, j))],
        out_specs=[pl.BlockSpec((None, d, tm // d, w), sect) for d in DIL_RATES],
        out_shape=[jax.ShapeDtypeStruct((3, d, m // d, w), BF16) for d in DIL_RATES],
        scratch_shapes=[pltpu.VMEM((w // LANES, tm, LANES), F32)],
        compiler_params=_cparams(("parallel", "arbitrary"),
                                 4 * (tm * kd + kd * w) + 12 * tm * w + 8 * tm * w),
        name="attn_in",
    )(xn, w_att)


def _dil_kernel(q_ref, kp_ref, ko_ref, kn_ref, vp_ref, vo_ref, vn_ref, o_ref, l_ref, *,
                dilation, group, qb, start_tiles, end_tiles):
    t = pl.program_id(0)
    n_res = q_ref.shape[0]
    res0 = pl.program_id(1) * n_res
    has_prev = jnp.logical_not(_any_equal(t, start_tiles))
    has_next = jnp.logical_not(_any_equal(t + 1, end_tiles))
    r = DIL_WINDOWS[group] // (2 * dilation)
    e = ATTN_HEAD_DIM
    qi = lax.broadcasted_iota(jnp.int32, (r, 3 * r), 0)
    kj = lax.broadcasted_iota(jnp.int32, (r, 3 * r), 1)
    rel = jnp.abs(kj - r - qi)
    in_window = rel <= r
    rel_f = rel.astype(F32) * float(dilation)
    prev_ok = jnp.logical_or(kj >= r, has_prev)
    next_ok = jnp.logical_or(kj < 2 * r, has_next)

    def valid(j):
        v = in_window
        if j == 0:
            v = jnp.logical_and(v, prev_ok)
        if j == qb - 1:
            v = jnp.logical_and(v, next_ok)
        return v

    valids = {j: valid(j) for j in {0, qb - 1, min(1, qb - 1)}}
    n_heads = N_DIL_GROUPS * ATTN_KV_HEADS
    scale = e ** -0.5
    nh = ATTN_KV_HEADS

    def heads(ref, ri, rows):
        return jnp.stack([ref[ri, rows, h * e:(h + 1) * e] for h in range(nh)])

    slopes = [2.0 ** (-ALIBI_MAX_BIAS * (group * nh + h + 1) / n_heads) for h in range(nh)]
    bias = jnp.stack([slope * rel_f for slope in slopes])
    every = slice(None)
    qk_dims = (((2,), (2,)), ((0,), (0,)))
    pv_dims = (((2,), (1,)), ((0,), (0,)))
    for ri in range(n_res):
        k_all = jnp.concatenate([heads(kp_ref, ri, every), heads(ko_ref, ri, every),
                                 heads(kn_ref, ri, every)], axis=1)
        v_all = jnp.concatenate([heads(vp_ref, ri, every), heads(vo_ref, ri, every),
                                 heads(vn_ref, ri, every)], axis=1)
        for j in range(qb):
            q = heads(q_ref, ri, slice(j * r, (j + 1) * r))
            s = lax.dot_general(q, k_all[:, j * r:(j + 3) * r], qk_dims,
                                preferred_element_type=F32)
            s = jnp.where(valids.get(j, in_window)[None], s * scale - bias, NEG_INF)
            mx = jnp.max(s, axis=-1, keepdims=True)
            p = jnp.exp(s - mx)
            den = jnp.sum(p, axis=-1, keepdims=True)
            o = lax.dot_general(p.astype(BF16), v_all[:, j * r:(j + 3) * r], pv_dims,
                                preferred_element_type=F32) / den
            lse = jnp.broadcast_to(mx + jnp.log(den), (nh, r, e))
            if dilation == 1:
                dst = pl.ds(j * r, r)
            else:
                dst = pl.ds(j * r * dilation + res0 + ri, r, stride=dilation)
            for h in range(nh):
                o_ref[h, dst, :] = o[h]
                l_ref[h, dst, :] = lse[h]


def dilated_attention_group(a, group, seq_bounds, qb, n_res):
    d = DIL_RATES[group]
    r = DIL_WINDOWS[group] // (2 * d)
    w = ATTN_KV_W
    rows = a.shape[2]
    m = rows * d
    tile = qb * r * d
    n_tiles = m // tile
    n_blocks = rows // r
    starts = tuple(s // tile for s in seq_bounds[:-1])
    ends = tuple(s // tile for s in seq_bounds[1:])

    own = lambda sec: pl.BlockSpec((None, n_res, qb * r, w), lambda t, rr: (sec, rr, t, 0))
    prev = lambda sec: pl.BlockSpec(
        (None, n_res, r, w), lambda t, rr: (sec, rr, jnp.maximum(t * qb - 1, 0), 0))
    nxt = lambda sec: pl.BlockSpec(
        (None, n_res, r, w), lambda t, rr: (sec, rr, jnp.minimum((t + 1) * qb, n_blocks - 1), 0))
    hm = (ATTN_KV_HEADS, m, ATTN_HEAD_DIM)
    out = pl.BlockSpec((ATTN_KV_HEADS, tile, ATTN_HEAD_DIM), lambda t, rr: (0, t, 0))
    kern = functools.partial(_dil_kernel, dilation=d, group=group, qb=qb,
                             start_tiles=starts, end_tiles=ends)
    return pl.pallas_call(
        kern,
        grid=(n_tiles, d // n_res),
        in_specs=[own(0), prev(1), own(1), nxt(1), prev(2), own(2), nxt(2)],
        out_specs=[out, out],
        out_shape=[jax.ShapeDtypeStruct(hm, F32), jax.ShapeDtypeStruct(hm, F32)],
        compiler_params=_cparams(("parallel", "arbitrary"),
                                 16 * tile * w + n_res * (12 * qb + 16) * r * w + 16 * r * w * 4),
        name=f"dilated_attn_g{group}",
    )(a, a, a, a, a, a, a)


def _mixer_post_kernel(of_ref, ob_ref, r_ref, g_ref, o0_ref, o1_ref, o2_ref,
                       l0_ref, l1_ref, l2_ref, ogla_ref, oatt_ref, *, dv):
    o = of_ref[...] + ob_ref[...]
    r = r_ref[...].astype(F32)
    gate = r * jax.nn.sigmoid(r) * g_ref[...]
    for h in range(GLA_HEADS):
        sl = slice(h * dv, (h + 1) * dv)
        oh = o[:, sl]
        ms = jnp.mean(oh * oh, axis=-1, keepdims=True)
        ogla_ref[:, sl] = (oh * lax.rsqrt(ms + NORM_EPS) * gate[:, sl]).astype(ogla_ref.dtype)
    e = ATTN_HEAD_DIM
    for h in range(ATTN_KV_HEADS):
        l0, l1, l2 = l0_ref[h], l1_ref[h], l2_ref[h]
        mx = jnp.maximum(jnp.maximum(l0, l1), l2)
        w0, w1, w2 = jnp.exp(l0 - mx), jnp.exp(l1 - mx), jnp.exp(l2 - mx)
        num = w0 * o0_ref[h] + w1 * o1_ref[h] + w2 * o2_ref[h]
        oatt_ref[:, h * e:(h + 1) * e] = (num / (w0 + w1 + w2)).astype(oatt_ref.dtype)


def mixer_post(o_f, o_b, z, r_col, gla_norm_g, outs, lses, dv, tr=256):
    m = z.shape[0]
    vw = GLA_HEADS * dv
    kvw = ATTN_KV_W
    row = lambda w: pl.BlockSpec((tr, w), lambda i: (i, 0))
    heads = pl.BlockSpec((ATTN_KV_HEADS, tr, ATTN_HEAD_DIM), lambda i: (0, i, 0))
    return pl.pallas_call(
        functools.partial(_mixer_post_kernel, dv=dv),
        grid=(m // tr,),
        in_specs=[row(vw), row(vw),
                  pl.BlockSpec((tr, vw), lambda i: (i, r_col)),
                  pl.BlockSpec((1, vw), lambda i: (0, 0))] + [heads] * 6,
        out_specs=[row(vw), row(kvw)],
        out_shape=[jax.ShapeDtypeStruct((m, vw), BF16),
                   jax.ShapeDtypeStruct((m, kvw), BF16)],
        compiler_params=_cparams(("parallel",), 36 * 1024 * 1024),
        name="mixer_post",
    )(o_f, o_b, z, gla_norm_g.reshape(1, vw), *outs, *lses)


def _merge_proj_kernel(og_ref, oa_ref, gg_ref, ga_ref, pg_ref, pa_ref, o_ref):
    yg = jnp.dot(og_ref[...], pg_ref[...], preferred_element_type=F32)
    ya = jnp.dot(oa_ref[...], pa_ref[...], preferred_element_type=F32)
    gg = jax.nn.sigmoid(gg_ref[...].astype(F32))
    ga = jax.nn.sigmoid(ga_ref[...].astype(F32))
    o_ref[...] = (gg * yg + ga * ya).astype(o_ref.dtype)


def merge_proj(o_gla, o_att, z, gate_start, proj_gla, proj_attn, tm=1024, tn=512):
    m, vw = o_gla.shape
    kvw = o_att.shape[1]
    d = proj_gla.shape[1]
    nj = d // tn
    gate_col = gate_start // tn
    return pl.pallas_call(
        _merge_proj_kernel,
        grid=(m // tm, nj),
        in_specs=[pl.BlockSpec((tm, vw), lambda i, j: (i, 0)),
                  pl.BlockSpec((tm, kvw), lambda i, j: (i, 0)),
                  pl.BlockSpec((tm, tn), lambda i, j: (i, gate_col + j)),
                  pl.BlockSpec((tm, tn), lambda i, j: (i, gate_col + nj + j)),
                  pl.BlockSpec((vw, tn), lambda i, j: (0, j)),
                  pl.BlockSpec((kvw, tn), lambda i, j: (0, j))],
        out_specs=pl.BlockSpec((tm, tn), lambda i, j: (i, j)),
        out_shape=jax.ShapeDtypeStruct((m, d), BF16),
        compiler_params=_cparams(("parallel", "arbitrary"), 40 * 1024 * 1024),
        name="merge_proj",
    )(o_gla, o_att, z, z, proj_gla, proj_attn)


def _cast_kernel(w_ref, o_ref):
    o_ref[...] = w_ref[...].astype(o_ref.dtype)


def cast_layer(w_stack, layer, tr=128):
    _, rows, cols = w_stack.shape
    return pl.pallas_call(
        _cast_kernel,
        grid=(rows // tr,),
        in_specs=[pl.BlockSpec((None, tr, cols), lambda i: (layer, i, 0))],
        out_specs=pl.BlockSpec((tr, cols), lambda i: (i, 0)),
        out_shape=jax.ShapeDtypeStruct((rows, cols), BF16),
        compiler_params=_cparams(("parallel",), 16 * tr * cols),
        name="cast_layer",
    )(w_stack)


def _split_w_in_kernel(w_ref, main_ref, att_ref, lr_ref, *, lr0, a0, g0):
    w = w_ref[...]
    main_ref[:, :lr0] = w[:, :lr0].astype(BF16)
    main_ref[:, lr0:] = w[:, g0:].astype(BF16)
    att_ref[...] = w[:, a0:g0].astype(BF16)
    rank = (a0 - lr0) // 2
    zeros = jnp.zeros((w.shape[0], LANES - rank), BF16)
    lr_ref[...] = jnp.concatenate([w[:, lr0:lr0 + rank].astype(BF16), zeros,
                                   w[:, lr0 + rank:a0].astype(BF16), zeros], axis=1)


def split_w_in(w_in, layer, lr0, a0, g0, tr=64):
    _, rows, cols = w_in.shape
    widths = (lr0 + cols - g0, g0 - a0, 2 * LANES)
    return pl.pallas_call(
        functools.partial(_split_w_in_kernel, lr0=lr0, a0=a0, g0=g0),
        grid=(rows // tr,),
        in_specs=[pl.BlockSpec((None, tr, cols), lambda i: (layer, i, 0))],
        out_specs=[pl.BlockSpec((tr, wd), lambda i: (i, 0)) for wd in widths],
        out_shape=[jax.ShapeDtypeStruct((rows, wd), BF16) for wd in widths],
        compiler_params=_cparams(("parallel",), 24 * tr * cols),
        name="split_w_in",
    )(w_in)


def _pad_to(x, axis, size):
    pad = [(0, 0)] * x.ndim
    pad[axis] = (0, size - x.shape[axis])
    return jnp.pad(x, pad)


def kernel(x_prompt, x_sample, ffn1_pre_g, ffn1_w_gate, ffn1_w_up, ffn1_w_down, ffn1_post_g,
           mix_pre_g, w_in, gla_decay_up_fwd, gla_decay_bias_fwd, gla_decay_up_bwd,
           gla_decay_bias_bwd, gla_norm_g, proj_gla, proj_attn, w_out, mix_post_g,
           ffn2_pre_g, ffn2_w_gate, ffn2_w_up, ffn2_w_down, ffn2_post_g):
    depth, d, d_ff = ffn1_w_gate.shape
    bp, tp, _ = x_prompt.shape
    bs, ts, _ = x_sample.shape
    mp, ms = bp * tp, bs * ts
    seq_bounds = tuple(b * tp for b in range(bp)) + tuple(mp + b * ts for b in range(bs + 1))

    dk = d // 16
    dv = d // 8
    qk_w = GLA_HEADS * dk
    v_w = GLA_HEADS * dv
    q_w = N_DIL_GROUPS * ATTN_KV_W
    lr0 = 2 * qk_w + 2 * v_w
    a0 = lr0 + 2 * GLA_RANK
    g0 = a0 + q_w + 2 * ATTN_KV_W

    x = (x_prompt.reshape(mp, d), x_sample.reshape(ms, d))
    xn = rmsnorm(x, ffn1_pre_g[0])

    def ffn(x, xn, l, wg, wu, wd, g_post, g_next, out_rows=None):
        h_main, h_tail = ffn_up(xn, cast_layer(wg, l), cast_layer(wu, l))
        y = ffn_down(h_main, h_tail, cast_layer(wd, l, tr=256))
        return norm_residual(y, x, g_post, g_next, 0.5, out_rows=out_rows)

    for l in range(depth):
        x, xn = ffn(x, xn, l, ffn1_w_gate, ffn1_w_up, ffn1_w_down, ffn1_post_g[l], mix_pre_g[l])

        w_main, w_att, w_lr = split_w_in(w_in, l, lr0, a0, g0)
        z = matmul(xn, w_main, BF16, name="mixer_in")
        lr = matmul(xn, w_lr, F32, name="mixer_in_lr")
        o_f = gla_scan(z, lr, _pad_to(gla_decay_up_fwd[l], 0, LANES),
                       gla_decay_bias_fwd[l].reshape(1, qk_w), seq_bounds, dk, dv, False)
        o_b = gla_scan(z, lr, _pad_to(gla_decay_up_bwd[l], 0, LANES),
                       gla_decay_bias_bwd[l].reshape(1, qk_w), seq_bounds, dk, dv, True)

        qkv = attn_in(xn, w_att)
        outs, lses = [], []
        for g, (qb, n_res) in enumerate(((8, 1), (4, 4), (1, 4))):
            o_g, l_g = dilated_attention_group(qkv[g], g, seq_bounds, qb, n_res)
            outs.append(o_g)
            lses.append(l_g)

        o_gla, o_att = mixer_post(o_f, o_b, z, (2 * qk_w + v_w) // v_w, gla_norm_g[l], outs, lses, dv)
        merged = merge_proj(o_gla, o_att, z, lr0, cast_layer(proj_gla, l, tr=256),
                            cast_layer(proj_attn, l, tr=256))
        y = matmul(merged, cast_layer(w_out, l, tr=256), BF16, name="mixer_out")
        x, xn = norm_residual(y, x, mix_post_g[l], ffn2_pre_g[l], 1.0)

        last = l + 1 == depth
        x, xn = ffn(x, xn, l, ffn2_w_gate, ffn2_w_up, ffn2_w_down, ffn2_post_g[l],
                    None if last else ffn1_pre_g[l + 1], out_rows=(mp, ms) if last else None)

    return (x[0].reshape(bp, tp, d), x[1].reshape(bs, ts, d))
```

```python
import functools

import jax
import jax.numpy as jnp
from jax import lax
from jax.experimental import pallas as pl
from jax.experimental.pallas import tpu as pltpu

F32 = jnp.float32
BF16 = jnp.bfloat16

NORM_EPS = 1e-6
NEG_INF = -1e30

GLA_HEADS = 4
GLA_RANK = 16
GLA_TAU = 16.0
GLA_CHUNK = 64
ATTN_KV_HEADS = 8
ATTN_HEAD_DIM = 128
ATTN_KV_W = ATTN_KV_HEADS * ATTN_HEAD_DIM
DIL_WINDOWS = (128, 512, 2048)
DIL_RATES = (1, 4, 16)
N_DIL_GROUPS = 3
ALIBI_MAX_BIAS = 8.0

LANES = 128
VMEM_LIMIT_CAP = 60 * 1024 * 1024


def _cparams(dims, vmem_bytes):
    return pltpu.CompilerParams(dimension_semantics=dims,
                                vmem_limit_bytes=int(min(vmem_bytes * 5 // 4, VMEM_LIMIT_CAP)))


def _any_equal(idx, values):
    return functools.reduce(jnp.logical_or, [idx == v for v in values])


def _part_specs(n_first, tr, d):
    return [pl.BlockSpec((tr, d), lambda i: (jnp.minimum(i, n_first - 1), 0)),
            pl.BlockSpec((tr, d), lambda i: (jnp.maximum(i - n_first, 0), 0))]


def _rmsnorm_kernel(xa_ref, xb_ref, g_ref, o_ref, *, n_first):
    x = jnp.where(pl.program_id(0) < n_first, xa_ref[...], xb_ref[...])
    ms = jnp.mean(x * x, axis=-1, keepdims=True)
    o_ref[...] = (x * lax.rsqrt(ms + NORM_EPS) * g_ref[...]).astype(o_ref.dtype)


def rmsnorm(x_parts, g, tr=256):
    d = x_parts[0].shape[1]
    n_first = x_parts[0].shape[0] // tr
    m = x_parts[0].shape[0] + x_parts[1].shape[0]
    return pl.pallas_call(
        functools.partial(_rmsnorm_kernel, n_first=n_first),
        grid=(m // tr,),
        in_specs=_part_specs(n_first, tr, d) + [pl.BlockSpec((1, d), lambda i: (0, 0))],
        out_specs=pl.BlockSpec((tr, d), lambda i: (i, 0)),
        out_shape=jax.ShapeDtypeStruct((m, d), BF16),
        compiler_params=_cparams(("arbitrary",), 10 * tr * d * 4),
        name="rmsnorm",
    )(*x_parts, g.reshape(1, d))


def _norm_residual_kernel(y_ref, *refs, scale, with_next, n_in_first, n_out_first):
    i = pl.program_id(0)
    refs = list(refs)
    if n_in_first is None:
        x = refs.pop(0)[...]
    else:
        xa_ref, xb_ref = refs.pop(0), refs.pop(0)
        x = jnp.where(i < n_in_first, xa_ref[...], xb_ref[...])
    gp_ref = refs.pop(0)
    y = y_ref[...].astype(F32)
    ms = jnp.mean(y * y, axis=-1, keepdims=True)
    xo = x + scale * (y * lax.rsqrt(ms + NORM_EPS) * gp_ref[...])
    if with_next:
        gn_ref, xn_ref = refs.pop(0), refs.pop()
        ms2 = jnp.mean(xo * xo, axis=-1, keepdims=True)
        xn_ref[...] = (xo * lax.rsqrt(ms2 + NORM_EPS) * gn_ref[...]).astype(xn_ref.dtype)
    if n_out_first is None:
        refs[0][...] = xo
    else:
        @pl.when(i < n_out_first)
        def _():
            refs[0][...] = xo

        @pl.when(i >= n_out_first)
        def _():
            refs[1][...] = xo


def norm_residual(y, x, g_post, g_next, scale, out_rows=None, tr=256):
    m, d = y.shape
    with_next = g_next is not None
    row = pl.BlockSpec((tr, d), lambda i: (i, 0))
    vec = pl.BlockSpec((1, d), lambda i: (0, 0))
    gains = [g_post.reshape(1, d)] + ([g_next.reshape(1, d)] if with_next else [])
    x_parts = x if isinstance(x, (tuple, list)) else None
    n_in_first = None if x_parts is None else x_parts[0].shape[0] // tr
    n_out_first = None if out_rows is None else out_rows[0] // tr
    x_specs = [row] if x_parts is None else _part_specs(n_in_first, tr, d)
    if out_rows is None:
        xo_specs, xo_shapes = [row], [jax.ShapeDtypeStruct((m, d), F32)]
    else:
        xo_specs = _part_specs(n_out_first, tr, d)
        xo_shapes = [jax.ShapeDtypeStruct((rows, d), F32) for rows in out_rows]
    out = pl.pallas_call(
        functools.partial(_norm_residual_kernel, scale=scale, with_next=with_next,
                          n_in_first=n_in_first, n_out_first=n_out_first),
        grid=(m // tr,),
        in_specs=[row] + x_specs + [vec] * len(gains),
        out_specs=xo_specs + ([row] if with_next else []),
        out_shape=xo_shapes + ([jax.ShapeDtypeStruct((m, d), BF16)] if with_next else []),
        compiler_params=_cparams(("arbitrary",), 14 * tr * d * 4),
        name="norm_residual",
    )(y, *(x_parts if x_parts is not None else [x]), *gains)
    xo = out[0] if out_rows is None else tuple(out[:2])
    return xo, (out[-1] if with_next else None)


def _mm_kernel(a_ref, w_ref, o_ref):
    o_ref[...] = jnp.dot(a_ref[...], w_ref[...],
                         preferred_element_type=F32).astype(o_ref.dtype)


def matmul(a, w, out_dtype, tm=1024, tn=1024, name="matmul"):
    m, kd = a.shape
    n = w.shape[1]
    tn = min(tn, n)
    osz = jnp.dtype(out_dtype).itemsize
    return pl.pallas_call(
        _mm_kernel,
        grid=(m // tm, n // tn),
        in_specs=[pl.BlockSpec((tm, kd), lambda i, j: (i, 0)),
                  pl.BlockSpec((kd, tn), lambda i, j: (0, j))],
        out_specs=pl.BlockSpec((tm, tn), lambda i, j: (i, j)),
        out_shape=jax.ShapeDtypeStruct((m, n), out_dtype),
        compiler_params=_cparams(("parallel", "arbitrary"),
                                 4 * (tm * kd + kd * tn) + 2 * tm * tn * osz + 3 * tm * tn * 4),
        name=name,
    )(a, w)


def _ffn_up_kernel(x_ref, wg_ref, wu_ref, o_ref, *, n_tail):
    j = pl.program_id(1)
    tn = o_ref.shape[1]

    def tile(width):
        x = x_ref[...]
        g = jnp.dot(x, wg_ref[:, :width], preferred_element_type=F32)
        u = jnp.dot(x, wu_ref[:, :width], preferred_element_type=F32)
        o_ref[:, :width] = (g * jax.nn.sigmoid(g) * u).astype(o_ref.dtype)

    if n_tail == tn:
        tile(tn)
    else:
        pl.when(j == 0)(lambda: tile(n_tail))
        pl.when(j > 0)(lambda: tile(tn))


def ffn_up(xn, wg, wu, tm=1024, tn=512):
    m, d = xn.shape
    f = wg.shape[1]
    nj = pl.cdiv(f, tn)
    col = lambda j: (j + nj - 1) % nj
    return pl.pallas_call(
        functools.partial(_ffn_up_kernel, n_tail=f - (nj - 1) * tn),
        grid=(m // tm, nj),
        in_specs=[pl.BlockSpec((tm, d), lambda i, j: (i, 0)),
                  pl.BlockSpec((d, tn), lambda i, j: (0, col(j))),
                  pl.BlockSpec((d, tn), lambda i, j: (0, col(j)))],
        out_specs=pl.BlockSpec((tm, tn), lambda i, j: (i, col(j))),
        out_shape=jax.ShapeDtypeStruct((m, f), BF16),
        compiler_params=_cparams(("parallel", "arbitrary"),
                                 4 * tm * d + 8 * d * tn + 4 * tm * tn + 16 * tm * tn),
        name="ffn_up",
    )(xn, wg, wu)


def _ffn_down_kernel(a_ref, w_ref, o_ref, acc_ref, *, k_tail):
    k = pl.program_id(2)
    last = pl.num_programs(2) - 1
    tk = a_ref.shape[1]

    @pl.when(k == 0)
    def _():
        acc_ref[...] = jnp.zeros_like(acc_ref)

    @pl.when(k < last)
    def _():
        acc_ref[...] += jnp.dot(a_ref[...], w_ref[...], preferred_element_type=F32)

    @pl.when(k == last)
    def _():
        if k_tail == tk:
            part = jnp.dot(a_ref[...], w_ref[...], preferred_element_type=F32)
        else:
            part = jnp.dot(a_ref[:, :k_tail], w_ref[:k_tail, :], preferred_element_type=F32)
        o_ref[...] = (acc_ref[...] + part).astype(o_ref.dtype)


def ffn_down(h, wd, tk, tm=1024, tn=1024):
    m, kd = h.shape
    n = wd.shape[1]
    nk = pl.cdiv(kd, tk)
    return pl.pallas_call(
        functools.partial(_ffn_down_kernel, k_tail=kd - (nk - 1) * tk),
        grid=(m // tm, n // tn, nk),
        in_specs=[pl.BlockSpec((tm, tk), lambda i, j, k: (i, k)),
                  pl.BlockSpec((tk, tn), lambda i, j, k: (k, j))],
        out_specs=pl.BlockSpec((tm, tn), lambda i, j, k: (i, j)),
        out_shape=jax.ShapeDtypeStruct((m, n), BF16),
        scratch_shapes=[pltpu.VMEM((tm, tn), F32)],
        compiler_params=_cparams(("parallel", "arbitrary", "arbitrary"),
                                 4 * (tm * tk + tk * tn) + 4 * tm * tn + 16 * tm * tn),
        name="ffn_down",
    )(h, wd)


def _gla_kernel(q_ref, k_ref, v_ref, lr_ref, up_ref, bias_ref, o_ref, st_ref, *,
                backward, n_chunks, reset_blocks, dk):
    i = pl.program_id(1)
    blk = pl.num_programs(1) - 1 - i if backward else i

    @pl.when(_any_equal(blk, reset_blocks))
    def _():
        st_ref[...] = jnp.zeros_like(st_ref)

    c = GLA_CHUNK
    nc = n_chunks
    row = lax.broadcasted_iota(jnp.int32, (c, c), 0)
    col = lax.broadcasted_iota(jnp.int32, (c, c), 1)
    if backward:
        tri, mask = row <= col, row < col
    else:
        tri, mask = row >= col, row >= col
    tri = jnp.broadcast_to(tri.astype(BF16)[None], (nc, c, c))

    def bmm(a, b, ca, cb):
        return lax.dot_general(a, b, (((ca,), (cb,)), ((0,), (0,))), preferred_element_type=F32)

    x = jnp.dot(lr_ref[...].astype(BF16), up_ref[...].astype(BF16),
                preferred_element_type=F32) + bias_ref[...]
    la = (jnp.minimum(x, 0.0) - jnp.log1p(jnp.exp(-jnp.abs(x)))) * (1.0 / GLA_TAU)
    la = la.reshape(nc, c, dk)
    la_hi = la.astype(BF16)
    la_lo = (la - la_hi.astype(F32)).astype(BF16)
    b = bmm(tri, la_hi, 2, 1) + bmm(tri, la_lo, 2, 1)
    tot = jnp.sum(la, axis=1, keepdims=True)
    q = q_ref[...].astype(F32).reshape(nc, c, dk)
    k = k_ref[...].astype(F32).reshape(nc, c, dk)
    v = v_ref[...].reshape(nc, c, v_ref.shape[1])
    qd = (q * (dk ** -0.5) * jnp.exp(b)).astype(BF16)
    kd = (k * jnp.exp(-b)).astype(BF16)
    ks = (k * jnp.exp(tot - b)).astype(BF16)
    scores = jnp.where(mask[None], bmm(qd, kd, 2, 2), 0.0).astype(BF16)
    o_intra = bmm(scores, v, 2, 1)
    st_inc = bmm(v, ks, 1, 1)
    chunk_decay = jnp.exp(tot)

    st = st_ref[...]
    nt = (((1,), (1,)), ((), ()))
    for ci in (reversed(range(nc)) if backward else range(nc)):
        o_ref[ci * c:(ci + 1) * c, :] = o_intra[ci] + lax.dot_general(
            qd[ci], st.astype(BF16), nt, preferred_element_type=F32)
        st = st * chunk_decay[ci] + st_inc[ci]
    st_ref[...] = st


def gla_scan(z, lr, up, bias, seq_bounds, dk, dv, backward, tb=512):
    m = z.shape[0]
    h = GLA_HEADS
    nblk = m // tb
    if backward:
        resets = tuple(s // tb - 1 for s in seq_bounds[1:])
        tok = lambda i: nblk - 1 - i
    else:
        resets = tuple(s // tb for s in seq_bounds[:-1])
        tok = lambda i: i
    k_off = (h * dk) // dk
    v_off = (2 * h * dk) // dv
    lr_col = 1 if backward else 0
    kern = functools.partial(_gla_kernel, backward=backward, n_chunks=tb // GLA_CHUNK,
                             reset_blocks=resets, dk=dk)
    return pl.pallas_call(
        kern,
        grid=(h, nblk),
        in_specs=[pl.BlockSpec((tb, dk), lambda hh, i: (tok(i), hh)),
                  pl.BlockSpec((tb, dk), lambda hh, i: (tok(i), k_off + hh)),
                  pl.BlockSpec((tb, dv), lambda hh, i: (tok(i), v_off + hh)),
                  pl.BlockSpec((tb, LANES), lambda hh, i: (tok(i), lr_col)),
                  pl.BlockSpec((LANES, dk), lambda hh, i: (0, hh)),
                  pl.BlockSpec((1, dk), lambda hh, i: (0, hh))],
        out_specs=pl.BlockSpec((tb, dv), lambda hh, i: (tok(i), hh)),
        out_shape=jax.ShapeDtypeStruct((m, h * dv), F32),
        scratch_shapes=[pltpu.VMEM((dv, dk), F32)],
        compiler_params=_cparams(("arbitrary", "arbitrary"), 40 * 1024 * 1024),
        name="gla_scan_bwd" if backward else "gla_scan_fwd",
    )(z, z, z, lr, up, bias)


def _attn_in_kernel(x_ref, w_ref, *refs, n_tiles):
    out_refs, acc_ref = refs[:-1], refs[-1]
    j = pl.program_id(1)
    _, n_chunks, tm, _ = acc_ref.shape

    def multiply(slot):
        res = jnp.dot(x_ref[...], w_ref[...], preferred_element_type=F32)
        for c in range(n_chunks):
            acc_ref[slot, c] = res[:, c * LANES:(c + 1) * LANES]

    def relayout(tile, slot):
        for g, (o_ref, d) in enumerate(zip(out_refs, DIL_RATES)):
            if tile == g or tile >= N_DIL_GROUPS:
                for r in range(d):
                    for c in range(n_chunks):
                        rows = pl.ds(r, tm // d, stride=d) if d > 1 else slice(None)
                        o_ref[r, :, c * LANES:(c + 1) * LANES] = (
                            acc_ref[slot, c, rows, :].astype(o_ref.dtype))

    for step in range(n_tiles + 1):
        @pl.when(j == step)
        def _(step=step):
            if step < n_tiles:
                multiply(step % 2)
            if step > 0:
                relayout(step - 1, (step - 1) % 2)


def attn_in(xn, w_att, tm=512):
    m, kd = xn.shape
    w = ATTN_KV_W
    nj = w_att.shape[1] // w
    sect = lambda i, j: (jnp.clip(j - 1 - (N_DIL_GROUPS - 1), 0, 2), 0, i, 0)
    return pl.pallas_call(
        functools.partial(_attn_in_kernel, n_tiles=nj),
        grid=(m // tm, nj + 1),
        in_specs=[pl.BlockSpec((tm, kd), lambda i, j: (i, 0)),
                  pl.BlockSpec((kd, w), lambda i, j: (0, jnp.minimum(j, nj - 1)))],
        out_specs=[pl.BlockSpec((None, d, tm // d, w), sect) for d in DIL_RATES],
        out_shape=[jax.ShapeDtypeStruct((3, d, m // d, w), BF16) for d in DIL_RATES],
        scratch_shapes=[pltpu.VMEM((2, w // LANES, tm, LANES), F32)],
        compiler_params=_cparams(("parallel", "arbitrary"),
                                 4 * (tm * kd + kd * w) + 12 * tm * w + 12 * tm * w),
        name="attn_in",
    )(xn, w_att)


def _dil_kernel(q_ref, kp_ref, ko_ref, kn_ref, vp_ref, vo_ref, vn_ref, o_ref, l_ref, *,
                dilation, group, qb, start_tiles, end_tiles):
    t = pl.program_id(0)
    n_res = q_ref.shape[0]
    res0 = pl.program_id(1) * n_res
    has_prev = jnp.logical_not(_any_equal(t, start_tiles))
    has_next = jnp.logical_not(_any_equal(t + 1, end_tiles))
    r = DIL_WINDOWS[group] // (2 * dilation)
    e = ATTN_HEAD_DIM
    qi = lax.broadcasted_iota(jnp.int32, (r, 3 * r), 0)
    kj = lax.broadcasted_iota(jnp.int32, (r, 3 * r), 1)
    rel = jnp.abs(kj - r - qi)
    in_window = rel <= r
    rel_f = rel.astype(F32) * float(dilation)
    prev_ok = jnp.logical_or(kj >= r, has_prev)
    next_ok = jnp.logical_or(kj < 2 * r, has_next)

    def valid(j):
        v = in_window
        if j == 0:
            v = jnp.logical_and(v, prev_ok)
        if j == qb - 1:
            v = jnp.logical_and(v, next_ok)
        return v

    valids = {j: valid(j) for j in {0, qb - 1, min(1, qb - 1)}}
    n_heads = N_DIL_GROUPS * ATTN_KV_HEADS
    scale = e ** -0.5
    nh = ATTN_KV_HEADS

    def heads(ref, ri, rows):
        return jnp.stack([ref[ri, rows, h * e:(h + 1) * e] for h in range(nh)])

    slopes = [2.0 ** (-ALIBI_MAX_BIAS * (group * nh + h + 1) / n_heads) for h in range(nh)]
    bias = jnp.stack([slope * rel_f for slope in slopes])
    every = slice(None)
    qk_dims = (((2,), (2,)), ((0,), (0,)))
    pv_dims = (((2,), (1,)), ((0,), (0,)))
    for ri in range(n_res):
        k_all = jnp.concatenate([heads(kp_ref, ri, every), heads(ko_ref, ri, every),
                                 heads(kn_ref, ri, every)], axis=1)
        v_all = jnp.concatenate([heads(vp_ref, ri, every), heads(vo_ref, ri, every),
                                 heads(vn_ref, ri, every)], axis=1)
        for j in range(qb):
            q = heads(q_ref, ri, slice(j * r, (j + 1) * r))
            s = lax.dot_general(q, k_all[:, j * r:(j + 3) * r], qk_dims,
                                preferred_element_type=F32)
            s = jnp.where(valids.get(j, in_window)[None], s * scale - bias, NEG_INF)
            mx = jnp.max(s, axis=-1, keepdims=True)
            p = jnp.exp(s - mx)
            den = jnp.sum(p, axis=-1, keepdims=True)
            o = lax.dot_general(p.astype(BF16), v_all[:, j * r:(j + 3) * r], pv_dims,
                                preferred_element_type=F32) / den
            lse = jnp.broadcast_to(mx + jnp.log(den), (nh, r, e))
            if dilation == 1:
                dst = pl.ds(j * r, r)
            else:
                dst = pl.ds(j * r * dilation + res0 + ri, r, stride=dilation)
            for h in range(nh):
                o_ref[h, dst, :] = o[h]
                l_ref[h, dst, :] = lse[h]


def dilated_attention_group(a, group, seq_bounds, qb, n_res):
    d = DIL_RATES[group]
    r = DIL_WINDOWS[group] // (2 * d)
    w = ATTN_KV_W
    rows = a.shape[2]
    m = rows * d
    tile = qb * r * d
    n_tiles = m // tile
    n_blocks = rows // r
    starts = tuple(s // tile for s in seq_bounds[:-1])
    ends = tuple(s // tile for s in seq_bounds[1:])

    own = lambda sec: pl.BlockSpec((None, n_res, qb * r, w), lambda t, rr: (sec, rr, t, 0))
    prev = lambda sec: pl.BlockSpec(
        (None, n_res, r, w), lambda t, rr: (sec, rr, jnp.maximum(t * qb - 1, 0), 0))
    nxt = lambda sec: pl.BlockSpec(
        (None, n_res, r, w), lambda t, rr: (sec, rr, jnp.minimum((t + 1) * qb, n_blocks - 1), 0))
    hm = (ATTN_KV_HEADS, m, ATTN_HEAD_DIM)
    out = pl.BlockSpec((ATTN_KV_HEADS, tile, ATTN_HEAD_DIM), lambda t, rr: (0, t, 0))
    kern = functools.partial(_dil_kernel, dilation=d, group=group, qb=qb,
                             start_tiles=starts, end_tiles=ends)
    return pl.pallas_call(
        kern,
        grid=(n_tiles, d // n_res),
        in_specs=[own(0), prev(1), own(1), nxt(1), prev(2), own(2), nxt(2)],
        out_specs=[out, out],
        out_shape=[jax.ShapeDtypeStruct(hm, F32), jax.ShapeDtypeStruct(hm, F32)],
        compiler_params=_cparams(("parallel", "arbitrary"),
                                 16 * tile * w + n_res * (12 * qb + 16) * r * w + 16 * r * w * 4),
        name=f"dilated_attn_g{group}",
    )(a, a, a, a, a, a, a)


def _mixer_post_kernel(of_ref, ob_ref, r_ref, g_ref, o0_ref, o1_ref, o2_ref,
                       l0_ref, l1_ref, l2_ref, ogla_ref, oatt_ref, *, dv):
    o = of_ref[...] + ob_ref[...]
    r = r_ref[...].astype(F32)
    gate = r * jax.nn.sigmoid(r) * g_ref[...]
    for h in range(GLA_HEADS):
        sl = slice(h * dv, (h + 1) * dv)
        oh = o[:, sl]
        ms = jnp.mean(oh * oh, axis=-1, keepdims=True)
        ogla_ref[:, sl] = (oh * lax.rsqrt(ms + NORM_EPS) * gate[:, sl]).astype(ogla_ref.dtype)
    e = ATTN_HEAD_DIM
    for h in range(ATTN_KV_HEADS):
        l0, l1, l2 = l0_ref[h], l1_ref[h], l2_ref[h]
        mx = jnp.maximum(jnp.maximum(l0, l1), l2)
        w0, w1, w2 = jnp.exp(l0 - mx), jnp.exp(l1 - mx), jnp.exp(l2 - mx)
        num = w0 * o0_ref[h] + w1 * o1_ref[h] + w2 * o2_ref[h]
        oatt_ref[:, h * e:(h + 1) * e] = (num / (w0 + w1 + w2)).astype(oatt_ref.dtype)


def mixer_post(o_f, o_b, z, r_col, gla_norm_g, outs, lses, dv, tr=256):
    m = z.shape[0]
    vw = GLA_HEADS * dv
    kvw = ATTN_KV_W
    row = lambda w: pl.BlockSpec((tr, w), lambda i: (i, 0))
    heads = pl.BlockSpec((ATTN_KV_HEADS, tr, ATTN_HEAD_DIM), lambda i: (0, i, 0))
    return pl.pallas_call(
        functools.partial(_mixer_post_kernel, dv=dv),
        grid=(m // tr,),
        in_specs=[row(vw), row(vw),
                  pl.BlockSpec((tr, vw), lambda i: (i, r_col)),
                  pl.BlockSpec((1, vw), lambda i: (0, 0))] + [heads] * 6,
        out_specs=[row(vw), row(kvw)],
        out_shape=[jax.ShapeDtypeStruct((m, vw), BF16),
                   jax.ShapeDtypeStruct((m, kvw), BF16)],
        compiler_params=_cparams(("parallel",), 36 * 1024 * 1024),
        name="mixer_post",
    )(o_f, o_b, z, gla_norm_g.reshape(1, vw), *outs, *lses)


def _merge_proj_kernel(og_ref, oa_ref, gg_ref, ga_ref, pg_ref, pa_ref, o_ref):
    yg = jnp.dot(og_ref[...], pg_ref[...], preferred_element_type=F32)
    ya = jnp.dot(oa_ref[...], pa_ref[...], preferred_element_type=F32)
    gg = jax.nn.sigmoid(gg_ref[...].astype(F32))
    ga = jax.nn.sigmoid(ga_ref[...].astype(F32))
    o_ref[...] = (gg * yg + ga * ya).astype(o_ref.dtype)


def merge_proj(o_gla, o_att, z, gate_start, proj_gla, proj_attn, tm=1024, tn=512):
    m, vw = o_gla.shape
    kvw = o_att.shape[1]
    d = proj_gla.shape[1]
    nj = d // tn
    gate_col = gate_start // tn
    return pl.pallas_call(
        _merge_proj_kernel,
        grid=(m // tm, nj),
        in_specs=[pl.BlockSpec((tm, vw), lambda i, j: (i, 0)),
                  pl.BlockSpec((tm, kvw), lambda i, j: (i, 0)),
                  pl.BlockSpec((tm, tn), lambda i, j: (i, gate_col + j)),
                  pl.BlockSpec((tm, tn), lambda i, j: (i, gate_col + nj + j)),
                  pl.BlockSpec((vw, tn), lambda i, j: (0, j)),
                  pl.BlockSpec((kvw, tn), lambda i, j: (0, j))],
        out_specs=pl.BlockSpec((tm, tn), lambda i, j: (i, j)),
        out_shape=jax.ShapeDtypeStruct((m, d), BF16),
        compiler_params=_cparams(("parallel", "arbitrary"), 40 * 1024 * 1024),
        name="merge_proj",
    )(o_gla, o_att, z, z, proj_gla, proj_attn)


def _cast_kernel(w_ref, o_ref):
    o_ref[...] = w_ref[...].astype(o_ref.dtype)


def cast_layer(w_stack, layer, tr=128):
    _, rows, cols = w_stack.shape
    return pl.pallas_call(
        _cast_kernel,
        grid=(rows // tr,),
        in_specs=[pl.BlockSpec((None, tr, cols), lambda i: (layer, i, 0))],
        out_specs=pl.BlockSpec((tr, cols), lambda i: (i, 0)),
        out_shape=jax.ShapeDtypeStruct((rows, cols), BF16),
        compiler_params=_cparams(("parallel",), 16 * tr * cols),
        name="cast_layer",
    )(w_stack)


def _split_w_in_kernel(w_ref, main_ref, att_ref, lr_ref, *, lr0, a0, g0):
    w = w_ref[...]
    main_ref[:, :lr0] = w[:, :lr0].astype(BF16)
    main_ref[:, lr0:] = w[:, g0:].astype(BF16)
    att_ref[...] = w[:, a0:g0].astype(BF16)
    rank = (a0 - lr0) // 2
    zeros = jnp.zeros((w.shape[0], LANES - rank), BF16)
    lr_ref[...] = jnp.concatenate([w[:, lr0:lr0 + rank].astype(BF16), zeros,
                                   w[:, lr0 + rank:a0].astype(BF16), zeros], axis=1)


def split_w_in(w_in, layer, lr0, a0, g0, tr=64):
    _, rows, cols = w_in.shape
    widths = (lr0 + cols - g0, g0 - a0, 2 * LANES)
    return pl.pallas_call(
        functools.partial(_split_w_in_kernel, lr0=lr0, a0=a0, g0=g0),
        grid=(rows // tr,),
        in_specs=[pl.BlockSpec((None, tr, cols), lambda i: (layer, i, 0))],
        out_specs=[pl.BlockSpec((tr, wd), lambda i: (i, 0)) for wd in widths],
        out_shape=[jax.ShapeDtypeStruct((rows, wd), BF16) for wd in widths],
        compiler_params=_cparams(("parallel",), 24 * tr * cols),
        name="split_w_in",
    )(w_in)


def _pad_to(x, axis, size):
    pad = [(0, 0)] * x.ndim
    pad[axis] = (0, size - x.shape[axis])
    return jnp.pad(x, pad)


def kernel(x_prompt, x_sample, ffn1_pre_g, ffn1_w_gate, ffn1_w_up, ffn1_w_down, ffn1_post_g,
           mix_pre_g, w_in, gla_decay_up_fwd, gla_decay_bias_fwd, gla_decay_up_bwd,
           gla_decay_bias_bwd, gla_norm_g, proj_gla, proj_attn, w_out, mix_post_g,
           ffn2_pre_g, ffn2_w_gate, ffn2_w_up, ffn2_w_down, ffn2_post_g):
    depth, d, d_ff = ffn1_w_gate.shape
    bp, tp, _ = x_prompt.shape
    bs, ts, _ = x_sample.shape
    mp, ms = bp * tp, bs * ts
    seq_bounds = tuple(b * tp for b in range(bp)) + tuple(mp + b * ts for b in range(bs + 1))

    dk = d // 16
    dv = d // 8
    qk_w = GLA_HEADS * dk
    v_w = GLA_HEADS * dv
    q_w = N_DIL_GROUPS * ATTN_KV_W
    lr0 = 2 * qk_w + 2 * v_w
    a0 = lr0 + 2 * GLA_RANK
    g0 = a0 + q_w + 2 * ATTN_KV_W
    ff_tk = -(-d_ff // (4 * 256)) * 256

    x = (x_prompt.reshape(mp, d), x_sample.reshape(ms, d))
    xn = rmsnorm(x, ffn1_pre_g[0])

    def ffn(x, xn, l, wg, wu, wd, g_post, g_next, out_rows=None):
        hmid = ffn_up(xn, cast_layer(wg, l), cast_layer(wu, l))
        y = ffn_down(hmid, cast_layer(wd, l, tr=256), ff_tk)
        return norm_residual(y, x, g_post, g_next, 0.5, out_rows=out_rows)

    for l in range(depth):
        x, xn = ffn(x, xn, l, ffn1_w_gate, ffn1_w_up, ffn1_w_down, ffn1_post_g[l], mix_pre_g[l])

        w_main, w_att, w_lr = split_w_in(w_in, l, lr0, a0, g0)
        z = matmul(xn, w_main, BF16, name="mixer_in")
        lr = matmul(xn, w_lr, F32, name="mixer_in_lr")
        o_f = gla_scan(z, lr, _pad_to(gla_decay_up_fwd[l], 0, LANES),
                       gla_decay_bias_fwd[l].reshape(1, qk_w), seq_bounds, dk, dv, False)
        o_b = gla_scan(z, lr, _pad_to(gla_decay_up_bwd[l], 0, LANES),
                       gla_decay_bias_bwd[l].reshape(1, qk_w), seq_bounds, dk, dv, True)

        qkv = attn_in(xn, w_att)
        outs, lses = [], []
        for g, (qb, n_res) in enumerate(((8, 1), (4, 4), (1, 4))):
            o_g, l_g = dilated_attention_group(qkv[g], g, seq_bounds, qb, n_res)
            outs.append(o_g)
            lses.append(l_g)

        o_gla, o_att = mixer_post(o_f, o_b, z, (2 * qk_w + v_w) // v_w, gla_norm_g[l], outs, lses, dv)
        merged = merge_proj(o_gla, o_att, z, lr0, cast_layer(proj_gla, l, tr=256),
                            cast_layer(proj_attn, l, tr=256))
        y = matmul(merged, cast_layer(w_out, l, tr=256), BF16, name="mixer_out")
        x, xn = norm_residual(y, x, mix_post_g[l], ffn2_pre_g[l], 1.0)

        last = l + 1 == depth
        x, xn = ffn(x, xn, l, ffn2_w_gate, ffn2_w_up, ffn2_w_down, ffn2_post_g[l],
                    None if last else ffn1_pre_g[l + 1], out_rows=(mp, ms) if last else None)

    return (x[0].reshape(bp, tp, d), x[1].reshape(bs, ts, d))
```

```python
import functools

import jax
import jax.numpy as jnp
from jax import lax
from jax.experimental import pallas as pl
from jax.experimental.pallas import tpu as pltpu

F32 = jnp.float32
BF16 = jnp.bfloat16

NORM_EPS = 1e-6
NEG_INF = -1e30

GLA_HEADS = 4
GLA_RANK = 16
GLA_TAU = 16.0
GLA_CHUNK = 64
ATTN_KV_HEADS = 8
ATTN_HEAD_DIM = 128
ATTN_KV_W = ATTN_KV_HEADS * ATTN_HEAD_DIM
DIL_WINDOWS = (128, 512, 2048)
DIL_RATES = (1, 4, 16)
N_DIL_GROUPS = 3
ALIBI_MAX_BIAS = 8.0

LANES = 128
VMEM_LIMIT_CAP = 60 * 1024 * 1024


def _cparams(dims, vmem_bytes):
    return pltpu.CompilerParams(dimension_semantics=dims,
                                vmem_limit_bytes=int(min(vmem_bytes * 5 // 4, VMEM_LIMIT_CAP)))


def _any_equal(idx, values):
    return functools.reduce(jnp.logical_or, [idx == v for v in values])


def _part_specs(n_first, tr, d):
    return [pl.BlockSpec((tr, d), lambda i: (jnp.minimum(i, n_first - 1), 0)),
            pl.BlockSpec((tr, d), lambda i: (jnp.maximum(i - n_first, 0), 0))]


def _rmsnorm_kernel(xa_ref, xb_ref, g_ref, o_ref, *, n_first):
    x = jnp.where(pl.program_id(0) < n_first, xa_ref[...], xb_ref[...])
    ms = jnp.mean(x * x, axis=-1, keepdims=True)
    o_ref[...] = (x * lax.rsqrt(ms + NORM_EPS) * g_ref[...]).astype(o_ref.dtype)


def rmsnorm(x_parts, g, tr=256):
    d = x_parts[0].shape[1]
    n_first = x_parts[0].shape[0] // tr
    m = x_parts[0].shape[0] + x_parts[1].shape[0]
    return pl.pallas_call(
        functools.partial(_rmsnorm_kernel, n_first=n_first),
        grid=(m // tr,),
        in_specs=_part_specs(n_first, tr, d) + [pl.BlockSpec((1, d), lambda i: (0, 0))],
        out_specs=pl.BlockSpec((tr, d), lambda i: (i, 0)),
        out_shape=jax.ShapeDtypeStruct((m, d), BF16),
        compiler_params=_cparams(("arbitrary",), 10 * tr * d * 4),
        name="rmsnorm",
    )(*x_parts, g.reshape(1, d))


def _norm_residual_kernel(y_ref, *refs, scale, with_next, n_in_first, n_out_first):
    i = pl.program_id(0)
    refs = list(refs)
    if n_in_first is None:
        x = refs.pop(0)[...]
    else:
        xa_ref, xb_ref = refs.pop(0), refs.pop(0)
        x = jnp.where(i < n_in_first, xa_ref[...], xb_ref[...])
    gp_ref = refs.pop(0)
    y = y_ref[...].astype(F32)
    ms = jnp.mean(y * y, axis=-1, keepdims=True)
    xo = x + scale * (y * lax.rsqrt(ms + NORM_EPS) * gp_ref[...])
    if with_next:
        gn_ref, xn_ref = refs.pop(0), refs.pop()
        ms2 = jnp.mean(xo * xo, axis=-1, keepdims=True)
        xn_ref[...] = (xo * lax.rsqrt(ms2 + NORM_EPS) * gn_ref[...]).astype(xn_ref.dtype)
    if n_out_first is None:
        refs[0][...] = xo
    else:
        @pl.when(i < n_out_first)
        def _():
            refs[0][...] = xo

        @pl.when(i >= n_out_first)
        def _():
            refs[1][...] = xo


def norm_residual(y, x, g_post, g_next, scale, out_rows=None, tr=256):
    m, d = y.shape
    with_next = g_next is not None
    row = pl.BlockSpec((tr, d), lambda i: (i, 0))
    vec = pl.BlockSpec((1, d), lambda i: (0, 0))
    gains = [g_post.reshape(1, d)] + ([g_next.reshape(1, d)] if with_next else [])
    x_parts = x if isinstance(x, (tuple, list)) else None
    n_in_first = None if x_parts is None else x_parts[0].shape[0] // tr
    n_out_first = None if out_rows is None else out_rows[0] // tr
    x_specs = [row] if x_parts is None else _part_specs(n_in_first, tr, d)
    if out_rows is None:
        xo_specs, xo_shapes = [row], [jax.ShapeDtypeStruct((m, d), F32)]
    else:
        xo_specs = _part_specs(n_out_first, tr, d)
        xo_shapes = [jax.ShapeDtypeStruct((rows, d), F32) for rows in out_rows]
    out = pl.pallas_call(
        functools.partial(_norm_residual_kernel, scale=scale, with_next=with_next,
                          n_in_first=n_in_first, n_out_first=n_out_first),
        grid=(m // tr,),
        in_specs=[row] + x_specs + [vec] * len(gains),
        out_specs=xo_specs + ([row] if with_next else []),
        out_shape=xo_shapes + ([jax.ShapeDtypeStruct((m, d), BF16)] if with_next else []),
        compiler_params=_cparams(("arbitrary",), 14 * tr * d * 4),
        name="norm_residual",
    )(y, *(x_parts if x_parts is not None else [x]), *gains)
    xo = out[0] if out_rows is None else tuple(out[:2])
    return xo, (out[-1] if with_next else None)


class Ride:
    def __init__(self, w_stack, layer, max_blocks, min_tr=32):
        _, self.rows, self.cols = w_stack.shape
        tr = min_tr
        while self.rows % tr or self.rows // tr > max_blocks:
            tr *= 2
            assert tr <= self.rows
        self.w_stack, self.layer, self.tr = w_stack, layer, tr
        self.n_blocks = self.rows // tr

    def specs(self, step_of):
        blk = lambda *g: jnp.minimum(step_of(*g), self.n_blocks - 1)
        src = pl.BlockSpec((None, self.tr, self.cols), lambda *g: (self.layer, blk(*g), 0))
        dst = pl.BlockSpec((self.tr, self.cols), lambda *g: (blk(*g), 0))
        return src, dst

    @property
    def out_shape(self):
        return jax.ShapeDtypeStruct((self.rows, self.cols), BF16)

    @property
    def vmem_bytes(self):
        return 12 * self.tr * self.cols


def _split_rides(refs, n_base_in, n_base_out, n_rides):
    a = n_base_in
    b = a + n_rides
    c = b + n_base_out
    d = c + n_rides
    return refs[:a], refs[a:b], refs[b:c], refs[c:d], refs[d:]


def _do_rides(step, rides_n_blocks, srcs, dsts):
    for n_blocks, src, dst in zip(rides_n_blocks, srcs, dsts):
        @pl.when(step < n_blocks)
        def _(src=src, dst=dst):
            dst[...] = src[...].astype(dst.dtype)


def _mm_kernel(*refs, ride_blocks):
    (a_ref, w_ref), srcs, (o_ref,), dsts, _ = _split_rides(refs, 2, 1, len(ride_blocks))
    o_ref[...] = jnp.dot(a_ref[...], w_ref[...],
                         preferred_element_type=F32).astype(o_ref.dtype)
    _do_rides(pl.program_id(0) * pl.num_programs(1) + pl.program_id(1), ride_blocks, srcs, dsts)


def matmul(a, w, out_dtype, tm=1024, tn=1024, rides=(), name="matmul"):
    m, kd = a.shape
    n = w.shape[1]
    tn = min(tn, n)
    nj = n // tn
    osz = jnp.dtype(out_dtype).itemsize
    assert all(r.n_blocks <= (m // tm) * nj for r in rides)
    ride_specs = [r.specs(lambda i, j: i * nj + j) for r in rides]
    out = pl.pallas_call(
        functools.partial(_mm_kernel, ride_blocks=tuple(r.n_blocks for r in rides)),
        grid=(m // tm, nj),
        in_specs=[pl.BlockSpec((tm, kd), lambda i, j: (i, 0)),
                  pl.BlockSpec((kd, tn), lambda i, j: (0, j))] + [s for s, _ in ride_specs],
        out_specs=[pl.BlockSpec((tm, tn), lambda i, j: (i, j))] + [d for _, d in ride_specs],
        out_shape=[jax.ShapeDtypeStruct((m, n), out_dtype)] + [r.out_shape for r in rides],
        compiler_params=_cparams(("arbitrary", "arbitrary"),
                                 4 * (tm * kd + kd * tn) + 2 * tm * tn * osz + 3 * tm * tn * 4
                                 + sum(r.vmem_bytes for r in rides)),
        name=name,
    )(a, w, *[r.w_stack for r in rides])
    return out[0] if not rides else out


def _ffn_up_kernel(*refs, n_tail, ride_blocks):
    (x_ref, wg_ref, wu_ref), srcs, (o_ref,), dsts, _ = _split_rides(refs, 3, 1, len(ride_blocks))
    j = pl.program_id(1)
    tn = o_ref.shape[1]
    _do_rides(pl.program_id(0) * pl.num_programs(1) + j, ride_blocks, srcs, dsts)

    def tile(width):
        x = x_ref[...]
        g = jnp.dot(x, wg_ref[:, :width], preferred_element_type=F32)
        u = jnp.dot(x, wu_ref[:, :width], preferred_element_type=F32)
        o_ref[:, :width] = (g * jax.nn.sigmoid(g) * u).astype(o_ref.dtype)

    if n_tail == tn:
        tile(tn)
    else:
        pl.when(j == 0)(lambda: tile(n_tail))
        pl.when(j > 0)(lambda: tile(tn))


def ffn_up(xn, wg, wu, tm=1024, tn=512, rides=()):
    m, d = xn.shape
    f = wg.shape[1]
    nj = pl.cdiv(f, tn)
    col = lambda j: (j + nj - 1) % nj
    assert all(r.n_blocks <= (m // tm) * nj for r in rides)
    ride_specs = [r.specs(lambda i, j: i * nj + j) for r in rides]
    out = pl.pallas_call(
        functools.partial(_ffn_up_kernel, n_tail=f - (nj - 1) * tn,
                          ride_blocks=tuple(r.n_blocks for r in rides)),
        grid=(m // tm, nj),
        in_specs=[pl.BlockSpec((tm, d), lambda i, j: (i, 0)),
                  pl.BlockSpec((d, tn), lambda i, j: (0, col(j))),
                  pl.BlockSpec((d, tn), lambda i, j: (0, col(j)))] + [s for s, _ in ride_specs],
        out_specs=[pl.BlockSpec((tm, tn), lambda i, j: (i, col(j)))] + [d for _, d in ride_specs],
        out_shape=[jax.ShapeDtypeStruct((m, f), BF16)] + [r.out_shape for r in rides],
        compiler_params=_cparams(("arbitrary", "arbitrary"),
                                 4 * tm * d + 8 * d * tn + 4 * tm * tn + 16 * tm * tn
                                 + sum(r.vmem_bytes for r in rides)),
        name="ffn_up",
    )(xn, wg, wu, *[r.w_stack for r in rides])
    return out


def _ffn_down_kernel(*refs, k_tail, ride_blocks):
    (a_ref, w_ref), srcs, (o_ref,), dsts, (acc_ref,) = _split_rides(refs, 2, 1, len(ride_blocks))
    k = pl.program_id(2)
    last = pl.num_programs(2) - 1
    tk = a_ref.shape[1]
    step = (pl.program_id(0) * pl.num_programs(1) + pl.program_id(1)) * pl.num_programs(2) + k
    _do_rides(step, ride_blocks, srcs, dsts)

    @pl.when(k == 0)
    def _():
        acc_ref[...] = jnp.zeros_like(acc_ref)

    @pl.when(k < last)
    def _():
        acc_ref[...] += jnp.dot(a_ref[...], w_ref[...], preferred_element_type=F32)

    @pl.when(k == last)
    def _():
        if k_tail == tk:
            part = jnp.dot(a_ref[...], w_ref[...], preferred_element_type=F32)
        else:
            part = jnp.dot(a_ref[:, :k_tail], w_ref[:k_tail, :], preferred_element_type=F32)
        o_ref[...] = (acc_ref[...] + part).astype(o_ref.dtype)


def ffn_down(h, wd, tk, tm=1024, tn=1024, rides=()):
    m, kd = h.shape
    n = wd.shape[1]
    nj = n // tn
    nk = pl.cdiv(kd, tk)
    assert all(r.n_blocks <= (m // tm) * nj * nk for r in rides)
    ride_specs = [r.specs(lambda i, j, k: (i * nj + j) * nk + k) for r in rides]
    out = pl.pallas_call(
        functools.partial(_ffn_down_kernel, k_tail=kd - (nk - 1) * tk,
                          ride_blocks=tuple(r.n_blocks for r in rides)),
        grid=(m // tm, nj, nk),
        in_specs=[pl.BlockSpec((tm, tk), lambda i, j, k: (i, k)),
                  pl.BlockSpec((tk, tn), lambda i, j, k: (k, j))] + [s for s, _ in ride_specs],
        out_specs=[pl.BlockSpec((tm, tn), lambda i, j, k: (i, j))] + [d for _, d in ride_specs],
        out_shape=[jax.ShapeDtypeStruct((m, n), BF16)] + [r.out_shape for r in rides],
        scratch_shapes=[pltpu.VMEM((tm, tn), F32)],
        compiler_params=_cparams(("arbitrary", "arbitrary", "arbitrary"),
                                 4 * (tm * tk + tk * tn) + 4 * tm * tn + 16 * tm * tn
                                 + sum(r.vmem_bytes for r in rides)),
        name="ffn_down",
    )(h, wd, *[r.w_stack for r in rides])
    return out


def _gla_kernel(q_ref, k_ref, v_ref, lr_ref, up_ref, bias_ref, o_ref, st_ref, *,
                backward, n_chunks, reset_blocks, dk):
    i = pl.program_id(1)
    blk = pl.num_programs(1) - 1 - i if backward else i

    @pl.when(_any_equal(blk, reset_blocks))
    def _():
        st_ref[...] = jnp.zeros_like(st_ref)

    c = GLA_CHUNK
    nc = n_chunks
    row = lax.broadcasted_iota(jnp.int32, (c, c), 0)
    col = lax.broadcasted_iota(jnp.int32, (c, c), 1)
    if backward:
        tri, mask = row <= col, row < col
    else:
        tri, mask = row >= col, row >= col
    tri = jnp.broadcast_to(tri.astype(BF16)[None], (nc, c, c))

    def bmm(a, b, ca, cb):
        return lax.dot_general(a, b, (((ca,), (cb,)), ((0,), (0,))), preferred_element_type=F32)

    x = jnp.dot(lr_ref[...].astype(BF16), up_ref[...].astype(BF16),
                preferred_element_type=F32) + bias_ref[...]
    la = (jnp.minimum(x, 0.0) - jnp.log1p(jnp.exp(-jnp.abs(x)))) * (1.0 / GLA_TAU)
    la = la.reshape(nc, c, dk)
    la_hi = la.astype(BF16)
    la_lo = (la - la_hi.astype(F32)).astype(BF16)
    b = bmm(tri, la_hi, 2, 1) + bmm(tri, la_lo, 2, 1)
    tot = jnp.sum(la, axis=1, keepdims=True)
    q = q_ref[...].astype(F32).reshape(nc, c, dk)
    k = k_ref[...].astype(F32).reshape(nc, c, dk)
    v = v_ref[...].reshape(nc, c, v_ref.shape[1])
    qd = (q * (dk ** -0.5) * jnp.exp(b)).astype(BF16)
    kd = (k * jnp.exp(-b)).astype(BF16)
    ks = (k * jnp.exp(tot - b)).astype(BF16)
    scores = jnp.where(mask[None], bmm(qd, kd, 2, 2), 0.0).astype(BF16)
    o_intra = bmm(scores, v, 2, 1)
    st_inc = bmm(v, ks, 1, 1)
    chunk_decay = jnp.exp(tot)

    st = st_ref[...]
    nt = (((1,), (1,)), ((), ()))
    for ci in (reversed(range(nc)) if backward else range(nc)):
        o_ref[ci * c:(ci + 1) * c, :] = o_intra[ci] + lax.dot_general(
            qd[ci], st.astype(BF16), nt, preferred_element_type=F32)
        st = st * chunk_decay[ci] + st_inc[ci]
    st_ref[...] = st


def gla_scan(z, lr, up, bias, seq_bounds, dk, dv, backward, tb=512):
    m = z.shape[0]
    h = GLA_HEADS
    nblk = m // tb
    if backward:
        resets = tuple(s // tb - 1 for s in seq_bounds[1:])
        tok = lambda i: nblk - 1 - i
    else:
        resets = tuple(s // tb for s in seq_bounds[:-1])
        tok = lambda i: i
    k_off = (h * dk) // dk
    v_off = (2 * h * dk) // dv
    lr_col = 1 if backward else 0
    kern = functools.partial(_gla_kernel, backward=backward, n_chunks=tb // GLA_CHUNK,
                             reset_blocks=resets, dk=dk)
    return pl.pallas_call(
        kern,
        grid=(h, nblk),
        in_specs=[pl.BlockSpec((tb, dk), lambda hh, i: (tok(i), hh)),
                  pl.BlockSpec((tb, dk), lambda hh, i: (tok(i), k_off + hh)),
                  pl.BlockSpec((tb, dv), lambda hh, i: (tok(i), v_off + hh)),
                  pl.BlockSpec((tb, LANES), lambda hh, i: (tok(i), lr_col)),
                  pl.BlockSpec((LANES, dk), lambda hh, i: (0, hh)),
                  pl.BlockSpec((1, dk), lambda hh, i: (0, hh))],
        out_specs=pl.BlockSpec((tb, dv), lambda hh, i: (tok(i), hh)),
        out_shape=jax.ShapeDtypeStruct((m, h * dv), F32),
        scratch_shapes=[pltpu.VMEM((dv, dk), F32)],
        compiler_params=_cparams(("arbitrary", "arbitrary"), 40 * 1024 * 1024),
        name="gla_scan_bwd" if backward else "gla_scan_fwd",
    )(z, z, z, lr, up, bias)


def _attn_in_kernel(x_ref, w_ref, *refs):
    out_refs, acc_ref = refs[:-1], refs[-1]
    j = pl.program_id(1)
    n_chunks, tm, _ = acc_ref.shape
    res = jnp.dot(x_ref[...], w_ref[...], preferred_element_type=F32)
    for c in range(n_chunks):
        acc_ref[c] = res[:, c * LANES:(c + 1) * LANES]
    for g, (o_ref, d) in enumerate(zip(out_refs, DIL_RATES)):
        @pl.when(jnp.logical_or(j == g, j >= N_DIL_GROUPS))
        def _(o_ref=o_ref, d=d):
            for r in range(d):
                for c in range(n_chunks):
                    rows = pl.ds(r, tm // d, stride=d) if d > 1 else slice(None)
                    o_ref[r, :, c * LANES:(c + 1) * LANES] = (
                        acc_ref[c, rows, :].astype(o_ref.dtype))


def attn_in(xn, w_att, tm=512):
    m, kd = xn.shape
    w = ATTN_KV_W
    nj = w_att.shape[1] // w
    sect = lambda i, j: (jnp.clip(j - (N_DIL_GROUPS - 1), 0, 2), 0, i, 0)
    return pl.pallas_call(
        _attn_in_kernel,
        grid=(m // tm, nj),
        in_specs=[pl.BlockSpec((tm, kd), lambda i, j: (i, 0)),
                  pl.BlockSpec((kd, w), lambda i, j: (0, j))],
        out_specs=[pl.BlockSpec((None, d, tm // d, w), sect) for d in DIL_RATES],
        out_shape=[jax.ShapeDtypeStruct((3, d, m // d, w), BF16) for d in DIL_RATES],
        scratch_shapes=[pltpu.VMEM((w // LANES, tm, LANES), F32)],
        compiler_params=_cparams(("parallel", "arbitrary"),
                                 4 * (tm * kd + kd * w) + 12 * tm * w + 8 * tm * w),
        name="attn_in",
    )(xn, w_att)


def _dil_kernel(q_ref, kp_ref, ko_ref, kn_ref, vp_ref, vo_ref, vn_ref, o_ref, l_ref, *,
                dilation, group, qb, start_tiles, end_tiles):
    t = pl.program_id(0)
    n_res = q_ref.shape[0]
    res0 = pl.program_id(1) * n_res
    has_prev = jnp.logical_not(_any_equal(t, start_tiles))
    has_next = jnp.logical_not(_any_equal(t + 1, end_tiles))
    r = DIL_WINDOWS[group] // (2 * dilation)
    e = ATTN_HEAD_DIM
    qi = lax.broadcasted_iota(jnp.int32, (r, 3 * r), 0)
    kj = lax.broadcasted_iota(jnp.int32, (r, 3 * r), 1)
    rel = jnp.abs(kj - r - qi)
    in_window = rel <= r
    rel_f = rel.astype(F32) * float(dilation)
    prev_ok = jnp.logical_or(kj >= r, has_prev)
    next_ok = jnp.logical_or(kj < 2 * r, has_next)

    def valid(j):
        v = in_window
        if j == 0:
            v = jnp.logical_and(v, prev_ok)
        if j == qb - 1:
            v = jnp.logical_and(v, next_ok)
        return v

    valids = {j: valid(j) for j in {0, qb - 1, min(1, qb - 1)}}
    n_heads = N_DIL_GROUPS * ATTN_KV_HEADS
    scale = e ** -0.5
    nh = ATTN_KV_HEADS

    def heads(ref, ri, rows):
        return jnp.stack([ref[ri, rows, h * e:(h + 1) * e] for h in range(nh)])

    slopes = [2.0 ** (-ALIBI_MAX_BIAS * (group * nh + h + 1) / n_heads) for h in range(nh)]
    bias = jnp.stack([slope * rel_f for slope in slopes])
    every = slice(None)
    qk_dims = (((2,), (2,)), ((0,), (0,)))
    pv_dims = (((2,), (1,)), ((0,), (0,)))
    for ri in range(n_res):
        k_all = jnp.concatenate([heads(kp_ref, ri, every), heads(ko_ref, ri, every),
                                 heads(kn_ref, ri, every)], axis=1)
        v_all = jnp.concatenate([heads(vp_ref, ri, every), heads(vo_ref, ri, every),
                                 heads(vn_ref, ri, every)], axis=1)
        for j in range(qb):
            q = heads(q_ref, ri, slice(j * r, (j + 1) * r))
            s = lax.dot_general(q, k_all[:, j * r:(j + 3) * r], qk_dims,
                                preferred_element_type=F32)
            s = jnp.where(valids.get(j, in_window)[None], s * scale - bias, NEG_INF)
            mx = jnp.max(s, axis=-1, keepdims=True)
            p = jnp.exp(s - mx)
            den = jnp.sum(p, axis=-1, keepdims=True)
            o = lax.dot_general(p.astype(BF16), v_all[:, j * r:(j + 3) * r], pv_dims,
                                preferred_element_type=F32) / den
            lse = jnp.broadcast_to(mx + jnp.log(den), (nh, r, e))
            if dilation == 1:
                dst = pl.ds(j * r, r)
            else:
                dst = pl.ds(j * r * dilation + res0 + ri, r, stride=dilation)
            for h in range(nh):
                o_ref[h, dst, :] = o[h]
                l_ref[h, dst, :] = lse[h]


def dilated_attention_group(a, group, seq_bounds, qb, n_res):
    d = DIL_RATES[group]
    r = DIL_WINDOWS[group] // (2 * d)
    w = ATTN_KV_W
    rows = a.shape[2]
    m = rows * d
    tile = qb * r * d
    n_tiles = m // tile
    n_blocks = rows // r
    starts = tuple(s // tile for s in seq_bounds[:-1])
    ends = tuple(s // tile for s in seq_bounds[1:])

    own = lambda sec: pl.BlockSpec((None, n_res, qb * r, w), lambda t, rr: (sec, rr, t, 0))
    prev = lambda sec: pl.BlockSpec(
        (None, n_res, r, w), lambda t, rr: (sec, rr, jnp.maximum(t * qb - 1, 0), 0))
    nxt = lambda sec: pl.BlockSpec(
        (None, n_res, r, w), lambda t, rr: (sec, rr, jnp.minimum((t + 1) * qb, n_blocks - 1), 0))
    hm = (ATTN_KV_HEADS, m, ATTN_HEAD_DIM)
    out = pl.BlockSpec((ATTN_KV_HEADS, tile, ATTN_HEAD_DIM), lambda t, rr: (0, t, 0))
    kern = functools.partial(_dil_kernel, dilation=d, group=group, qb=qb,
                             start_tiles=starts, end_tiles=ends)
    return pl.pallas_call(
        kern,
        grid=(n_tiles, d // n_res),
        in_specs=[own(0), prev(1), own(1), nxt(1), prev(2), own(2), nxt(2)],
        out_specs=[out, out],
        out_shape=[jax.ShapeDtypeStruct(hm, F32), jax.ShapeDtypeStruct(hm, F32)],
        compiler_params=_cparams(("parallel", "arbitrary"),
                                 16 * tile * w + n_res * (12 * qb + 16) * r * w + 16 * r * w * 4),
        name=f"dilated_attn_g{group}",
    )(a, a, a, a, a, a, a)


def _mixer_post_kernel(of_ref, ob_ref, r_ref, g_ref, o0_ref, o1_ref, o2_ref,
                       l0_ref, l1_ref, l2_ref, ogla_ref, oatt_ref, *, dv):
    o = of_ref[...] + ob_ref[...]
    r = r_ref[...].astype(F32)
    gate = r * jax.nn.sigmoid(r) * g_ref[...]
    for h in range(GLA_HEADS):
        sl = slice(h * dv, (h + 1) * dv)
        oh = o[:, sl]
        ms = jnp.mean(oh * oh, axis=-1, keepdims=True)
        ogla_ref[:, sl] = (oh * lax.rsqrt(ms + NORM_EPS) * gate[:, sl]).astype(ogla_ref.dtype)
    e = ATTN_HEAD_DIM
    for h in range(ATTN_KV_HEADS):
        l0, l1, l2 = l0_ref[h], l1_ref[h], l2_ref[h]
        mx = jnp.maximum(jnp.maximum(l0, l1), l2)
        w0, w1, w2 = jnp.exp(l0 - mx), jnp.exp(l1 - mx), jnp.exp(l2 - mx)
        num = w0 * o0_ref[h] + w1 * o1_ref[h] + w2 * o2_ref[h]
        oatt_ref[:, h * e:(h + 1) * e] = (num / (w0 + w1 + w2)).astype(oatt_ref.dtype)


def mixer_post(o_f, o_b, z, r_col, gla_norm_g, outs, lses, dv, tr=256):
    m = z.shape[0]
    vw = GLA_HEADS * dv
    kvw = ATTN_KV_W
    row = lambda w: pl.BlockSpec((tr, w), lambda i: (i, 0))
    heads = pl.BlockSpec((ATTN_KV_HEADS, tr, ATTN_HEAD_DIM), lambda i: (0, i, 0))
    return pl.pallas_call(
        functools.partial(_mixer_post_kernel, dv=dv),
        grid=(m // tr,),
        in_specs=[row(vw), row(vw),
                  pl.BlockSpec((tr, vw), lambda i: (i, r_col)),
                  pl.BlockSpec((1, vw), lambda i: (0, 0))] + [heads] * 6,
        out_specs=[row(vw), row(kvw)],
        out_shape=[jax.ShapeDtypeStruct((m, vw), BF16),
                   jax.ShapeDtypeStruct((m, kvw), BF16)],
        compiler_params=_cparams(("parallel",), 36 * 1024 * 1024),
        name="mixer_post",
    )(o_f, o_b, z, gla_norm_g.reshape(1, vw), *outs, *lses)


def _merge_proj_kernel(og_ref, oa_ref, gg_ref, ga_ref, pg_ref, pa_ref, o_ref):
    yg = jnp.dot(og_ref[...], pg_ref[...], preferred_element_type=F32)
    ya = jnp.dot(oa_ref[...], pa_ref[...], preferred_element_type=F32)
    gg = jax.nn.sigmoid(gg_ref[...].astype(F32))
    ga = jax.nn.sigmoid(ga_ref[...].astype(F32))
    o_ref[...] = (gg * yg + ga * ya).astype(o_ref.dtype)


def merge_proj(o_gla, o_att, z, gate_start, proj_gla, proj_attn, tm=1024, tn=512):
    m, vw = o_gla.shape
    kvw = o_att.shape[1]
    d = proj_gla.shape[1]
    nj = d // tn
    gate_col = gate_start // tn
    return pl.pallas_call(
        _merge_proj_kernel,
        grid=(m // tm, nj),
        in_specs=[pl.BlockSpec((tm, vw), lambda i, j: (i, 0)),
                  pl.BlockSpec((tm, kvw), lambda i, j: (i, 0)),
                  pl.BlockSpec((tm, tn), lambda i, j: (i, gate_col + j)),
                  pl.BlockSpec((tm, tn), lambda i, j: (i, gate_col + nj + j)),
                  pl.BlockSpec((vw, tn), lambda i, j: (0, j)),
                  pl.BlockSpec((kvw, tn), lambda i, j: (0, j))],
        out_specs=pl.BlockSpec((tm, tn), lambda i, j: (i, j)),
        out_shape=jax.ShapeDtypeStruct((m, d), BF16),
        compiler_params=_cparams(("parallel", "arbitrary"), 40 * 1024 * 1024),
        name="merge_proj",
    )(o_gla, o_att, z, z, proj_gla, proj_attn)


def _cast_kernel(w_ref, o_ref):
    o_ref[...] = w_ref[...].astype(o_ref.dtype)


def cast_layer(w_stack, layer, tr=128):
    _, rows, cols = w_stack.shape
    return pl.pallas_call(
        _cast_kernel,
        grid=(rows // tr,),
        in_specs=[pl.BlockSpec((None, tr, cols), lambda i: (layer, i, 0))],
        out_specs=pl.BlockSpec((tr, cols), lambda i: (i, 0)),
        out_shape=jax.ShapeDtypeStruct((rows, cols), BF16),
        compiler_params=_cparams(("parallel",), 16 * tr * cols),
        name="cast_layer",
    )(w_stack)


def _pad_to(x, axis, size):
    pad = [(0, 0)] * x.ndim
    pad[axis] = (0, size - x.shape[axis])
    return jnp.pad(x, pad)


def kernel(x_prompt, x_sample, ffn1_pre_g, ffn1_w_gate, ffn1_w_up, ffn1_w_down, ffn1_post_g,
           mix_pre_g, w_in, gla_decay_up_fwd, gla_decay_bias_fwd, gla_decay_up_bwd,
           gla_decay_bias_bwd, gla_norm_g, proj_gla, proj_attn, w_out, mix_post_g,
           ffn2_pre_g, ffn2_w_gate, ffn2_w_up, ffn2_w_down, ffn2_post_g):
    depth, d, d_ff = ffn1_w_gate.shape
    bp, tp, _ = x_prompt.shape
    bs, ts, _ = x_sample.shape
    mp, ms = bp * tp, bs * ts
    seq_bounds = tuple(b * tp for b in range(bp)) + tuple(mp + b * ts for b in range(bs + 1))

    dk = d // 16
    dv = d // 8
    qk_w = GLA_HEADS * dk
    v_w = GLA_HEADS * dv
    q_w = N_DIL_GROUPS * ATTN_KV_W
    lr0 = 2 * qk_w + 2 * v_w
    a0 = lr0 + 2 * GLA_RANK
    g0 = a0 + q_w + 2 * ATTN_KV_W
    ff_tk = -(-d_ff // (4 * 256)) * 256

    x = (x_prompt.reshape(mp, d), x_sample.reshape(ms, d))
    xn = rmsnorm(x, ffn1_pre_g[0])

    m_blocks = (mp + ms) // 1024
    up_steps = m_blocks * pl.cdiv(d_ff, 512)
    down_steps = m_blocks * (d // 1024) * pl.cdiv(d_ff, ff_tk)
    mixer_in_steps = m_blocks * ((lr0 + 2 * d) // 1024)

    def ffn(x, xn, l, wg_bf16, wu_bf16, wd, g_post, g_next, out_rows=None, down_rides=()):
        hmid, wd_bf16 = ffn_up(xn, wg_bf16, wu_bf16, rides=(Ride(wd, l, up_steps),))
        y, *later = ffn_down(hmid, wd_bf16, ff_tk, rides=down_rides)
        x, xn = norm_residual(y, x, g_post, g_next, 0.5, out_rows=out_rows)
        return x, xn, later

    wg1, wu1 = cast_layer(ffn1_w_gate, 0), cast_layer(ffn1_w_up, 0)
    for l in range(depth):
        x, xn, _ = ffn(x, xn, l, wg1, wu1, ffn1_w_down, ffn1_post_g[l], mix_pre_g[l])

        w_main = jnp.concatenate([w_in[l][:, :lr0], w_in[l][:, g0:]], axis=1).astype(BF16)
        w_att = w_in[l][:, a0:g0].astype(BF16)
        w_lr = jnp.concatenate([_pad_to(w_in[l][:, lr0:lr0 + GLA_RANK], 1, LANES),
                                _pad_to(w_in[l][:, lr0 + GLA_RANK:a0], 1, LANES)],
                               axis=1).astype(BF16)
        z, wg2, wu2 = matmul(xn, w_main, BF16, name="mixer_in",
                             rides=(Ride(ffn2_w_gate, l, mixer_in_steps),
                                    Ride(ffn2_w_up, l, mixer_in_steps)))
        lr = matmul(xn, w_lr, F32, name="mixer_in_lr")
        o_f = gla_scan(z, lr, _pad_to(gla_decay_up_fwd[l], 0, LANES),
                       gla_decay_bias_fwd[l].reshape(1, qk_w), seq_bounds, dk, dv, False)
        o_b = gla_scan(z, lr, _pad_to(gla_decay_up_bwd[l], 0, LANES),
                       gla_decay_bias_bwd[l].reshape(1, qk_w), seq_bounds, dk, dv, True)

        qkv = attn_in(xn, w_att)
        outs, lses = [], []
        for g, (qb, n_res) in enumerate(((8, 1), (4, 4), (1, 4))):
            o_g, l_g = dilated_attention_group(qkv[g], g, seq_bounds, qb, n_res)
            outs.append(o_g)
            lses.append(l_g)

        o_gla, o_att = mixer_post(o_f, o_b, z, (2 * qk_w + v_w) // v_w, gla_norm_g[l], outs, lses, dv)
        merged = merge_proj(o_gla, o_att, z, lr0, cast_layer(proj_gla, l, tr=256),
                            cast_layer(proj_attn, l, tr=256))
        y = matmul(merged, cast_layer(w_out, l, tr=256), BF16, name="mixer_out")
        x, xn = norm_residual(y, x, mix_post_g[l], ffn2_pre_g[l], 1.0)

        if l + 1 == depth:
            x, xn, _ = ffn(x, xn, l, wg2, wu2, ffn2_w_down, ffn2_post_g[l], None,
                           out_rows=(mp, ms))
        else:
            x, xn, (wg1, wu1) = ffn(x, xn, l, wg2, wu2, ffn2_w_down, ffn2_post_g[l],
                                    ffn1_pre_g[l + 1],
                                    down_rides=(Ride(ffn1_w_gate, l + 1, down_steps),
                                                Ride(ffn1_w_up, l + 1, down_steps)))

    return (x[0].reshape(bp, tp, d), x[1].reshape(bs, ts, d))
```

```python
import functools

import jax
import jax.numpy as jnp
from jax import lax
from jax.experimental import pallas as pl
from jax.experimental.pallas import tpu as pltpu

F32 = jnp.float32
BF16 = jnp.bfloat16

NORM_EPS = 1e-6
NEG_INF = -1e30

GLA_HEADS = 4
GLA_RANK = 16
GLA_TAU = 16.0
GLA_CHUNK = 64
ATTN_KV_HEADS = 8
ATTN_HEAD_DIM = 128
ATTN_KV_W = ATTN_KV_HEADS * ATTN_HEAD_DIM
DIL_WINDOWS = (128, 512, 2048)
DIL_RATES = (1, 4, 16)
N_DIL_GROUPS = 3
ALIBI_MAX_BIAS = 8.0

LANES = 128
VMEM_LIMIT_CAP = 60 * 1024 * 1024


def _cparams(dims, vmem_bytes):
    return pltpu.CompilerParams(dimension_semantics=dims,
                                vmem_limit_bytes=int(min(vmem_bytes * 5 // 4, VMEM_LIMIT_CAP)))


def _any_equal(idx, values):
    return functools.reduce(jnp.logical_or, [idx == v for v in values])


def _part_specs(n_first, tr, d):
    return [pl.BlockSpec((tr, d), lambda i: (jnp.minimum(i, n_first - 1), 0)),
            pl.BlockSpec((tr, d), lambda i: (jnp.maximum(i - n_first, 0), 0))]


def _rmsnorm_kernel(xa_ref, xb_ref, g_ref, o_ref, *, n_first):
    x = jnp.where(pl.program_id(0) < n_first, xa_ref[...], xb_ref[...])
    ms = jnp.mean(x * x, axis=-1, keepdims=True)
    o_ref[...] = (x * lax.rsqrt(ms + NORM_EPS) * g_ref[...]).astype(o_ref.dtype)


def rmsnorm(x_parts, g, tr=256):
    d = x_parts[0].shape[1]
    n_first = x_parts[0].shape[0] // tr
    m = x_parts[0].shape[0] + x_parts[1].shape[0]
    return pl.pallas_call(
        functools.partial(_rmsnorm_kernel, n_first=n_first),
        grid=(m // tr,),
        in_specs=_part_specs(n_first, tr, d) + [pl.BlockSpec((1, d), lambda i: (0, 0))],
        out_specs=pl.BlockSpec((tr, d), lambda i: (i, 0)),
        out_shape=jax.ShapeDtypeStruct((m, d), BF16),
        compiler_params=_cparams(("arbitrary",), 10 * tr * d * 4),
        name="rmsnorm",
    )(*x_parts, g.reshape(1, d))


def _norm_residual_kernel(y_ref, *refs, scale, with_next, n_in_first, n_out_first):
    i = pl.program_id(0)
    refs = list(refs)
    if n_in_first is None:
        x = refs.pop(0)[...]
    else:
        xa_ref, xb_ref = refs.pop(0), refs.pop(0)
        x = jnp.where(i < n_in_first, xa_ref[...], xb_ref[...])
    gp_ref = refs.pop(0)
    y = y_ref[...].astype(F32)
    ms = jnp.mean(y * y, axis=-1, keepdims=True)
    xo = x + scale * (y * lax.rsqrt(ms + NORM_EPS) * gp_ref[...])
    if with_next:
        gn_ref, xn_ref = refs.pop(0), refs.pop()
        ms2 = jnp.mean(xo * xo, axis=-1, keepdims=True)
        xn_ref[...] = (xo * lax.rsqrt(ms2 + NORM_EPS) * gn_ref[...]).astype(xn_ref.dtype)
    if n_out_first is None:
        refs[0][...] = xo
    else:
        @pl.when(i < n_out_first)
        def _():
            refs[0][...] = xo

        @pl.when(i >= n_out_first)
        def _():
            refs[1][...] = xo


def norm_residual(y, x, g_post, g_next, scale, out_rows=None, tr=256):
    m, d = y.shape
    with_next = g_next is not None
    row = pl.BlockSpec((tr, d), lambda i: (i, 0))
    vec = pl.BlockSpec((1, d), lambda i: (0, 0))
    gains = [g_post.reshape(1, d)] + ([g_next.reshape(1, d)] if with_next else [])
    x_parts = x if isinstance(x, (tuple, list)) else None
    n_in_first = None if x_parts is None else x_parts[0].shape[0] // tr
    n_out_first = None if out_rows is None else out_rows[0] // tr
    x_specs = [row] if x_parts is None else _part_specs(n_in_first, tr, d)
    if out_rows is None:
        xo_specs, xo_shapes = [row], [jax.ShapeDtypeStruct((m, d), F32)]
    else:
        xo_specs = _part_specs(n_out_first, tr, d)
        xo_shapes = [jax.ShapeDtypeStruct((rows, d), F32) for rows in out_rows]
    out = pl.pallas_call(
        functools.partial(_norm_residual_kernel, scale=scale, with_next=with_next,
                          n_in_first=n_in_first, n_out_first=n_out_first),
        grid=(m // tr,),
        in_specs=[row] + x_specs + [vec] * len(gains),
        out_specs=xo_specs + ([row] if with_next else []),
        out_shape=xo_shapes + ([jax.ShapeDtypeStruct((m, d), BF16)] if with_next else []),
        compiler_params=_cparams(("arbitrary",), 14 * tr * d * 4),
        name="norm_residual",
    )(y, *(x_parts if x_parts is not None else [x]), *gains)
    xo = out[0] if out_rows is None else tuple(out[:2])
    return xo, (out[-1] if with_next else None)


class Ride:
    def __init__(self, w_stack, layer, max_blocks, min_tr=16):
        _, self.rows, self.cols = w_stack.shape
        tr = min_tr
        while self.rows % tr or self.rows // tr > max_blocks:
            tr *= 2
            assert tr <= self.rows
        self.w_stack, self.layer, self.tr = w_stack, layer, tr
        self.n_blocks = self.rows // tr

    def specs(self, step_of):
        blk = lambda *g: jnp.minimum(step_of(*g), self.n_blocks - 1)
        src = pl.BlockSpec((None, self.tr, self.cols), lambda *g: (self.layer, blk(*g), 0))
        dst = pl.BlockSpec((self.tr, self.cols), lambda *g: (blk(*g), 0))
        return src, dst

    @property
    def out_shape(self):
        return jax.ShapeDtypeStruct((self.rows, self.cols), BF16)

    @property
    def vmem_bytes(self):
        return 12 * self.tr * self.cols


def _split_rides(refs, n_base_in, n_base_out, n_rides):
    a = n_base_in
    b = a + n_rides
    c = b + n_base_out
    d = c + n_rides
    return refs[:a], refs[a:b], refs[b:c], refs[c:d], refs[d:]


def _do_rides(step, rides_n_blocks, srcs, dsts):
    for n_blocks, src, dst in zip(rides_n_blocks, srcs, dsts):
        @pl.when(step < n_blocks)
        def _(src=src, dst=dst):
            dst[...] = src[...].astype(dst.dtype)


def _mm_kernel(*refs, ride_blocks):
    (a_ref, w_ref), srcs, (o_ref,), dsts, _ = _split_rides(refs, 2, 1, len(ride_blocks))
    o_ref[...] = jnp.dot(a_ref[...], w_ref[...],
                         preferred_element_type=F32).astype(o_ref.dtype)
    _do_rides(pl.program_id(0) * pl.num_programs(1) + pl.program_id(1), ride_blocks, srcs, dsts)


def matmul(a, w, out_dtype, tm=1024, tn=1024, w_cols=None, rides=(), name="matmul"):
    m, kd = a.shape
    n, col_of = w_cols if w_cols is not None else (w.shape[1], lambda j: j)
    tn = min(tn, n)
    nj = n // tn
    osz = jnp.dtype(out_dtype).itemsize
    assert all(r.n_blocks <= (m // tm) * nj for r in rides)
    ride_specs = [r.specs(lambda i, j: i * nj + j) for r in rides]
    out = pl.pallas_call(
        functools.partial(_mm_kernel, ride_blocks=tuple(r.n_blocks for r in rides)),
        grid=(m // tm, nj),
        in_specs=[pl.BlockSpec((tm, kd), lambda i, j: (i, 0)),
                  pl.BlockSpec((kd, tn), lambda i, j: (0, col_of(j)))] + [s for s, _ in ride_specs],
        out_specs=[pl.BlockSpec((tm, tn), lambda i, j: (i, j))] + [d for _, d in ride_specs],
        out_shape=[jax.ShapeDtypeStruct((m, n), out_dtype)] + [r.out_shape for r in rides],
        compiler_params=_cparams(("arbitrary", "arbitrary"),
                                 4 * (tm * kd + kd * tn) + 2 * tm * tn * osz + 3 * tm * tn * 4
                                 + sum(r.vmem_bytes for r in rides)),
        name=name,
    )(a, w, *[r.w_stack for r in rides])
    return out[0] if not rides else out


def _ffn_up_kernel(*refs, n_tail, ride_blocks):
    (x_ref, wg_ref, wu_ref), srcs, (o_ref,), dsts, _ = _split_rides(refs, 3, 1, len(ride_blocks))
    j = pl.program_id(1)
    tn = o_ref.shape[1]
    _do_rides(pl.program_id(0) * pl.num_programs(1) + j, ride_blocks, srcs, dsts)

    def tile(width):
        x = x_ref[...]
        g = jnp.dot(x, wg_ref[:, :width], preferred_element_type=F32)
        u = jnp.dot(x, wu_ref[:, :width], preferred_element_type=F32)
        o_ref[:, :width] = (g * jax.nn.sigmoid(g) * u).astype(o_ref.dtype)

    if n_tail == tn:
        tile(tn)
    else:
        pl.when(j == 0)(lambda: tile(n_tail))
        pl.when(j > 0)(lambda: tile(tn))


def ffn_up(xn, wg, wu, tm=1024, tn=512, rides=()):
    m, d = xn.shape
    f = wg.shape[1]
    nj = pl.cdiv(f, tn)
    col = lambda j: (j + nj - 1) % nj
    assert all(r.n_blocks <= (m // tm) * nj for r in rides)
    ride_specs = [r.specs(lambda i, j: i * nj + j) for r in rides]
    out = pl.pallas_call(
        functools.partial(_ffn_up_kernel, n_tail=f - (nj - 1) * tn,
                          ride_blocks=tuple(r.n_blocks for r in rides)),
        grid=(m // tm, nj),
        in_specs=[pl.BlockSpec((tm, d), lambda i, j: (i, 0)),
                  pl.BlockSpec((d, tn), lambda i, j: (0, col(j))),
                  pl.BlockSpec((d, tn), lambda i, j: (0, col(j)))] + [s for s, _ in ride_specs],
        out_specs=[pl.BlockSpec((tm, tn), lambda i, j: (i, col(j)))] + [d for _, d in ride_specs],
        out_shape=[jax.ShapeDtypeStruct((m, f), BF16)] + [r.out_shape for r in rides],
        compiler_params=_cparams(("arbitrary", "arbitrary"),
                                 4 * tm * d + 8 * d * tn + 4 * tm * tn + 16 * tm * tn
                                 + sum(r.vmem_bytes for r in rides)),
        name="ffn_up",
    )(xn, wg, wu, *[r.w_stack for r in rides])
    return out


def _ffn_down_kernel(*refs, k_tail, ride_blocks):
    (a_ref, w_ref), srcs, (o_ref,), dsts, (acc_ref,) = _split_rides(refs, 2, 1, len(ride_blocks))
    k = pl.program_id(2)
    last = pl.num_programs(2) - 1
    tk = a_ref.shape[1]
    step = (pl.program_id(0) * pl.num_programs(1) + pl.program_id(1)) * pl.num_programs(2) + k
    _do_rides(step, ride_blocks, srcs, dsts)

    @pl.when(k == 0)
    def _():
        acc_ref[...] = jnp.zeros_like(acc_ref)

    @pl.when(k < last)
    def _():
        acc_ref[...] += jnp.dot(a_ref[...], w_ref[...], preferred_element_type=F32)

    @pl.when(k == last)
    def _():
        if k_tail == tk:
            part = jnp.dot(a_ref[...], w_ref[...], preferred_element_type=F32)
        else:
            part = jnp.dot(a_ref[:, :k_tail], w_ref[:k_tail, :], preferred_element_type=F32)
        o_ref[...] = (acc_ref[...] + part).astype(o_ref.dtype)


def ffn_down(h, wd, tk, tm=1024, tn=1024, rides=()):
    m, kd = h.shape
    n = wd.shape[1]
    nj = n // tn
    nk = pl.cdiv(kd, tk)
    assert all(r.n_blocks <= (m // tm) * nj * nk for r in rides)
    ride_specs = [r.specs(lambda i, j, k: (i * nj + j) * nk + k) for r in rides]
    out = pl.pallas_call(
        functools.partial(_ffn_down_kernel, k_tail=kd - (nk - 1) * tk,
                          ride_blocks=tuple(r.n_blocks for r in rides)),
        grid=(m // tm, nj, nk),
        in_specs=[pl.BlockSpec((tm, tk), lambda i, j, k: (i, k)),
                  pl.BlockSpec((tk, tn), lambda i, j, k: (k, j))] + [s for s, _ in ride_specs],
        out_specs=[pl.BlockSpec((tm, tn), lambda i, j, k: (i, j))] + [d for _, d in ride_specs],
        out_shape=[jax.ShapeDtypeStruct((m, n), BF16)] + [r.out_shape for r in rides],
        scratch_shapes=[pltpu.VMEM((tm, tn), F32)],
        compiler_params=_cparams(("arbitrary", "arbitrary", "arbitrary"),
                                 4 * (tm * tk + tk * tn) + 4 * tm * tn + 16 * tm * tn
                                 + sum(r.vmem_bytes for r in rides)),
        name="ffn_down",
    )(h, wd, *[r.w_stack for r in rides])
    return out


def _gla_kernel(q_ref, k_ref, v_ref, lr_ref, up_ref, bias_ref, o_ref, st_ref, *,
                backward, n_chunks, reset_blocks, dk):
    i = pl.program_id(1)
    blk = pl.num_programs(1) - 1 - i if backward else i

    @pl.when(_any_equal(blk, reset_blocks))
    def _():
        st_ref[...] = jnp.zeros_like(st_ref)

    c = GLA_CHUNK
    nc = n_chunks
    row = lax.broadcasted_iota(jnp.int32, (c, c), 0)
    col = lax.broadcasted_iota(jnp.int32, (c, c), 1)
    if backward:
        tri, mask = row <= col, row < col
    else:
        tri, mask = row >= col, row >= col
    tri = jnp.broadcast_to(tri.astype(BF16)[None], (nc, c, c))

    def bmm(a, b, ca, cb):
        return lax.dot_general(a, b, (((ca,), (cb,)), ((0,), (0,))), preferred_element_type=F32)

    x = jnp.dot(lr_ref[...].astype(BF16), up_ref[...].astype(BF16),
                preferred_element_type=F32) + bias_ref[...]
    la = (jnp.minimum(x, 0.0) - jnp.log1p(jnp.exp(-jnp.abs(x)))) * (1.0 / GLA_TAU)
    la = la.reshape(nc, c, dk)
    la_hi = la.astype(BF16)
    la_lo = (la - la_hi.astype(F32)).astype(BF16)
    b = bmm(tri, la_hi, 2, 1) + bmm(tri, la_lo, 2, 1)
    tot = jnp.sum(la, axis=1, keepdims=True)
    q = q_ref[...].astype(F32).reshape(nc, c, dk)
    k = k_ref[...].astype(F32).reshape(nc, c, dk)
    v = v_ref[...].reshape(nc, c, v_ref.shape[1])
    qd = (q * (dk ** -0.5) * jnp.exp(b)).astype(BF16)
    kd = (k * jnp.exp(-b)).astype(BF16)
    ks = (k * jnp.exp(tot - b)).astype(BF16)
    scores = jnp.where(mask[None], bmm(qd, kd, 2, 2), 0.0).astype(BF16)
    o_intra = bmm(scores, v, 2, 1)
    st_inc = bmm(v, ks, 1, 1)
    chunk_decay = jnp.exp(tot)

    st = st_ref[...]
    nt = (((1,), (1,)), ((), ()))
    for ci in (reversed(range(nc)) if backward else range(nc)):
        o_ref[ci * c:(ci + 1) * c, :] = o_intra[ci] + lax.dot_general(
            qd[ci], st.astype(BF16), nt, preferred_element_type=F32)
        st = st * chunk_decay[ci] + st_inc[ci]
    st_ref[...] = st


def gla_scan(z, lr, up, bias, seq_bounds, dk, dv, backward, tb=512):
    m = z.shape[0]
    h = GLA_HEADS
    nblk = m // tb
    if backward:
        resets = tuple(s // tb - 1 for s in seq_bounds[1:])
        tok = lambda i: nblk - 1 - i
    else:
        resets = tuple(s // tb for s in seq_bounds[:-1])
        tok = lambda i: i
    k_off = (h * dk) // dk
    v_off = (2 * h * dk) // dv
    lr_col = 1 if backward else 0
    kern = functools.partial(_gla_kernel, backward=backward, n_chunks=tb // GLA_CHUNK,
                             reset_blocks=resets, dk=dk)
    return pl.pallas_call(
        kern,
        grid=(h, nblk),
        in_specs=[pl.BlockSpec((tb, dk), lambda hh, i: (tok(i), hh)),
                  pl.BlockSpec((tb, dk), lambda hh, i: (tok(i), k_off + hh)),
                  pl.BlockSpec((tb, dv), lambda hh, i: (tok(i), v_off + hh)),
                  pl.BlockSpec((tb, LANES), lambda hh, i: (tok(i), lr_col)),
                  pl.BlockSpec((LANES, dk), lambda hh, i: (0, hh)),
                  pl.BlockSpec((1, dk), lambda hh, i: (0, hh))],
        out_specs=pl.BlockSpec((tb, dv), lambda hh, i: (tok(i), hh)),
        out_shape=jax.ShapeDtypeStruct((m, h * dv), F32),
        scratch_shapes=[pltpu.VMEM((dv, dk), F32)],
        compiler_params=_cparams(("arbitrary", "arbitrary"), 40 * 1024 * 1024),
        name="gla_scan_bwd" if backward else "gla_scan_fwd",
    )(z, z, z, lr, up, bias)


def _attn_in_kernel(x_ref, w_ref, *refs):
    out_refs, acc_ref = refs[:-1], refs[-1]
    j = pl.program_id(1)
    n_chunks, tm, _ = acc_ref.shape
    res = jnp.dot(x_ref[...], w_ref[...], preferred_element_type=F32)
    for c in range(n_chunks):
        acc_ref[c] = res[:, c * LANES:(c + 1) * LANES]
    for g, (o_ref, d) in enumerate(zip(out_refs, DIL_RATES)):
        @pl.when(jnp.logical_or(j == g, j >= N_DIL_GROUPS))
        def _(o_ref=o_ref, d=d):
            for r in range(d):
                for c in range(n_chunks):
                    rows = pl.ds(r, tm // d, stride=d) if d > 1 else slice(None)
                    o_ref[r, :, c * LANES:(c + 1) * LANES] = (
                        acc_ref[c, rows, :].astype(o_ref.dtype))


def attn_in(xn, w_att, col0, tm=512):
    m, kd = xn.shape
    w = ATTN_KV_W
    nj = N_DIL_GROUPS + 2
    blk0 = col0 // w
    assert blk0 * w == col0
    sect = lambda i, j: (jnp.clip(j - (N_DIL_GROUPS - 1), 0, 2), 0, i, 0)
    return pl.pallas_call(
        _attn_in_kernel,
        grid=(m // tm, nj),
        in_specs=[pl.BlockSpec((tm, kd), lambda i, j: (i, 0)),
                  pl.BlockSpec((kd, w), lambda i, j: (0, blk0 + j))],
        out_specs=[pl.BlockSpec((None, d, tm // d, w), sect) for d in DIL_RATES],
        out_shape=[jax.ShapeDtypeStruct((3, d, m // d, w), BF16) for d in DIL_RATES],
        scratch_shapes=[pltpu.VMEM((w // LANES, tm, LANES), F32)],
        compiler_params=_cparams(("parallel", "arbitrary"),
                                 4 * (tm * kd + kd * w) + 12 * tm * w + 8 * tm * w),
        name="attn_in",
    )(xn, w_att)


def _dil_kernel(q_ref, kp_ref, ko_ref, kn_ref, vp_ref, vo_ref, vn_ref, o_ref, l_ref, *,
                dilation, group, qb, start_tiles, end_tiles):
    t = pl.program_id(0)
    n_res = q_ref.shape[0]
    res0 = pl.program_id(1) * n_res
    has_prev = jnp.logical_not(_any_equal(t, start_tiles))
    has_next = jnp.logical_not(_any_equal(t + 1, end_tiles))
    r = DIL_WINDOWS[group] // (2 * dilation)
    e = ATTN_HEAD_DIM
    qi = lax.broadcasted_iota(jnp.int32, (r, 3 * r), 0)
    kj = lax.broadcasted_iota(jnp.int32, (r, 3 * r), 1)
    rel = jnp.abs(kj - r - qi)
    in_window = rel <= r
    rel_f = rel.astype(F32) * float(dilation)
    prev_ok = jnp.logical_or(kj >= r, has_prev)
    next_ok = jnp.logical_or(kj < 2 * r, has_next)

    def valid(j):
        v = in_window
        if j == 0:
            v = jnp.logical_and(v, prev_ok)
        if j == qb - 1:
            v = jnp.logical_and(v, next_ok)
        return v

    valids = {j: valid(j) for j in {0, qb - 1, min(1, qb - 1)}}
    n_heads = N_DIL_GROUPS * ATTN_KV_HEADS
    scale = e ** -0.5
    nh = ATTN_KV_HEADS

    def heads(ref, ri, rows):
        return jnp.stack([ref[ri, rows, h * e:(h + 1) * e] for h in range(nh)])

    slopes = [2.0 ** (-ALIBI_MAX_BIAS * (group * nh + h + 1) / n_heads) for h in range(nh)]
    bias = jnp.stack([slope * rel_f for slope in slopes])
    every = slice(None)
    qk_dims = (((2,), (2,)), ((0,), (0,)))
    pv_dims = (((2,), (1,)), ((0,), (0,)))
    for ri in range(n_res):
        k_all = jnp.concatenate([heads(kp_ref, ri, every), heads(ko_ref, ri, every),
                                 heads(kn_ref, ri, every)], axis=1)
        v_all = jnp.concatenate([heads(vp_ref, ri, every), heads(vo_ref, ri, every),
                                 heads(vn_ref, ri, every)], axis=1)
        for j in range(qb):
            q = heads(q_ref, ri, slice(j * r, (j + 1) * r))
            s = lax.dot_general(q, k_all[:, j * r:(j + 3) * r], qk_dims,
                                preferred_element_type=F32)
            s = jnp.where(valids.get(j, in_window)[None], s * scale - bias, NEG_INF)
            mx = jnp.max(s, axis=-1, keepdims=True)
            p = jnp.exp(s - mx)
            den = jnp.sum(p, axis=-1, keepdims=True)
            o = lax.dot_general(p.astype(BF16), v_all[:, j * r:(j + 3) * r], pv_dims,
                                preferred_element_type=F32) / den
            lse = jnp.broadcast_to(mx + jnp.log(den), (nh, r, e))
            if dilation == 1:
                dst = pl.ds(j * r, r)
            else:
                dst = pl.ds(j * r * dilation + res0 + ri, r, stride=dilation)
            for h in range(nh):
                o_ref[h, dst, :] = o[h]
                l_ref[h, dst, :] = lse[h]


def dilated_attention_group(a, group, seq_bounds, qb, n_res):
    d = DIL_RATES[group]
    r = DIL_WINDOWS[group] // (2 * d)
    w = ATTN_KV_W
    rows = a.shape[2]
    m = rows * d
    tile = qb * r * d
    n_tiles = m // tile
    n_blocks = rows // r
    starts = tuple(s // tile for s in seq_bounds[:-1])
    ends = tuple(s // tile for s in seq_bounds[1:])

    own = lambda sec: pl.BlockSpec((None, n_res, qb * r, w), lambda t, rr: (sec, rr, t, 0))
    prev = lambda sec: pl.BlockSpec(
        (None, n_res, r, w), lambda t, rr: (sec, rr, jnp.maximum(t * qb - 1, 0), 0))
    nxt = lambda sec: pl.BlockSpec(
        (None, n_res, r, w), lambda t, rr: (sec, rr, jnp.minimum((t + 1) * qb, n_blocks - 1), 0))
    hm = (ATTN_KV_HEADS, m, ATTN_HEAD_DIM)
    out = pl.BlockSpec((ATTN_KV_HEADS, tile, ATTN_HEAD_DIM), lambda t, rr: (0, t, 0))
    kern = functools.partial(_dil_kernel, dilation=d, group=group, qb=qb,
                             start_tiles=starts, end_tiles=ends)
    return pl.pallas_call(
        kern,
        grid=(n_tiles, d // n_res),
        in_specs=[own(0), prev(1), own(1), nxt(1), prev(2), own(2), nxt(2)],
        out_specs=[out, out],
        out_shape=[jax.ShapeDtypeStruct(hm, F32), jax.ShapeDtypeStruct(hm, F32)],
        compiler_params=_cparams(("parallel", "arbitrary"),
                                 16 * tile * w + n_res * (12 * qb + 16) * r * w + 16 * r * w * 4),
        name=f"dilated_attn_g{group}",
    )(a, a, a, a, a, a, a)


def _mixer_post_kernel(of_ref, ob_ref, r_ref, g_ref, o0_ref, o1_ref, o2_ref,
                       l0_ref, l1_ref, l2_ref, ogla_ref, oatt_ref, *, dv):
    o = of_ref[...] + ob_ref[...]
    r = r_ref[...].astype(F32)
    gate = r * jax.nn.sigmoid(r) * g_ref[...]
    for h in range(GLA_HEADS):
        sl = slice(h * dv, (h + 1) * dv)
        oh = o[:, sl]
        ms = jnp.mean(oh * oh, axis=-1, keepdims=True)
        ogla_ref[:, sl] = (oh * lax.rsqrt(ms + NORM_EPS) * gate[:, sl]).astype(ogla_ref.dtype)
    e = ATTN_HEAD_DIM
    for h in range(ATTN_KV_HEADS):
        l0, l1, l2 = l0_ref[h], l1_ref[h], l2_ref[h]
        mx = jnp.maximum(jnp.maximum(l0, l1), l2)
        w0, w1, w2 = jnp.exp(l0 - mx), jnp.exp(l1 - mx), jnp.exp(l2 - mx)
        num = w0 * o0_ref[h] + w1 * o1_ref[h] + w2 * o2_ref[h]
        oatt_ref[:, h * e:(h + 1) * e] = (num / (w0 + w1 + w2)).astype(oatt_ref.dtype)


def mixer_post(o_f, o_b, z, r_col, gla_norm_g, outs, lses, dv, tr=256):
    m = z.shape[0]
    vw = GLA_HEADS * dv
    kvw = ATTN_KV_W
    row = lambda w: pl.BlockSpec((tr, w), lambda i: (i, 0))
    heads = pl.BlockSpec((ATTN_KV_HEADS, tr, ATTN_HEAD_DIM), lambda i: (0, i, 0))
    return pl.pallas_call(
        functools.partial(_mixer_post_kernel, dv=dv),
        grid=(m // tr,),
        in_specs=[row(vw), row(vw),
                  pl.BlockSpec((tr, vw), lambda i: (i, r_col)),
                  pl.BlockSpec((1, vw), lambda i: (0, 0))] + [heads] * 6,
        out_specs=[row(vw), row(kvw)],
        out_shape=[jax.ShapeDtypeStruct((m, vw), BF16),
                   jax.ShapeDtypeStruct((m, kvw), BF16)],
        compiler_params=_cparams(("parallel",), 36 * 1024 * 1024),
        name="mixer_post",
    )(o_f, o_b, z, gla_norm_g.reshape(1, vw), *outs, *lses)


def _merge_proj_kernel(og_ref, oa_ref, gg_ref, ga_ref, pg_ref, pa_ref, o_ref):
    yg = jnp.dot(og_ref[...], pg_ref[...], preferred_element_type=F32)
    ya = jnp.dot(oa_ref[...], pa_ref[...], preferred_element_type=F32)
    gg = jax.nn.sigmoid(gg_ref[...].astype(F32))
    ga = jax.nn.sigmoid(ga_ref[...].astype(F32))
    o_ref[...] = (gg * yg + ga * ya).astype(o_ref.dtype)


def merge_proj(o_gla, o_att, z, gate_start, proj_gla, proj_attn, tm=1024, tn=512):
    m, vw = o_gla.shape
    kvw = o_att.shape[1]
    d = proj_gla.shape[1]
    nj = d // tn
    gate_col = gate_start // tn
    return pl.pallas_call(
        _merge_proj_kernel,
        grid=(m // tm, nj),
        in_specs=[pl.BlockSpec((tm, vw), lambda i, j: (i, 0)),
                  pl.BlockSpec((tm, kvw), lambda i, j: (i, 0)),
                  pl.BlockSpec((tm, tn), lambda i, j: (i, gate_col + j)),
                  pl.BlockSpec((tm, tn), lambda i, j: (i, gate_col + nj + j)),
                  pl.BlockSpec((vw, tn), lambda i, j: (0, j)),
                  pl.BlockSpec((kvw, tn), lambda i, j: (0, j))],
        out_specs=pl.BlockSpec((tm, tn), lambda i, j: (i, j)),
        out_shape=jax.ShapeDtypeStruct((m, d), BF16),
        compiler_params=_cparams(("parallel", "arbitrary"), 40 * 1024 * 1024),
        name="merge_proj",
    )(o_gla, o_att, z, z, proj_gla, proj_attn)


def _cast_kernel(w_ref, o_ref):
    o_ref[...] = w_ref[...].astype(o_ref.dtype)


def cast_layer(w_stack, layer, tr=128):
    _, rows, cols = w_stack.shape
    return pl.pallas_call(
        _cast_kernel,
        grid=(rows // tr,),
        in_specs=[pl.BlockSpec((None, tr, cols), lambda i: (layer, i, 0))],
        out_specs=pl.BlockSpec((tr, cols), lambda i: (i, 0)),
        out_shape=jax.ShapeDtypeStruct((rows, cols), BF16),
        compiler_params=_cparams(("parallel",), 16 * tr * cols),
        name="cast_layer",
    )(w_stack)


def _pad_to(x, axis, size):
    pad = [(0, 0)] * x.ndim
    pad[axis] = (0, size - x.shape[axis])
    return jnp.pad(x, pad)


def kernel(x_prompt, x_sample, ffn1_pre_g, ffn1_w_gate, ffn1_w_up, ffn1_w_down, ffn1_post_g,
           mix_pre_g, w_in, gla_decay_up_fwd, gla_decay_bias_fwd, gla_decay_up_bwd,
           gla_decay_bias_bwd, gla_norm_g, proj_gla, proj_attn, w_out, mix_post_g,
           ffn2_pre_g, ffn2_w_gate, ffn2_w_up, ffn2_w_down, ffn2_post_g):
    depth, d, d_ff = ffn1_w_gate.shape
    bp, tp, _ = x_prompt.shape
    bs, ts, _ = x_sample.shape
    mp, ms = bp * tp, bs * ts
    seq_bounds = tuple(b * tp for b in range(bp)) + tuple(mp + b * ts for b in range(bs + 1))

    dk = d // 16
    dv = d // 8
    qk_w = GLA_HEADS * dk
    v_w = GLA_HEADS * dv
    q_w = N_DIL_GROUPS * ATTN_KV_W
    lr0 = 2 * qk_w + 2 * v_w
    a0 = lr0 + 2 * GLA_RANK
    g0 = a0 + q_w + 2 * ATTN_KV_W
    ff_tk = -(-d_ff // (4 * 256)) * 256

    x = (x_prompt.reshape(mp, d), x_sample.reshape(ms, d))
    xn = rmsnorm(x, ffn1_pre_g[0])

    m_blocks = (mp + ms) // 1024
    up_steps = m_blocks * pl.cdiv(d_ff, 512)
    down_steps = m_blocks * (d // 1024) * pl.cdiv(d_ff, ff_tk)
    mixer_in_steps = m_blocks * ((lr0 + 2 * d) // 1024)

    def ffn(x, xn, l, wg_bf16, wu_bf16, wd, g_post, g_next, out_rows=None, down_rides=()):
        hmid, wd_bf16 = ffn_up(xn, wg_bf16, wu_bf16, rides=(Ride(wd, l, up_steps),))
        y, *later = ffn_down(hmid, wd_bf16, ff_tk, rides=down_rides)
        x, xn = norm_residual(y, x, g_post, g_next, 0.5, out_rows=out_rows)
        return x, xn, later

    wg1, wu1 = cast_layer(ffn1_w_gate, 0), cast_layer(ffn1_w_up, 0)
    for l in range(depth):
        x, xn, _ = ffn(x, xn, l, wg1, wu1, ffn1_w_down, ffn1_post_g[l], mix_pre_g[l])

        w_all = jnp.concatenate(
            [w_in[l][:, :lr0], w_in[l][:, a0:],
             _pad_to(w_in[l][:, lr0:lr0 + GLA_RANK], 1, LANES),
             _pad_to(w_in[l][:, lr0 + GLA_RANK:a0], 1, LANES)], axis=1).astype(BF16)
        att_tiles = (g0 - a0) // 1024
        z, wg2, wu2, pg, pa, wo = matmul(
            xn, w_all, BF16, name="mixer_in",
            w_cols=(lr0 + 2 * d, lambda j: jnp.where(j < lr0 // 1024, j, j + att_tiles)),
            rides=(Ride(ffn2_w_gate, l, mixer_in_steps), Ride(ffn2_w_up, l, mixer_in_steps),
                   Ride(proj_gla, l, mixer_in_steps), Ride(proj_attn, l, mixer_in_steps),
                   Ride(w_out, l, mixer_in_steps)))
        lr_blk = (w_all.shape[1] - 2 * LANES) // (2 * LANES)
        lr = matmul(xn, w_all, F32, tn=2 * LANES, w_cols=(2 * LANES, lambda j: lr_blk),
                    name="mixer_in_lr")
        o_f = gla_scan(z, lr, _pad_to(gla_decay_up_fwd[l], 0, LANES),
                       gla_decay_bias_fwd[l].reshape(1, qk_w), seq_bounds, dk, dv, False)
        o_b = gla_scan(z, lr, _pad_to(gla_decay_up_bwd[l], 0, LANES),
                       gla_decay_bias_bwd[l].reshape(1, qk_w), seq_bounds, dk, dv, True)

        qkv = attn_in(xn, w_all, lr0)
        outs, lses = [], []
        for g, (qb, n_res) in enumerate(((8, 1), (4, 4), (1, 4))):
            o_g, l_g = dilated_attention_group(qkv[g], g, seq_bounds, qb, n_res)
            outs.append(o_g)
            lses.append(l_g)

        o_gla, o_att = mixer_post(o_f, o_b, z, (2 * qk_w + v_w) // v_w, gla_norm_g[l], outs, lses, dv)
        merged = merge_proj(o_gla, o_att, z, lr0, pg, pa)
        y = matmul(merged, wo, BF16, name="mixer_out")
        x, xn = norm_residual(y, x, mix_post_g[l], ffn2_pre_g[l], 1.0)

        if l + 1 == depth:
            x, xn, _ = ffn(x, xn, l, wg2, wu2, ffn2_w_down, ffn2_post_g[l], None,
                           out_rows=(mp, ms))
        else:
            x, xn, (wg1, wu1) = ffn(x, xn, l, wg2, wu2, ffn2_w_down, ffn2_post_g[l],
                                    ffn1_pre_g[l + 1],
                                    down_rides=(Ride(ffn1_w_gate, l + 1, down_steps),
                                                Ride(ffn1_w_up, l + 1, down_steps)))

    return (x[0].reshape(bp, tp, d), x[1].reshape(bs, ts, d))
```

```python
import functools

import jax
import jax.numpy as jnp
from jax import lax
from jax.experimental import pallas as pl
from jax.experimental.pallas import tpu as pltpu

F32 = jnp.float32
BF16 = jnp.bfloat16

NORM_EPS = 1e-6
NEG_INF = -1e30

GLA_HEADS = 4
GLA_RANK = 16
GLA_TAU = 16.0
GLA_CHUNK = 64
ATTN_KV_HEADS = 8
ATTN_HEAD_DIM = 128
ATTN_KV_W = ATTN_KV_HEADS * ATTN_HEAD_DIM
DIL_WINDOWS = (128, 512, 2048)
DIL_RATES = (1, 4, 16)
N_DIL_GROUPS = 3
ALIBI_MAX_BIAS = 8.0

LANES = 128
VMEM_LIMIT_CAP = 60 * 1024 * 1024


def _cparams(dims, vmem_bytes):
    return pltpu.CompilerParams(dimension_semantics=dims,
                                vmem_limit_bytes=int(min(vmem_bytes * 5 // 4, VMEM_LIMIT_CAP)))


def _any_equal(idx, values):
    return functools.reduce(jnp.logical_or, [idx == v for v in values])


def _part_specs(n_first, tr, d):
    return [pl.BlockSpec((tr, d), lambda i: (jnp.minimum(i, n_first - 1), 0)),
            pl.BlockSpec((tr, d), lambda i: (jnp.maximum(i - n_first, 0), 0))]


def _rmsnorm_kernel(xa_ref, xb_ref, g_ref, o_ref, *, n_first):
    x = jnp.where(pl.program_id(0) < n_first, xa_ref[...], xb_ref[...])
    ms = jnp.mean(x * x, axis=-1, keepdims=True)
    o_ref[...] = (x * lax.rsqrt(ms + NORM_EPS) * g_ref[...]).astype(o_ref.dtype)


def rmsnorm(x_parts, g, tr=256):
    d = x_parts[0].shape[1]
    n_first = x_parts[0].shape[0] // tr
    m = x_parts[0].shape[0] + x_parts[1].shape[0]
    return pl.pallas_call(
        functools.partial(_rmsnorm_kernel, n_first=n_first),
        grid=(m // tr,),
        in_specs=_part_specs(n_first, tr, d) + [pl.BlockSpec((1, d), lambda i: (0, 0))],
        out_specs=pl.BlockSpec((tr, d), lambda i: (i, 0)),
        out_shape=jax.ShapeDtypeStruct((m, d), BF16),
        compiler_params=_cparams(("arbitrary",), 10 * tr * d * 4),
        name="rmsnorm",
    )(*x_parts, g.reshape(1, d))


def _norm_residual_kernel(y_ref, *refs, scale, with_next, n_in_first, n_out_first):
    i = pl.program_id(0)
    refs = list(refs)
    if n_in_first is None:
        x = refs.pop(0)[...]
    else:
        xa_ref, xb_ref = refs.pop(0), refs.pop(0)
        x = jnp.where(i < n_in_first, xa_ref[...], xb_ref[...])
    gp_ref = refs.pop(0)
    y = y_ref[...].astype(F32)
    ms = jnp.mean(y * y, axis=-1, keepdims=True)
    xo = x + scale * (y * lax.rsqrt(ms + NORM_EPS) * gp_ref[...])
    if with_next:
        gn_ref, xn_ref = refs.pop(0), refs.pop()
        ms2 = jnp.mean(xo * xo, axis=-1, keepdims=True)
        xn_ref[...] = (xo * lax.rsqrt(ms2 + NORM_EPS) * gn_ref[...]).astype(xn_ref.dtype)
    if n_out_first is None:
        refs[0][...] = xo
    else:
        @pl.when(i < n_out_first)
        def _():
            refs[0][...] = xo

        @pl.when(i >= n_out_first)
        def _():
            refs[1][...] = xo


def norm_residual(y, x, g_post, g_next, scale, out_rows=None, tr=256):
    m, d = y.shape
    with_next = g_next is not None
    row = pl.BlockSpec((tr, d), lambda i: (i, 0))
    vec = pl.BlockSpec((1, d), lambda i: (0, 0))
    gains = [g_post.reshape(1, d)] + ([g_next.reshape(1, d)] if with_next else [])
    x_parts = x if isinstance(x, (tuple, list)) else None
    n_in_first = None if x_parts is None else x_parts[0].shape[0] // tr
    n_out_first = None if out_rows is None else out_rows[0] // tr
    x_specs = [row] if x_parts is None else _part_specs(n_in_first, tr, d)
    if out_rows is None:
        xo_specs, xo_shapes = [row], [jax.ShapeDtypeStruct((m, d), F32)]
    else:
        xo_specs = _part_specs(n_out_first, tr, d)
        xo_shapes = [jax.ShapeDtypeStruct((rows, d), F32) for rows in out_rows]
    out = pl.pallas_call(
        functools.partial(_norm_residual_kernel, scale=scale, with_next=with_next,
                          n_in_first=n_in_first, n_out_first=n_out_first),
        grid=(m // tr,),
        in_specs=[row] + x_specs + [vec] * len(gains),
        out_specs=xo_specs + ([row] if with_next else []),
        out_shape=xo_shapes + ([jax.ShapeDtypeStruct((m, d), BF16)] if with_next else []),
        compiler_params=_cparams(("arbitrary",), 14 * tr * d * 4),
        name="norm_residual",
    )(y, *(x_parts if x_parts is not None else [x]), *gains)
    xo = out[0] if out_rows is None else tuple(out[:2])
    return xo, (out[-1] if with_next else None)


class Ride:
    def __init__(self, w_stack, layer, max_blocks, min_tr=16):
        _, self.rows, self.cols = w_stack.shape
        tr = min_tr
        while self.rows % tr or self.rows // tr > max_blocks:
            tr *= 2
            assert tr <= self.rows
        self.w_stack, self.layer, self.tr = w_stack, layer, tr
        self.n_blocks = self.rows // tr

    def specs(self, step_of):
        blk = lambda *g: jnp.minimum(step_of(*g), self.n_blocks - 1)
        src = pl.BlockSpec((None, self.tr, self.cols), lambda *g: (self.layer, blk(*g), 0))
        dst = pl.BlockSpec((self.tr, self.cols), lambda *g: (blk(*g), 0))
        return src, dst

    @property
    def out_shape(self):
        return jax.ShapeDtypeStruct((self.rows, self.cols), BF16)

    @property
    def vmem_bytes(self):
        return 12 * self.tr * self.cols


def _split_rides(refs, n_base_in, n_base_out, n_rides):
    a = n_base_in
    b = a + n_rides
    c = b + n_base_out
    d = c + n_rides
    return refs[:a], refs[a:b], refs[b:c], refs[c:d], refs[d:]


def _do_rides(step, rides_n_blocks, srcs, dsts):
    for n_blocks, src, dst in zip(rides_n_blocks, srcs, dsts):
        @pl.when(step < n_blocks)
        def _(src=src, dst=dst):
            dst[...] = src[...].astype(dst.dtype)


def _mm_kernel(*refs, ride_blocks):
    (a_ref, w_ref), srcs, (o_ref,), dsts, _ = _split_rides(refs, 2, 1, len(ride_blocks))
    o_ref[...] = jnp.dot(a_ref[...], w_ref[...],
                         preferred_element_type=F32).astype(o_ref.dtype)
    _do_rides(pl.program_id(0) * pl.num_programs(1) + pl.program_id(1), ride_blocks, srcs, dsts)


def matmul(a, w, out_dtype, tm=1024, tn=1024, w_cols=None, rides=(), name="matmul"):
    m, kd = a.shape
    n, col_of = w_cols if w_cols is not None else (w.shape[1], lambda j: j)
    tn = min(tn, n)
    nj = n // tn
    osz = jnp.dtype(out_dtype).itemsize
    assert all(r.n_blocks <= (m // tm) * nj for r in rides)
    ride_specs = [r.specs(lambda i, j: i * nj + j) for r in rides]
    out = pl.pallas_call(
        functools.partial(_mm_kernel, ride_blocks=tuple(r.n_blocks for r in rides)),
        grid=(m // tm, nj),
        in_specs=[pl.BlockSpec((tm, kd), lambda i, j: (i, 0)),
                  pl.BlockSpec((kd, tn), lambda i, j: (0, col_of(j)))] + [s for s, _ in ride_specs],
        out_specs=[pl.BlockSpec((tm, tn), lambda i, j: (i, j))] + [d for _, d in ride_specs],
        out_shape=[jax.ShapeDtypeStruct((m, n), out_dtype)] + [r.out_shape for r in rides],
        compiler_params=_cparams(("arbitrary", "arbitrary"),
                                 4 * (tm * kd + kd * tn) + 2 * tm * tn * osz + 3 * tm * tn * 4
                                 + sum(r.vmem_bytes for r in rides)),
        name=name,
    )(a, w, *[r.w_stack for r in rides])
    return out[0] if not rides else out


def _ffn_up_kernel(*refs, n_tail, ride_blocks):
    (x_ref, wg_ref, wu_ref), srcs, (o_ref,), dsts, _ = _split_rides(refs, 3, 1, len(ride_blocks))
    j = pl.program_id(1)
    tn = o_ref.shape[1]
    _do_rides(pl.program_id(0) * pl.num_programs(1) + j, ride_blocks, srcs, dsts)

    def tile(width):
        x = x_ref[...]
        g = jnp.dot(x, wg_ref[:, :width], preferred_element_type=F32)
        u = jnp.dot(x, wu_ref[:, :width], preferred_element_type=F32)
        o_ref[:, :width] = (g * jax.nn.sigmoid(g) * u).astype(o_ref.dtype)

    if n_tail == tn:
        tile(tn)
    else:
        pl.when(j == 0)(lambda: tile(n_tail))
        pl.when(j > 0)(lambda: tile(tn))


def ffn_up(xn, wg, wu, tm=1024, tn=512, rides=()):
    m, d = xn.shape
    f = wg.shape[1]
    nj = pl.cdiv(f, tn)
    col = lambda j: (j + nj - 1) % nj
    assert all(r.n_blocks <= (m // tm) * nj for r in rides)
    ride_specs = [r.specs(lambda i, j: i * nj + j) for r in rides]
    out = pl.pallas_call(
        functools.partial(_ffn_up_kernel, n_tail=f - (nj - 1) * tn,
                          ride_blocks=tuple(r.n_blocks for r in rides)),
        grid=(m // tm, nj),
        in_specs=[pl.BlockSpec((tm, d), lambda i, j: (i, 0)),
                  pl.BlockSpec((d, tn), lambda i, j: (0, col(j))),
                  pl.BlockSpec((d, tn), lambda i, j: (0, col(j)))] + [s for s, _ in ride_specs],
        out_specs=[pl.BlockSpec((tm, tn), lambda i, j: (i, col(j)))] + [d for _, d in ride_specs],
        out_shape=[jax.ShapeDtypeStruct((m, f), BF16)] + [r.out_shape for r in rides],
        compiler_params=_cparams(("arbitrary", "arbitrary"),
                                 4 * tm * d + 8 * d * tn + 4 * tm * tn + 16 * tm * tn
                                 + sum(r.vmem_bytes for r in rides)),
        name="ffn_up",
    )(xn, wg, wu, *[r.w_stack for r in rides])
    return out


def _ffn_up_prenorm_kernel(*refs, n_tail, ride_blocks, scale, n_chunks, n_row_blocks):
    ((y_ref, x_ref, gp_ref, gn_ref, wg_ref, wu_ref), srcs, (o_ref, xo_ref), dsts,
     (xn_buf,)) = _split_rides(refs, 6, 2, len(ride_blocks))
    i = pl.program_id(0)
    j = pl.program_id(1)
    tn = o_ref.shape[1]
    rc = y_ref.shape[0]
    _do_rides(i * pl.num_programs(1) + j, ride_blocks, srcs, dsts)

    def prenorm():
        c = jnp.minimum(j, n_chunks - 1)
        y = y_ref[...].astype(F32)
        ms = jnp.mean(y * y, axis=-1, keepdims=True)
        xo = x_ref[...] + scale * (y * lax.rsqrt(ms + NORM_EPS) * gp_ref[...])
        xo_ref[...] = xo
        ms2 = jnp.mean(xo * xo, axis=-1, keepdims=True)
        xn_buf[i % 2, pl.ds(pl.multiple_of(c * rc, rc), rc), :] = (
            xo * lax.rsqrt(ms2 + NORM_EPS) * gn_ref[...]).astype(xn_buf.dtype)

    def tile(width):
        x = xn_buf[(i + 1) % 2]
        g = jnp.dot(x, wg_ref[:, :width], preferred_element_type=F32)
        u = jnp.dot(x, wu_ref[:, :width], preferred_element_type=F32)
        o_ref[:, :width] = (g * jax.nn.sigmoid(g) * u).astype(o_ref.dtype)

    @pl.when(i == 0)
    def _():
        o_ref[...] = jnp.zeros_like(o_ref)
        prenorm()

    @pl.when(jnp.logical_and(i > 0, j == 0))
    def _():
        tile(n_tail)
        prenorm()

    @pl.when(jnp.logical_and(i > 0, jnp.logical_and(j > 0, j < n_chunks)))
    def _():
        tile(tn)
        prenorm()

    @pl.when(jnp.logical_and(i > 0, j >= n_chunks))
    def _():
        tile(tn)


def ffn_up_prenorm(y, x, g_post, g_next, scale, wg, wu, tm=1024, tn=512, rides=()):
    m, d = y.shape
    f = wg.shape[1]
    nj = pl.cdiv(f, tn)
    n_row_blocks = m // tm
    n_chunks = 1
    while n_chunks * 2 <= min(nj, 16):
        n_chunks *= 2
    rc = tm // n_chunks
    col = lambda j: (j + nj - 1) % nj
    chunk = lambda i, j: (jnp.minimum(i * n_chunks + jnp.minimum(j, n_chunks - 1),
                                      n_row_blocks * n_chunks - 1), 0)
    assert all(r.n_blocks <= (n_row_blocks + 1) * nj for r in rides)
    ride_specs = [r.specs(lambda i, j: i * nj + j) for r in rides]
    vec = pl.BlockSpec((1, d), lambda i, j: (0, 0))
    out = pl.pallas_call(
        functools.partial(_ffn_up_prenorm_kernel, n_tail=f - (nj - 1) * tn,
                          ride_blocks=tuple(r.n_blocks for r in rides), scale=scale,
                          n_chunks=n_chunks, n_row_blocks=n_row_blocks),
        grid=(n_row_blocks + 1, nj),
        in_specs=[pl.BlockSpec((rc, d), chunk), pl.BlockSpec((rc, d), chunk), vec, vec,
                  pl.BlockSpec((d, tn), lambda i, j: (0, col(j))),
                  pl.BlockSpec((d, tn), lambda i, j: (0, col(j)))] + [s for s, _ in ride_specs],
        out_specs=[pl.BlockSpec((tm, tn), lambda i, j: (i, col(j))),
                   pl.BlockSpec((rc, d), chunk)] + [dst for _, dst in ride_specs],
        out_shape=[jax.ShapeDtypeStruct((m + tm, f), BF16), jax.ShapeDtypeStruct((m, d), F32)]
        + [r.out_shape for r in rides],
        scratch_shapes=[pltpu.VMEM((2, tm, d), BF16)],
        compiler_params=_cparams(("arbitrary", "arbitrary"),
                                 4 * tm * d + 8 * d * tn + 4 * tm * tn + 16 * tm * tn
                                 + 20 * rc * d + 16 * rc * d + sum(r.vmem_bytes for r in rides)),
        name="ffn_up_prenorm",
    )(y, x, g_post.reshape(1, d), g_next.reshape(1, d), wg, wu, *[r.w_stack for r in rides])
    return out


def _ffn_down_kernel(*refs, k_tail, ride_blocks):
    (a_ref, w_ref), srcs, (o_ref,), dsts, (acc_ref,) = _split_rides(refs, 2, 1, len(ride_blocks))
    k = pl.program_id(2)
    last = pl.num_programs(2) - 1
    tk = a_ref.shape[1]
    step = (pl.program_id(0) * pl.num_programs(1) + pl.program_id(1)) * pl.num_programs(2) + k
    _do_rides(step, ride_blocks, srcs, dsts)

    @pl.when(k == 0)
    def _():
        acc_ref[...] = jnp.zeros_like(acc_ref)

    @pl.when(k < last)
    def _():
        acc_ref[...] += jnp.dot(a_ref[...], w_ref[...], preferred_element_type=F32)

    @pl.when(k == last)
    def _():
        if k_tail == tk:
            part = jnp.dot(a_ref[...], w_ref[...], preferred_element_type=F32)
        else:
            part = jnp.dot(a_ref[:, :k_tail], w_ref[:k_tail, :], preferred_element_type=F32)
        o_ref[...] = (acc_ref[...] + part).astype(o_ref.dtype)


def ffn_down(h, wd, tk, tm=1024, tn=1024, skip_blocks=0, rides=()):
    m, kd = h.shape
    m -= skip_blocks * tm
    n = wd.shape[1]
    nj = n // tn
    nk = pl.cdiv(kd, tk)
    assert all(r.n_blocks <= (m // tm) * nj * nk for r in rides)
    ride_specs = [r.specs(lambda i, j, k: (i * nj + j) * nk + k) for r in rides]
    out = pl.pallas_call(
        functools.partial(_ffn_down_kernel, k_tail=kd - (nk - 1) * tk,
                          ride_blocks=tuple(r.n_blocks for r in rides)),
        grid=(m // tm, nj, nk),
        in_specs=[pl.BlockSpec((tm, tk), lambda i, j, k: (i + skip_blocks, k)),
                  pl.BlockSpec((tk, tn), lambda i, j, k: (k, j))] + [s for s, _ in ride_specs],
        out_specs=[pl.BlockSpec((tm, tn), lambda i, j, k: (i, j))] + [d for _, d in ride_specs],
        out_shape=[jax.ShapeDtypeStruct((m, n), BF16)] + [r.out_shape for r in rides],
        scratch_shapes=[pltpu.VMEM((tm, tn), F32)],
        compiler_params=_cparams(("arbitrary", "arbitrary", "arbitrary"),
                                 4 * (tm * tk + tk * tn) + 4 * tm * tn + 16 * tm * tn
                                 + sum(r.vmem_bytes for r in rides)),
        name="ffn_down",
    )(h, wd, *[r.w_stack for r in rides])
    return out


def _gla_kernel(q_ref, k_ref, v_ref, lr_ref, up_ref, bias_ref, o_ref, st_ref, *,
                backward, n_chunks, reset_blocks, dk):
    i = pl.program_id(1)
    blk = pl.num_programs(1) - 1 - i if backward else i

    @pl.when(_any_equal(blk, reset_blocks))
    def _():
        st_ref[...] = jnp.zeros_like(st_ref)

    c = GLA_CHUNK
    nc = n_chunks
    row = lax.broadcasted_iota(jnp.int32, (c, c), 0)
    col = lax.broadcasted_iota(jnp.int32, (c, c), 1)
    if backward:
        tri, mask = row <= col, row < col
    else:
        tri, mask = row >= col, row >= col
    tri = jnp.broadcast_to(tri.astype(BF16)[None], (nc, c, c))

    def bmm(a, b, ca, cb):
        return lax.dot_general(a, b, (((ca,), (cb,)), ((0,), (0,))), preferred_element_type=F32)

    x = jnp.dot(lr_ref[...].astype(BF16), up_ref[...].astype(BF16),
                preferred_element_type=F32) + bias_ref[...]
    la = (jnp.minimum(x, 0.0) - jnp.log1p(jnp.exp(-jnp.abs(x)))) * (1.0 / GLA_TAU)
    la = la.reshape(nc, c, dk)
    la_hi = la.astype(BF16)
    la_lo = (la - la_hi.astype(F32)).astype(BF16)
    b = bmm(tri, la_hi, 2, 1) + bmm(tri, la_lo, 2, 1)
    tot = jnp.sum(la, axis=1, keepdims=True)
    q = q_ref[...].astype(F32).reshape(nc, c, dk)
    k = k_ref[...].astype(F32).reshape(nc, c, dk)
    v = v_ref[...].reshape(nc, c, v_ref.shape[1])
    qd = (q * (dk ** -0.5) * jnp.exp(b)).astype(BF16)
    kd = (k * jnp.exp(-b)).astype(BF16)
    ks = (k * jnp.exp(tot - b)).astype(BF16)
    scores = jnp.where(mask[None], bmm(qd, kd, 2, 2), 0.0).astype(BF16)
    o_intra = bmm(scores, v, 2, 1)
    st_inc = bmm(v, ks, 1, 1)
    chunk_decay = jnp.exp(tot)

    st = st_ref[...]
    nt = (((1,), (1,)), ((), ()))
    for ci in (reversed(range(nc)) if backward else range(nc)):
        o_ref[ci * c:(ci + 1) * c, :] = o_intra[ci] + lax.dot_general(
            qd[ci], st.astype(BF16), nt, preferred_element_type=F32)
        st = st * chunk_decay[ci] + st_inc[ci]
    st_ref[...] = st


def gla_scan(z, lr, up, bias, seq_bounds, dk, dv, backward, tb=512):
    m = z.shape[0]
    h = GLA_HEADS
    nblk = m // tb
    if backward:
        resets = tuple(s // tb - 1 for s in seq_bounds[1:])
        tok = lambda i: nblk - 1 - i
    else:
        resets = tuple(s // tb for s in seq_bounds[:-1])
        tok = lambda i: i
    k_off = (h * dk) // dk
    v_off = (2 * h * dk) // dv
    lr_col = 1 if backward else 0
    kern = functools.partial(_gla_kernel, backward=backward, n_chunks=tb // GLA_CHUNK,
                             reset_blocks=resets, dk=dk)
    return pl.pallas_call(
        kern,
        grid=(h, nblk),
        in_specs=[pl.BlockSpec((tb, dk), lambda hh, i: (tok(i), hh)),
                  pl.BlockSpec((tb, dk), lambda hh, i: (tok(i), k_off + hh)),
                  pl.BlockSpec((tb, dv), lambda hh, i: (tok(i), v_off + hh)),
                  pl.BlockSpec((tb, LANES), lambda hh, i: (tok(i), lr_col)),
                  pl.BlockSpec((LANES, dk), lambda hh, i: (0, hh)),
                  pl.BlockSpec((1, dk), lambda hh, i: (0, hh))],
        out_specs=pl.BlockSpec((tb, dv), lambda hh, i: (tok(i), hh)),
        out_shape=jax.ShapeDtypeStruct((m, h * dv), F32),
        scratch_shapes=[pltpu.VMEM((dv, dk), F32)],
        compiler_params=_cparams(("arbitrary", "arbitrary"), 40 * 1024 * 1024),
        name="gla_scan_bwd" if backward else "gla_scan_fwd",
    )(z, z, z, lr, up, bias)


def _attn_in_kernel(x_ref, w_ref, *refs):
    out_refs, acc_ref = refs[:-1], refs[-1]
    j = pl.program_id(1)
    n_chunks, tm, _ = acc_ref.shape
    res = jnp.dot(x_ref[...], w_ref[...], preferred_element_type=F32)
    for c in range(n_chunks):
        acc_ref[c] = res[:, c * LANES:(c + 1) * LANES]
    for g, (o_ref, d) in enumerate(zip(out_refs, DIL_RATES)):
        @pl.when(jnp.logical_or(j == g, j >= N_DIL_GROUPS))
        def _(o_ref=o_ref, d=d):
            for r in range(d):
                for c in range(n_chunks):
                    rows = pl.ds(r, tm // d, stride=d) if d > 1 else slice(None)
                    o_ref[r, :, c * LANES:(c + 1) * LANES] = (
                        acc_ref[c, rows, :].astype(o_ref.dtype))


def attn_in(xn, w_att, col0, tm=512):
    m, kd = xn.shape
    w = ATTN_KV_W
    nj = N_DIL_GROUPS + 2
    blk0 = col0 // w
    assert blk0 * w == col0
    sect = lambda i, j: (jnp.clip(j - (N_DIL_GROUPS - 1), 0, 2), 0, i, 0)
    return pl.pallas_call(
        _attn_in_kernel,
        grid=(m // tm, nj),
        in_specs=[pl.BlockSpec((tm, kd), lambda i, j: (i, 0)),
                  pl.BlockSpec((kd, w), lambda i, j: (0, blk0 + j))],
        out_specs=[pl.BlockSpec((None, d, tm // d, w), sect) for d in DIL_RATES],
        out_shape=[jax.ShapeDtypeStruct((3, d, m // d, w), BF16) for d in DIL_RATES],
        scratch_shapes=[pltpu.VMEM((w // LANES, tm, LANES), F32)],
        compiler_params=_cparams(("parallel", "arbitrary"),
                                 4 * (tm * kd + kd * w) + 12 * tm * w + 8 * tm * w),
        name="attn_in",
    )(xn, w_att)


def _dil_kernel(q_ref, kp_ref, ko_ref, kn_ref, vp_ref, vo_ref, vn_ref, o_ref, l_ref, *,
                dilation, group, qb, start_tiles, end_tiles):
    t = pl.program_id(0)
    n_res = q_ref.shape[0]
    res0 = pl.program_id(1) * n_res
    has_prev = jnp.logical_not(_any_equal(t, start_tiles))
    has_next = jnp.logical_not(_any_equal(t + 1, end_tiles))
    r = DIL_WINDOWS[group] // (2 * dilation)
    e = ATTN_HEAD_DIM
    qi = lax.broadcasted_iota(jnp.int32, (r, 3 * r), 0)
    kj = lax.broadcasted_iota(jnp.int32, (r, 3 * r), 1)
    rel = jnp.abs(kj - r - qi)
    in_window = rel <= r
    rel_f = rel.astype(F32) * float(dilation)
    prev_ok = jnp.logical_or(kj >= r, has_prev)
    next_ok = jnp.logical_or(kj < 2 * r, has_next)

    def valid(j):
        v = in_window
        if j == 0:
            v = jnp.logical_and(v, prev_ok)
        if j == qb - 1:
            v = jnp.logical_and(v, next_ok)
        return v

    valids = {j: valid(j) for j in {0, qb - 1, min(1, qb - 1)}}
    n_heads = N_DIL_GROUPS * ATTN_KV_HEADS
    scale = e ** -0.5
    nh = ATTN_KV_HEADS

    def heads(ref, ri, rows):
        return jnp.stack([ref[ri, rows, h * e:(h + 1) * e] for h in range(nh)])

    slopes = [2.0 ** (-ALIBI_MAX_BIAS * (group * nh + h + 1) / n_heads) for h in range(nh)]
    bias = jnp.stack([slope * rel_f for slope in slopes])
    every = slice(None)
    qk_dims = (((2,), (2,)), ((0,), (0,)))
    pv_dims = (((2,), (1,)), ((0,), (0,)))
    for ri in range(n_res):
        k_all = jnp.concatenate([heads(kp_ref, ri, every), heads(ko_ref, ri, every),
                                 heads(kn_ref, ri, every)], axis=1)
        v_all = jnp.concatenate([heads(vp_ref, ri, every), heads(vo_ref, ri, every),
                                 heads(vn_ref, ri, every)], axis=1)
        for j in range(qb):
            q = heads(q_ref, ri, slice(j * r, (j + 1) * r))
            s = lax.dot_general(q, k_all[:, j * r:(j + 3) * r], qk_dims,
                                preferred_element_type=F32)
            s = jnp.where(valids.get(j, in_window)[None], s * scale - bias, NEG_INF)
            mx = jnp.max(s, axis=-1, keepdims=True)
            p = jnp.exp(s - mx)
            den = jnp.sum(p, axis=-1, keepdims=True)
            o = lax.dot_general(p.astype(BF16), v_all[:, j * r:(j + 3) * r], pv_dims,
                                preferred_element_type=F32) / den
            lse = jnp.broadcast_to(mx + jnp.log(den), (nh, r, e))
            if dilation == 1:
                dst = pl.ds(j * r, r)
            else:
                dst = pl.ds(j * r * dilation + res0 + ri, r, stride=dilation)
            for h in range(nh):
                o_ref[h, dst, :] = o[h]
                l_ref[h, dst, :] = lse[h]


def dilated_attention_group(a, group, seq_bounds, qb, n_res):
    d = DIL_RATES[group]
    r = DIL_WINDOWS[group] // (2 * d)
    w = ATTN_KV_W
    rows = a.shape[2]
    m = rows * d
    tile = qb * r * d
    n_tiles = m // tile
    n_blocks = rows // r
    starts = tuple(s // tile for s in seq_bounds[:-1])
    ends = tuple(s // tile for s in seq_bounds[1:])

    own = lambda sec: pl.BlockSpec((None, n_res, qb * r, w), lambda t, rr: (sec, rr, t, 0))
    prev = lambda sec: pl.BlockSpec(
        (None, n_res, r, w), lambda t, rr: (sec, rr, jnp.maximum(t * qb - 1, 0), 0))
    nxt = lambda sec: pl.BlockSpec(
        (None, n_res, r, w), lambda t, rr: (sec, rr, jnp.minimum((t + 1) * qb, n_blocks - 1), 0))
    hm = (ATTN_KV_HEADS, m, ATTN_HEAD_DIM)
    out = pl.BlockSpec((ATTN_KV_HEADS, tile, ATTN_HEAD_DIM), lambda t, rr: (0, t, 0))
    kern = functools.partial(_dil_kernel, dilation=d, group=group, qb=qb,
                             start_tiles=starts, end_tiles=ends)
    return pl.pallas_call(
        kern,
        grid=(n_tiles, d // n_res),
        in_specs=[own(0), prev(1), own(1), nxt(1), prev(2), own(2), nxt(2)],
        out_specs=[out, out],
        out_shape=[jax.ShapeDtypeStruct(hm, F32), jax.ShapeDtypeStruct(hm, F32)],
        compiler_params=_cparams(("parallel", "arbitrary"),
                                 16 * tile * w + n_res * (12 * qb + 16) * r * w + 16 * r * w * 4),
        name=f"dilated_attn_g{group}",
    )(a, a, a, a, a, a, a)


def _mixer_post_kernel(of_ref, ob_ref, r_ref, g_ref, o0_ref, o1_ref, o2_ref,
                       l0_ref, l1_ref, l2_ref, ogla_ref, oatt_ref, *, dv):
    o = of_ref[...] + ob_ref[...]
    r = r_ref[...].astype(F32)
    gate = r * jax.nn.sigmoid(r) * g_ref[...]
    for h in range(GLA_HEADS):
        sl = slice(h * dv, (h + 1) * dv)
        oh = o[:, sl]
        ms = jnp.mean(oh * oh, axis=-1, keepdims=True)
        ogla_ref[:, sl] = (oh * lax.rsqrt(ms + NORM_EPS) * gate[:, sl]).astype(ogla_ref.dtype)
    e = ATTN_HEAD_DIM
    for h in range(ATTN_KV_HEADS):
        l0, l1, l2 = l0_ref[h], l1_ref[h], l2_ref[h]
        mx = jnp.maximum(jnp.maximum(l0, l1), l2)
        w0, w1, w2 = jnp.exp(l0 - mx), jnp.exp(l1 - mx), jnp.exp(l2 - mx)
        num = w0 * o0_ref[h] + w1 * o1_ref[h] + w2 * o2_ref[h]
        oatt_ref[:, h * e:(h + 1) * e] = (num / (w0 + w1 + w2)).astype(oatt_ref.dtype)


def mixer_post(o_f, o_b, z, r_col, gla_norm_g, outs, lses, dv, tr=256):
    m = z.shape[0]
    vw = GLA_HEADS * dv
    kvw = ATTN_KV_W
    row = lambda w: pl.BlockSpec((tr, w), lambda i: (i, 0))
    heads = pl.BlockSpec((ATTN_KV_HEADS, tr, ATTN_HEAD_DIM), lambda i: (0, i, 0))
    return pl.pallas_call(
        functools.partial(_mixer_post_kernel, dv=dv),
        grid=(m // tr,),
        in_specs=[row(vw), row(vw),
                  pl.BlockSpec((tr, vw), lambda i: (i, r_col)),
                  pl.BlockSpec((1, vw), lambda i: (0, 0))] + [heads] * 6,
        out_specs=[row(vw), row(kvw)],
        out_shape=[jax.ShapeDtypeStruct((m, vw), BF16),
                   jax.ShapeDtypeStruct((m, kvw), BF16)],
        compiler_params=_cparams(("parallel",), 36 * 1024 * 1024),
        name="mixer_post",
    )(o_f, o_b, z, gla_norm_g.reshape(1, vw), *outs, *lses)


def _merge_proj_kernel(og_ref, oa_ref, gg_ref, ga_ref, pg_ref, pa_ref, o_ref):
    yg = jnp.dot(og_ref[...], pg_ref[...], preferred_element_type=F32)
    ya = jnp.dot(oa_ref[...], pa_ref[...], preferred_element_type=F32)
    gg = jax.nn.sigmoid(gg_ref[...].astype(F32))
    ga = jax.nn.sigmoid(ga_ref[...].astype(F32))
    o_ref[...] = (gg * yg + ga * ya).astype(o_ref.dtype)


def merge_proj(o_gla, o_att, z, gate_start, proj_gla, proj_attn, tm=1024, tn=512):
    m, vw = o_gla.shape
    kvw = o_att.shape[1]
    d = proj_gla.shape[1]
    nj = d // tn
    gate_col = gate_start // tn
    return pl.pallas_call(
        _merge_proj_kernel,
        grid=(m // tm, nj),
        in_specs=[pl.BlockSpec((tm, vw), lambda i, j: (i, 0)),
                  pl.BlockSpec((tm, kvw), lambda i, j: (i, 0)),
                  pl.BlockSpec((tm, tn), lambda i, j: (i, gate_col + j)),
                  pl.BlockSpec((tm, tn), lambda i, j: (i, gate_col + nj + j)),
                  pl.BlockSpec((vw, tn), lambda i, j: (0, j)),
                  pl.BlockSpec((kvw, tn), lambda i, j: (0, j))],
        out_specs=pl.BlockSpec((tm, tn), lambda i, j: (i, j)),
        out_shape=jax.ShapeDtypeStruct((m, d), BF16),
        compiler_params=_cparams(("parallel", "arbitrary"), 40 * 1024 * 1024),
        name="merge_proj",
    )(o_gla, o_att, z, z, proj_gla, proj_attn)


def _cast_kernel(w_ref, o_ref):
    o_ref[...] = w_ref[...].astype(o_ref.dtype)


def cast_layer(w_stack, layer, tr=128):
    _, rows, cols = w_stack.shape
    return pl.pallas_call(
        _cast_kernel,
        grid=(rows // tr,),
        in_specs=[pl.BlockSpec((None, tr, cols), lambda i: (layer, i, 0))],
        out_specs=pl.BlockSpec((tr, cols), lambda i: (i, 0)),
        out_shape=jax.ShapeDtypeStruct((rows, cols), BF16),
        compiler_params=_cparams(("parallel",), 16 * tr * cols),
        name="cast_layer",
    )(w_stack)


def _pad_to(x, axis, size):
    pad = [(0, 0)] * x.ndim
    pad[axis] = (0, size - x.shape[axis])
    return jnp.pad(x, pad)


def kernel(x_prompt, x_sample, ffn1_pre_g, ffn1_w_gate, ffn1_w_up, ffn1_w_down, ffn1_post_g,
           mix_pre_g, w_in, gla_decay_up_fwd, gla_decay_bias_fwd, gla_decay_up_bwd,
           gla_decay_bias_bwd, gla_norm_g, proj_gla, proj_attn, w_out, mix_post_g,
           ffn2_pre_g, ffn2_w_gate, ffn2_w_up, ffn2_w_down, ffn2_post_g):
    depth, d, d_ff = ffn1_w_gate.shape
    bp, tp, _ = x_prompt.shape
    bs, ts, _ = x_sample.shape
    mp, ms = bp * tp, bs * ts
    seq_bounds = tuple(b * tp for b in range(bp)) + tuple(mp + b * ts for b in range(bs + 1))

    dk = d // 16
    dv = d // 8
    qk_w = GLA_HEADS * dk
    v_w = GLA_HEADS * dv
    q_w = N_DIL_GROUPS * ATTN_KV_W
    lr0 = 2 * qk_w + 2 * v_w
    a0 = lr0 + 2 * GLA_RANK
    g0 = a0 + q_w + 2 * ATTN_KV_W
    ff_tk = -(-d_ff // (4 * 256)) * 256

    x = (x_prompt.reshape(mp, d), x_sample.reshape(ms, d))
    xn = rmsnorm(x, ffn1_pre_g[0])

    m_blocks = (mp + ms) // 1024
    up_steps = m_blocks * pl.cdiv(d_ff, 512)
    down_steps = m_blocks * (d // 1024) * pl.cdiv(d_ff, ff_tk)
    mixer_in_steps = m_blocks * ((lr0 + 2 * d) // 1024)

    wg1, wu1 = cast_layer(ffn1_w_gate, 0), cast_layer(ffn1_w_up, 0)
    pending = None
    for l in range(depth):
        wd_ride = (Ride(ffn1_w_down, l, up_steps),)
        if pending is None:
            hmid, wd1 = ffn_up(xn, wg1, wu1, rides=wd_ride)
        else:
            y_prev, g_prev, scale_prev = pending
            hmid, x, wd1 = ffn_up_prenorm(y_prev, x, g_prev, ffn1_pre_g[l], scale_prev,
                                          wg1, wu1, rides=wd_ride)
        y, = ffn_down(hmid, wd1, ff_tk, skip_blocks=0 if pending is None else 1)
        x, xn = norm_residual(y, x, ffn1_post_g[l], mix_pre_g[l], 0.5)

        w_all = jnp.concatenate(
            [w_in[l][:, :lr0], w_in[l][:, a0:],
             _pad_to(w_in[l][:, lr0:lr0 + GLA_RANK], 1, LANES),
             _pad_to(w_in[l][:, lr0 + GLA_RANK:a0], 1, LANES)], axis=1).astype(BF16)
        att_tiles = (g0 - a0) // 1024
        z, wg2, wu2, pg, pa, wo = matmul(
            xn, w_all, BF16, name="mixer_in",
            w_cols=(lr0 + 2 * d, lambda j: jnp.where(j < lr0 // 1024, j, j + att_tiles)),
            rides=(Ride(ffn2_w_gate, l, mixer_in_steps), Ride(ffn2_w_up, l, mixer_in_steps),
                   Ride(proj_gla, l, mixer_in_steps), Ride(proj_attn, l, mixer_in_steps),
                   Ride(w_out, l, mixer_in_steps)))
        lr_blk = (w_all.shape[1] - 2 * LANES) // (2 * LANES)
        lr = matmul(xn, w_all, F32, tn=2 * LANES, w_cols=(2 * LANES, lambda j: lr_blk),
                    name="mixer_in_lr")
        o_f = gla_scan(z, lr, _pad_to(gla_decay_up_fwd[l], 0, LANES),
                       gla_decay_bias_fwd[l].reshape(1, qk_w), seq_bounds, dk, dv, False)
        o_b = gla_scan(z, lr, _pad_to(gla_decay_up_bwd[l], 0, LANES),
                       gla_decay_bias_bwd[l].reshape(1, qk_w), seq_bounds, dk, dv, True)

        qkv = attn_in(xn, w_all, lr0)
        outs, lses = [], []
        for g, (qb, n_res) in enumerate(((8, 1), (4, 4), (1, 4))):
            o_g, l_g = dilated_attention_group(qkv[g], g, seq_bounds, qb, n_res)
            outs.append(o_g)
            lses.append(l_g)

        o_gla, o_att = mixer_post(o_f, o_b, z, (2 * qk_w + v_w) // v_w, gla_norm_g[l], outs, lses, dv)
        merged = merge_proj(o_gla, o_att, z, lr0, pg, pa)
        y = matmul(merged, wo, BF16, name="mixer_out")

        hmid, x, wd2 = ffn_up_prenorm(y, x, mix_post_g[l], ffn2_pre_g[l], 1.0, wg2, wu2,
                                      rides=(Ride(ffn2_w_down, l, up_steps),))
        if l + 1 == depth:
            y, = ffn_down(hmid, wd2, ff_tk, skip_blocks=1)
        else:
            y, wg1, wu1 = ffn_down(hmid, wd2, ff_tk, skip_blocks=1,
                                   rides=(Ride(ffn1_w_gate, l + 1, down_steps),
                                          Ride(ffn1_w_up, l + 1, down_steps)))
        pending = (y, ffn2_post_g[l], 0.5)

    y, g_post, scale = pending
    x, _ = norm_residual(y, x, g_post, None, scale, out_rows=(mp, ms))
    return (x[0].reshape(bp, tp, d), x[1].reshape(bs, ts, d))
```

```python
import functools

import jax
import jax.numpy as jnp
from jax import lax
from jax.experimental import pallas as pl
from jax.experimental.pallas import tpu as pltpu

F32 = jnp.float32
BF16 = jnp.bfloat16

NORM_EPS = 1e-6
NEG_INF = -1e30

GLA_HEADS = 4
GLA_RANK = 16
GLA_TAU = 16.0
GLA_CHUNK = 64
ATTN_KV_HEADS = 8
ATTN_HEAD_DIM = 128
ATTN_KV_W = ATTN_KV_HEADS * ATTN_HEAD_DIM
DIL_WINDOWS = (128, 512, 2048)
DIL_RATES = (1, 4, 16)
N_DIL_GROUPS = 3
ALIBI_MAX_BIAS = 8.0

LANES = 128
VMEM_LIMIT_CAP = 60 * 1024 * 1024


def _cparams(dims, vmem_bytes):
    return pltpu.CompilerParams(dimension_semantics=dims,
                                vmem_limit_bytes=int(min(vmem_bytes * 5 // 4, VMEM_LIMIT_CAP)))


def _any_equal(idx, values):
    return functools.reduce(jnp.logical_or, [idx == v for v in values])


def _part_specs(n_first, tr, d):
    return [pl.BlockSpec((tr, d), lambda i: (jnp.minimum(i, n_first - 1), 0)),
            pl.BlockSpec((tr, d), lambda i: (jnp.maximum(i - n_first, 0), 0))]


def _rmsnorm_kernel(xa_ref, xb_ref, g_ref, o_ref, *, n_first):
    x = jnp.where(pl.program_id(0) < n_first, xa_ref[...], xb_ref[...])
    ms = jnp.mean(x * x, axis=-1, keepdims=True)
    o_ref[...] = (x * lax.rsqrt(ms + NORM_EPS) * g_ref[...]).astype(o_ref.dtype)


def rmsnorm(x_parts, g, tr=256):
    d = x_parts[0].shape[1]
    n_first = x_parts[0].shape[0] // tr
    m = x_parts[0].shape[0] + x_parts[1].shape[0]
    return pl.pallas_call(
        functools.partial(_rmsnorm_kernel, n_first=n_first),
        grid=(m // tr,),
        in_specs=_part_specs(n_first, tr, d) + [pl.BlockSpec((1, d), lambda i: (0, 0))],
        out_specs=pl.BlockSpec((tr, d), lambda i: (i, 0)),
        out_shape=jax.ShapeDtypeStruct((m, d), BF16),
        compiler_params=_cparams(("arbitrary",), 10 * tr * d * 4),
        name="rmsnorm",
    )(*x_parts, g.reshape(1, d))


def _norm_residual_kernel(y_ref, *refs, scale, with_next, n_in_first, n_out_first):
    i = pl.program_id(0)
    refs = list(refs)
    if n_in_first is None:
        x = refs.pop(0)[...]
    else:
        xa_ref, xb_ref = refs.pop(0), refs.pop(0)
        x = jnp.where(i < n_in_first, xa_ref[...], xb_ref[...])
    gp_ref = refs.pop(0)
    y = y_ref[...].astype(F32)
    ms = jnp.mean(y * y, axis=-1, keepdims=True)
    xo = x + scale * (y * lax.rsqrt(ms + NORM_EPS) * gp_ref[...])
    if with_next:
        gn_ref, xn_ref = refs.pop(0), refs.pop()
        ms2 = jnp.mean(xo * xo, axis=-1, keepdims=True)
        xn_ref[...] = (xo * lax.rsqrt(ms2 + NORM_EPS) * gn_ref[...]).astype(xn_ref.dtype)
    if n_out_first is None:
        refs[0][...] = xo
    else:
        @pl.when(i < n_out_first)
        def _():
            refs[0][...] = xo

        @pl.when(i >= n_out_first)
        def _():
            refs[1][...] = xo


def norm_residual(y, x, g_post, g_next, scale, out_rows=None, tr=256):
    m, d = y.shape
    with_next = g_next is not None
    row = pl.BlockSpec((tr, d), lambda i: (i, 0))
    vec = pl.BlockSpec((1, d), lambda i: (0, 0))
    gains = [g_post.reshape(1, d)] + ([g_next.reshape(1, d)] if with_next else [])
    x_parts = x if isinstance(x, (tuple, list)) else None
    n_in_first = None if x_parts is None else x_parts[0].shape[0] // tr
    n_out_first = None if out_rows is None else out_rows[0] // tr
    x_specs = [row] if x_parts is None else _part_specs(n_in_first, tr, d)
    if out_rows is None:
        xo_specs, xo_shapes = [row], [jax.ShapeDtypeStruct((m, d), F32)]
    else:
        xo_specs = _part_specs(n_out_first, tr, d)
        xo_shapes = [jax.ShapeDtypeStruct((rows, d), F32) for rows in out_rows]
    out = pl.pallas_call(
        functools.partial(_norm_residual_kernel, scale=scale, with_next=with_next,
                          n_in_first=n_in_first, n_out_first=n_out_first),
        grid=(m // tr,),
        in_specs=[row] + x_specs + [vec] * len(gains),
        out_specs=xo_specs + ([row] if with_next else []),
        out_shape=xo_shapes + ([jax.ShapeDtypeStruct((m, d), BF16)] if with_next else []),
        compiler_params=_cparams(("arbitrary",), 14 * tr * d * 4),
        name="norm_residual",
    )(y, *(x_parts if x_parts is not None else [x]), *gains)
    xo = out[0] if out_rows is None else tuple(out[:2])
    return xo, (out[-1] if with_next else None)


class Ride:
    def __init__(self, w_stack, layer, max_blocks, min_tr=16):
        _, self.rows, self.cols = w_stack.shape
        tr = min_tr
        while self.rows % tr or self.rows // tr > max_blocks:
            tr *= 2
            assert tr <= self.rows
        self.w_stack, self.layer, self.tr = w_stack, layer, tr
        self.n_blocks = self.rows // tr

    def specs(self, step_of):
        blk = lambda *g: jnp.minimum(step_of(*g), self.n_blocks - 1)
        src = pl.BlockSpec((None, self.tr, self.cols), lambda *g: (self.layer, blk(*g), 0))
        dst = pl.BlockSpec((self.tr, self.cols), lambda *g: (blk(*g), 0))
        return src, dst

    @property
    def out_shape(self):
        return jax.ShapeDtypeStruct((self.rows, self.cols), BF16)

    @property
    def vmem_bytes(self):
        return 12 * self.tr * self.cols


def _split_rides(refs, n_base_in, n_base_out, n_rides):
    a = n_base_in
    b = a + n_rides
    c = b + n_base_out
    d = c + n_rides
    return refs[:a], refs[a:b], refs[b:c], refs[c:d], refs[d:]


def _do_rides(step, rides_n_blocks, srcs, dsts):
    for n_blocks, src, dst in zip(rides_n_blocks, srcs, dsts):
        @pl.when(step < n_blocks)
        def _(src=src, dst=dst):
            dst[...] = src[...].astype(dst.dtype)


def _mm_kernel(*refs, ride_blocks):
    (a_ref, w_ref), srcs, (o_ref,), dsts, _ = _split_rides(refs, 2, 1, len(ride_blocks))
    o_ref[...] = jnp.dot(a_ref[...], w_ref[...],
                         preferred_element_type=F32).astype(o_ref.dtype)
    _do_rides(pl.program_id(0) * pl.num_programs(1) + pl.program_id(1), ride_blocks, srcs, dsts)


def matmul(a, w, out_dtype, tm=1024, tn=1024, w_cols=None, skip_blocks=0, rides=(),
           name="matmul"):
    m, kd = a.shape
    m -= skip_blocks * tm
    n, col_of = w_cols if w_cols is not None else (w.shape[1], lambda j: j)
    tn = min(tn, n)
    nj = n // tn
    osz = jnp.dtype(out_dtype).itemsize
    assert all(r.n_blocks <= (m // tm) * nj for r in rides)
    ride_specs = [r.specs(lambda i, j: i * nj + j) for r in rides]
    out = pl.pallas_call(
        functools.partial(_mm_kernel, ride_blocks=tuple(r.n_blocks for r in rides)),
        grid=(m // tm, nj),
        in_specs=[pl.BlockSpec((tm, kd), lambda i, j: (i + skip_blocks, 0)),
                  pl.BlockSpec((kd, tn), lambda i, j: (0, col_of(j)))] + [s for s, _ in ride_specs],
        out_specs=[pl.BlockSpec((tm, tn), lambda i, j: (i, j))] + [d for _, d in ride_specs],
        out_shape=[jax.ShapeDtypeStruct((m, n), out_dtype)] + [r.out_shape for r in rides],
        compiler_params=_cparams(("arbitrary", "arbitrary"),
                                 4 * (tm * kd + kd * tn) + 2 * tm * tn * osz + 3 * tm * tn * 4
                                 + sum(r.vmem_bytes for r in rides)),
        name=name,
    )(a, w, *[r.w_stack for r in rides])
    return out[0] if not rides else out


def _ffn_up_kernel(*refs, n_tail, ride_blocks):
    (x_ref, wg_ref, wu_ref), srcs, (o_ref,), dsts, _ = _split_rides(refs, 3, 1, len(ride_blocks))
    j = pl.program_id(1)
    tn = o_ref.shape[1]
    _do_rides(pl.program_id(0) * pl.num_programs(1) + j, ride_blocks, srcs, dsts)

    def tile(width):
        x = x_ref[...]
        g = jnp.dot(x, wg_ref[:, :width], preferred_element_type=F32)
        u = jnp.dot(x, wu_ref[:, :width], preferred_element_type=F32)
        o_ref[:, :width] = (g * jax.nn.sigmoid(g) * u).astype(o_ref.dtype)

    if n_tail == tn:
        tile(tn)
    else:
        pl.when(j == 0)(lambda: tile(n_tail))
        pl.when(j > 0)(lambda: tile(tn))


def ffn_up(xn, wg, wu, tm=1024, tn=512, rides=()):
    m, d = xn.shape
    f = wg.shape[1]
    nj = pl.cdiv(f, tn)
    col = lambda j: (j + nj - 1) % nj
    assert all(r.n_blocks <= (m // tm) * nj for r in rides)
    ride_specs = [r.specs(lambda i, j: i * nj + j) for r in rides]
    out = pl.pallas_call(
        functools.partial(_ffn_up_kernel, n_tail=f - (nj - 1) * tn,
                          ride_blocks=tuple(r.n_blocks for r in rides)),
        grid=(m // tm, nj),
        in_specs=[pl.BlockSpec((tm, d), lambda i, j: (i, 0)),
                  pl.BlockSpec((d, tn), lambda i, j: (0, col(j))),
                  pl.BlockSpec((d, tn), lambda i, j: (0, col(j)))] + [s for s, _ in ride_specs],
        out_specs=[pl.BlockSpec((tm, tn), lambda i, j: (i, col(j)))] + [d for _, d in ride_specs],
        out_shape=[jax.ShapeDtypeStruct((m, f), BF16)] + [r.out_shape for r in rides],
        compiler_params=_cparams(("arbitrary", "arbitrary"),
                                 4 * tm * d + 8 * d * tn + 4 * tm * tn + 16 * tm * tn
                                 + sum(r.vmem_bytes for r in rides)),
        name="ffn_up",
    )(xn, wg, wu, *[r.w_stack for r in rides])
    return out


def _ffn_up_prenorm_kernel(*refs, n_tail, ride_blocks, scale, n_chunks, n_row_blocks):
    ((y_ref, x_ref, gp_ref, gn_ref, wg_ref, wu_ref), srcs, (o_ref, xo_ref), dsts,
     (xn_buf,)) = _split_rides(refs, 6, 2, len(ride_blocks))
    i = pl.program_id(0)
    j = pl.program_id(1)
    tn = o_ref.shape[1]
    rc = y_ref.shape[0]
    _do_rides(i * pl.num_programs(1) + j, ride_blocks, srcs, dsts)

    def prenorm():
        c = jnp.minimum(j, n_chunks - 1)
        y = y_ref[...].astype(F32)
        ms = jnp.mean(y * y, axis=-1, keepdims=True)
        xo = x_ref[...] + scale * (y * lax.rsqrt(ms + NORM_EPS) * gp_ref[...])
        xo_ref[...] = xo
        ms2 = jnp.mean(xo * xo, axis=-1, keepdims=True)
        xn_buf[i % 2, pl.ds(pl.multiple_of(c * rc, rc), rc), :] = (
            xo * lax.rsqrt(ms2 + NORM_EPS) * gn_ref[...]).astype(xn_buf.dtype)

    def tile(width):
        x = xn_buf[(i + 1) % 2]
        g = jnp.dot(x, wg_ref[:, :width], preferred_element_type=F32)
        u = jnp.dot(x, wu_ref[:, :width], preferred_element_type=F32)
        o_ref[:, :width] = (g * jax.nn.sigmoid(g) * u).astype(o_ref.dtype)

    @pl.when(i == 0)
    def _():
        o_ref[...] = jnp.zeros_like(o_ref)
        prenorm()

    @pl.when(jnp.logical_and(i > 0, j == 0))
    def _():
        tile(n_tail)
        prenorm()

    @pl.when(jnp.logical_and(i > 0, jnp.logical_and(j > 0, j < n_chunks)))
    def _():
        tile(tn)
        prenorm()

    @pl.when(jnp.logical_and(i > 0, j >= n_chunks))
    def _():
        tile(tn)


def ffn_up_prenorm(y, x, g_post, g_next, scale, wg, wu, tm=1024, tn=512, rides=()):
    m, d = y.shape
    f = wg.shape[1]
    nj = pl.cdiv(f, tn)
    n_row_blocks = m // tm
    n_chunks = 1
    while n_chunks * 2 <= min(nj, 16):
        n_chunks *= 2
    rc = tm // n_chunks
    col = lambda j: (j + nj - 1) % nj
    chunk = lambda i, j: (jnp.minimum(i * n_chunks + jnp.minimum(j, n_chunks - 1),
                                      n_row_blocks * n_chunks - 1), 0)
    assert all(r.n_blocks <= (n_row_blocks + 1) * nj for r in rides)
    ride_specs = [r.specs(lambda i, j: i * nj + j) for r in rides]
    vec = pl.BlockSpec((1, d), lambda i, j: (0, 0))
    out = pl.pallas_call(
        functools.partial(_ffn_up_prenorm_kernel, n_tail=f - (nj - 1) * tn,
                          ride_blocks=tuple(r.n_blocks for r in rides), scale=scale,
                          n_chunks=n_chunks, n_row_blocks=n_row_blocks),
        grid=(n_row_blocks + 1, nj),
        in_specs=[pl.BlockSpec((rc, d), chunk), pl.BlockSpec((rc, d), chunk), vec, vec,
                  pl.BlockSpec((d, tn), lambda i, j: (0, col(j))),
                  pl.BlockSpec((d, tn), lambda i, j: (0, col(j)))] + [s for s, _ in ride_specs],
        out_specs=[pl.BlockSpec((tm, tn), lambda i, j: (i, col(j))),
                   pl.BlockSpec((rc, d), chunk)] + [dst for _, dst in ride_specs],
        out_shape=[jax.ShapeDtypeStruct((m + tm, f), BF16), jax.ShapeDtypeStruct((m, d), F32)]
        + [r.out_shape for r in rides],
        scratch_shapes=[pltpu.VMEM((2, tm, d), BF16)],
        compiler_params=_cparams(("arbitrary", "arbitrary"),
                                 4 * tm * d + 8 * d * tn + 4 * tm * tn + 16 * tm * tn
                                 + 20 * rc * d + 16 * rc * d + sum(r.vmem_bytes for r in rides)),
        name="ffn_up_prenorm",
    )(y, x, g_post.reshape(1, d), g_next.reshape(1, d), wg, wu, *[r.w_stack for r in rides])
    return out


def _ffn_down_kernel(*refs, k_tail, ride_blocks):
    (a_ref, w_ref), srcs, (o_ref,), dsts, (acc_ref,) = _split_rides(refs, 2, 1, len(ride_blocks))
    k = pl.program_id(2)
    last = pl.num_programs(2) - 1
    tk = a_ref.shape[1]
    step = (pl.program_id(0) * pl.num_programs(1) + pl.program_id(1)) * pl.num_programs(2) + k
    _do_rides(step, ride_blocks, srcs, dsts)

    @pl.when(k == 0)
    def _():
        acc_ref[...] = jnp.zeros_like(acc_ref)

    @pl.when(k < last)
    def _():
        acc_ref[...] += jnp.dot(a_ref[...], w_ref[...], preferred_element_type=F32)

    @pl.when(k == last)
    def _():
        if k_tail == tk:
            part = jnp.dot(a_ref[...], w_ref[...], preferred_element_type=F32)
        else:
            part = jnp.dot(a_ref[:, :k_tail], w_ref[:k_tail, :], preferred_element_type=F32)
        o_ref[...] = (acc_ref[...] + part).astype(o_ref.dtype)


def ffn_down(h, wd, tk, tm=1024, tn=1024, skip_blocks=0, rides=()):
    m, kd = h.shape
    m -= skip_blocks * tm
    n = wd.shape[1]
    nj = n // tn
    nk = pl.cdiv(kd, tk)
    assert all(r.n_blocks <= (m // tm) * nj * nk for r in rides)
    ride_specs = [r.specs(lambda i, j, k: (i * nj + j) * nk + k) for r in rides]
    out = pl.pallas_call(
        functools.partial(_ffn_down_kernel, k_tail=kd - (nk - 1) * tk,
                          ride_blocks=tuple(r.n_blocks for r in rides)),
        grid=(m // tm, nj, nk),
        in_specs=[pl.BlockSpec((tm, tk), lambda i, j, k: (i + skip_blocks, k)),
                  pl.BlockSpec((tk, tn), lambda i, j, k: (k, j))] + [s for s, _ in ride_specs],
        out_specs=[pl.BlockSpec((tm, tn), lambda i, j, k: (i, j))] + [d for _, d in ride_specs],
        out_shape=[jax.ShapeDtypeStruct((m, n), BF16)] + [r.out_shape for r in rides],
        scratch_shapes=[pltpu.VMEM((tm, tn), F32)],
        compiler_params=_cparams(("arbitrary", "arbitrary", "arbitrary"),
                                 4 * (tm * tk + tk * tn) + 4 * tm * tn + 16 * tm * tn
                                 + sum(r.vmem_bytes for r in rides)),
        name="ffn_down",
    )(h, wd, *[r.w_stack for r in rides])
    return out


def _gla_kernel(q_ref, k_ref, v_ref, lr_ref, up_ref, bias_ref, o_ref, st_ref, *,
                backward, n_chunks, reset_blocks, dk):
    i = pl.program_id(1)
    blk = pl.num_programs(1) - 1 - i if backward else i

    @pl.when(_any_equal(blk, reset_blocks))
    def _():
        st_ref[...] = jnp.zeros_like(st_ref)

    c = GLA_CHUNK
    nc = n_chunks
    row = lax.broadcasted_iota(jnp.int32, (c, c), 0)
    col = lax.broadcasted_iota(jnp.int32, (c, c), 1)
    if backward:
        tri, mask = row <= col, row < col
    else:
        tri, mask = row >= col, row >= col
    tri = jnp.broadcast_to(tri.astype(BF16)[None], (nc, c, c))

    def bmm(a, b, ca, cb):
        return lax.dot_general(a, b, (((ca,), (cb,)), ((0,), (0,))), preferred_element_type=F32)

    x = jnp.dot(lr_ref[...].astype(BF16), up_ref[...].astype(BF16),
                preferred_element_type=F32) + bias_ref[...]
    la = (jnp.minimum(x, 0.0) - jnp.log1p(jnp.exp(-jnp.abs(x)))) * (1.0 / GLA_TAU)
    la = la.reshape(nc, c, dk)
    la_hi = la.astype(BF16)
    la_lo = (la - la_hi.astype(F32)).astype(BF16)
    b = bmm(tri, la_hi, 2, 1) + bmm(tri, la_lo, 2, 1)
    tot = jnp.sum(la, axis=1, keepdims=True)
    q = q_ref[...].astype(F32).reshape(nc, c, dk)
    k = k_ref[...].astype(F32).reshape(nc, c, dk)
    v = v_ref[...].reshape(nc, c, v_ref.shape[1])
    qd = (q * (dk ** -0.5) * jnp.exp(b)).astype(BF16)
    kd = (k * jnp.exp(-b)).astype(BF16)
    ks = (k * jnp.exp(tot - b)).astype(BF16)
    scores = jnp.where(mask[None], bmm(qd, kd, 2, 2), 0.0).astype(BF16)
    o_intra = bmm(scores, v, 2, 1)
    st_inc = bmm(v, ks, 1, 1)
    chunk_decay = jnp.exp(tot)

    st = st_ref[...]
    nt = (((1,), (1,)), ((), ()))
    for ci in (reversed(range(nc)) if backward else range(nc)):
        o_ref[ci * c:(ci + 1) * c, :] = o_intra[ci] + lax.dot_general(
            qd[ci], st.astype(BF16), nt, preferred_element_type=F32)
        st = st * chunk_decay[ci] + st_inc[ci]
    st_ref[...] = st


def gla_scan(z, lr, up, bias, seq_bounds, dk, dv, backward, tb=512):
    m = z.shape[0]
    h = GLA_HEADS
    nblk = m // tb
    if backward:
        resets = tuple(s // tb - 1 for s in seq_bounds[1:])
        tok = lambda i: nblk - 1 - i
    else:
        resets = tuple(s // tb for s in seq_bounds[:-1])
        tok = lambda i: i
    k_off = (h * dk) // dk
    v_off = (2 * h * dk) // dv
    lr_col = 1 if backward else 0
    kern = functools.partial(_gla_kernel, backward=backward, n_chunks=tb // GLA_CHUNK,
                             reset_blocks=resets, dk=dk)
    return pl.pallas_call(
        kern,
        grid=(h, nblk),
        in_specs=[pl.BlockSpec((tb, dk), lambda hh, i: (tok(i), hh)),
                  pl.BlockSpec((tb, dk), lambda hh, i: (tok(i), k_off + hh)),
                  pl.BlockSpec((tb, dv), lambda hh, i: (tok(i), v_off + hh)),
                  pl.BlockSpec((tb, LANES), lambda hh, i: (tok(i), lr_col)),
                  pl.BlockSpec((LANES, dk), lambda hh, i: (0, hh)),
                  pl.BlockSpec((1, dk), lambda hh, i: (0, hh))],
        out_specs=pl.BlockSpec((tb, dv), lambda hh, i: (tok(i), hh)),
        out_shape=jax.ShapeDtypeStruct((m, h * dv), F32),
        scratch_shapes=[pltpu.VMEM((dv, dk), F32)],
        compiler_params=_cparams(("arbitrary", "arbitrary"), 40 * 1024 * 1024),
        name="gla_scan_bwd" if backward else "gla_scan_fwd",
    )(z, z, z, lr, up, bias)


def _attn_in_kernel(x_ref, w_ref, *refs):
    out_refs, acc_ref = refs[:-1], refs[-1]
    j = pl.program_id(1)
    n_chunks, tm, _ = acc_ref.shape
    res = jnp.dot(x_ref[...], w_ref[...], preferred_element_type=F32)
    for c in range(n_chunks):
        acc_ref[c] = res[:, c * LANES:(c + 1) * LANES]
    for g, (o_ref, d) in enumerate(zip(out_refs, DIL_RATES)):
        @pl.when(jnp.logical_or(j == g, j >= N_DIL_GROUPS))
        def _(o_ref=o_ref, d=d):
            for r in range(d):
                for c in range(n_chunks):
                    rows = pl.ds(r, tm // d, stride=d) if d > 1 else slice(None)
                    o_ref[r, :, c * LANES:(c + 1) * LANES] = (
                        acc_ref[c, rows, :].astype(o_ref.dtype))


def attn_in(xn, w_att, col0, tm=512):
    m, kd = xn.shape
    w = ATTN_KV_W
    nj = N_DIL_GROUPS + 2
    blk0 = col0 // w
    assert blk0 * w == col0
    sect = lambda i, j: (jnp.clip(j - (N_DIL_GROUPS - 1), 0, 2), 0, i, 0)
    return pl.pallas_call(
        _attn_in_kernel,
        grid=(m // tm, nj),
        in_specs=[pl.BlockSpec((tm, kd), lambda i, j: (i, 0)),
                  pl.BlockSpec((kd, w), lambda i, j: (0, blk0 + j))],
        out_specs=[pl.BlockSpec((None, d, tm // d, w), sect) for d in DIL_RATES],
        out_shape=[jax.ShapeDtypeStruct((3, d, m // d, w), BF16) for d in DIL_RATES],
        scratch_shapes=[pltpu.VMEM((w // LANES, tm, LANES), F32)],
        compiler_params=_cparams(("parallel", "arbitrary"),
                                 4 * (tm * kd + kd * w) + 12 * tm * w + 8 * tm * w),
        name="attn_in",
    )(xn, w_att)


def _dil_kernel(q_ref, kp_ref, ko_ref, kn_ref, vp_ref, vo_ref, vn_ref, o_ref, l_ref, *,
                dilation, group, qb, start_tiles, end_tiles):
    t = pl.program_id(0)
    n_res = q_ref.shape[0]
    res0 = pl.program_id(1) * n_res
    has_prev = jnp.logical_not(_any_equal(t, start_tiles))
    has_next = jnp.logical_not(_any_equal(t + 1, end_tiles))
    r = DIL_WINDOWS[group] // (2 * dilation)
    e = ATTN_HEAD_DIM
    qi = lax.broadcasted_iota(jnp.int32, (r, 3 * r), 0)
    kj = lax.broadcasted_iota(jnp.int32, (r, 3 * r), 1)
    rel = jnp.abs(kj - r - qi)
    in_window = rel <= r
    rel_f = rel.astype(F32) * float(dilation)
    prev_ok = jnp.logical_or(kj >= r, has_prev)
    next_ok = jnp.logical_or(kj < 2 * r, has_next)

    def valid(j):
        v = in_window
        if j == 0:
            v = jnp.logical_and(v, prev_ok)
        if j == qb - 1:
            v = jnp.logical_and(v, next_ok)
        return v

    valids = {j: valid(j) for j in {0, qb - 1, min(1, qb - 1)}}
    n_heads = N_DIL_GROUPS * ATTN_KV_HEADS
    scale = e ** -0.5
    nh = ATTN_KV_HEADS

    def heads(ref, ri, rows):
        return jnp.stack([ref[ri, rows, h * e:(h + 1) * e] for h in range(nh)])

    slopes = [2.0 ** (-ALIBI_MAX_BIAS * (group * nh + h + 1) / n_heads) for h in range(nh)]
    bias = jnp.stack([slope * rel_f for slope in slopes])
    every = slice(None)
    qk_dims = (((2,), (2,)), ((0,), (0,)))
    pv_dims = (((2,), (1,)), ((0,), (0,)))
    for ri in range(n_res):
        k_all = jnp.concatenate([heads(kp_ref, ri, every), heads(ko_ref, ri, every),
                                 heads(kn_ref, ri, every)], axis=1)
        v_all = jnp.concatenate([heads(vp_ref, ri, every), heads(vo_ref, ri, every),
                                 heads(vn_ref, ri, every)], axis=1)
        for j in range(qb):
            q = heads(q_ref, ri, slice(j * r, (j + 1) * r))
            s = lax.dot_general(q, k_all[:, j * r:(j + 3) * r], qk_dims,
                                preferred_element_type=F32)
            s = jnp.where(valids.get(j, in_window)[None], s * scale - bias, NEG_INF)
            mx = jnp.max(s, axis=-1, keepdims=True)
            p = jnp.exp(s - mx)
            den = jnp.sum(p, axis=-1, keepdims=True)
            o = lax.dot_general(p.astype(BF16), v_all[:, j * r:(j + 3) * r], pv_dims,
                                preferred_element_type=F32) / den
            lse = jnp.broadcast_to(mx + jnp.log(den), (nh, r, e))
            if dilation == 1:
                dst = pl.ds(j * r, r)
            else:
                dst = pl.ds(j * r * dilation + res0 + ri, r, stride=dilation)
            for h in range(nh):
                o_ref[h, dst, :] = o[h]
                l_ref[h, dst, :] = lse[h]


def dilated_attention_group(a, group, seq_bounds, qb, n_res):
    d = DIL_RATES[group]
    r = DIL_WINDOWS[group] // (2 * d)
    w = ATTN_KV_W
    rows = a.shape[2]
    m = rows * d
    tile = qb * r * d
    n_tiles = m // tile
    n_blocks = rows // r
    starts = tuple(s // tile for s in seq_bounds[:-1])
    ends = tuple(s // tile for s in seq_bounds[1:])

    own = lambda sec: pl.BlockSpec((None, n_res, qb * r, w), lambda t, rr: (sec, rr, t, 0))
    prev = lambda sec: pl.BlockSpec(
        (None, n_res, r, w), lambda t, rr: (sec, rr, jnp.maximum(t * qb - 1, 0), 0))
    nxt = lambda sec: pl.BlockSpec(
        (None, n_res, r, w), lambda t, rr: (sec, rr, jnp.minimum((t + 1) * qb, n_blocks - 1), 0))
    hm = (ATTN_KV_HEADS, m, ATTN_HEAD_DIM)
    out = pl.BlockSpec((ATTN_KV_HEADS, tile, ATTN_HEAD_DIM), lambda t, rr: (0, t, 0))
    kern = functools.partial(_dil_kernel, dilation=d, group=group, qb=qb,
                             start_tiles=starts, end_tiles=ends)
    return pl.pallas_call(
        kern,
        grid=(n_tiles, d // n_res),
        in_specs=[own(0), prev(1), own(1), nxt(1), prev(2), own(2), nxt(2)],
        out_specs=[out, out],
        out_shape=[jax.ShapeDtypeStruct(hm, F32), jax.ShapeDtypeStruct(hm, F32)],
        compiler_params=_cparams(("parallel", "arbitrary"),
                                 16 * tile * w + n_res * (12 * qb + 16) * r * w + 16 * r * w * 4),
        name=f"dilated_attn_g{group}",
    )(a, a, a, a, a, a, a)


def _merge_proj_kernel(of_ref, ob_ref, r_ref, g_ref, o0_ref, o1_ref, o2_ref, l0_ref, l1_ref,
                       l2_ref, gg_ref, ga_ref, pg_ref, pa_ref, o_ref, ogla_buf, oatt_buf, *, dv):
    i = pl.program_id(0)
    j = pl.program_id(1)
    rc = of_ref.shape[0]
    rows = pl.ds(pl.multiple_of(j * rc, rc), rc)

    def branch_outputs():
        o = of_ref[...] + ob_ref[...]
        r = r_ref[...].astype(F32)
        gate = r * jax.nn.sigmoid(r) * g_ref[...]
        for h in range(GLA_HEADS):
            sl = slice(h * dv, (h + 1) * dv)
            oh = o[:, sl]
            ms = jnp.mean(oh * oh, axis=-1, keepdims=True)
            ogla_buf[i % 2, rows, sl] = (oh * lax.rsqrt(ms + NORM_EPS)
                                         * gate[:, sl]).astype(ogla_buf.dtype)
        e = ATTN_HEAD_DIM
        for h in range(ATTN_KV_HEADS):
            l0, l1, l2 = l0_ref[h], l1_ref[h], l2_ref[h]
            mx = jnp.maximum(jnp.maximum(l0, l1), l2)
            w0, w1, w2 = jnp.exp(l0 - mx), jnp.exp(l1 - mx), jnp.exp(l2 - mx)
            num = w0 * o0_ref[h] + w1 * o1_ref[h] + w2 * o2_ref[h]
            oatt_buf[i % 2, rows, h * e:(h + 1) * e] = (num / (w0 + w1 + w2)).astype(oatt_buf.dtype)

    @pl.when(i == 0)
    def _():
        o_ref[...] = jnp.zeros_like(o_ref)
        branch_outputs()

    @pl.when(i > 0)
    def _():
        yg = jnp.dot(ogla_buf[(i + 1) % 2], pg_ref[...], preferred_element_type=F32)
        ya = jnp.dot(oatt_buf[(i + 1) % 2], pa_ref[...], preferred_element_type=F32)
        gg = jax.nn.sigmoid(gg_ref[...].astype(F32))
        ga = jax.nn.sigmoid(ga_ref[...].astype(F32))
        o_ref[...] = (gg * yg + ga * ya).astype(o_ref.dtype)
        branch_outputs()


def merge_proj(o_f, o_b, z, r_col, gla_norm_g, outs, lses, dv, gate_start, proj_gla, proj_attn,
               tm=1024, tn=512):
    m = z.shape[0]
    vw = GLA_HEADS * dv
    kvw = ATTN_KV_W
    d = proj_gla.shape[1]
    nj = d // tn
    rc = tm // nj
    assert rc * nj == tm and rc % 16 == 0
    n_row_blocks = m // tm
    gate_col = gate_start // tn
    chunk = lambda i, j: jnp.minimum(i * nj + j, n_row_blocks * nj - 1)
    prev = lambda i: jnp.maximum(i - 1, 0)
    row = lambda w, col=0: pl.BlockSpec((rc, w), lambda i, j: (chunk(i, j), col))
    heads = pl.BlockSpec((ATTN_KV_HEADS, rc, ATTN_HEAD_DIM), lambda i, j: (0, chunk(i, j), 0))
    return pl.pallas_call(
        functools.partial(_merge_proj_kernel, dv=dv),
        grid=(n_row_blocks + 1, nj),
        in_specs=[row(vw), row(vw), row(vw, r_col),
                  pl.BlockSpec((1, vw), lambda i, j: (0, 0))] + [heads] * 6
        + [pl.BlockSpec((tm, tn), lambda i, j: (prev(i), gate_col + j)),
           pl.BlockSpec((tm, tn), lambda i, j: (prev(i), gate_col + nj + j)),
           pl.BlockSpec((vw, tn), lambda i, j: (0, j)),
           pl.BlockSpec((kvw, tn), lambda i, j: (0, j))],
        out_specs=pl.BlockSpec((tm, tn), lambda i, j: (i, j)),
        out_shape=jax.ShapeDtypeStruct((m + tm, d), BF16),
        scratch_shapes=[pltpu.VMEM((2, tm, vw), BF16), pltpu.VMEM((2, tm, kvw), BF16)],
        compiler_params=_cparams(("arbitrary", "arbitrary"),
                                 4 * tm * (vw + kvw) + 4 * tn * (vw + kvw) + 12 * tm * tn
                                 + 16 * tm * tn + rc * (20 * vw + 48 * kvw) + 16 * rc * vw),
        name="merge_proj",
    )(o_f, o_b, z, gla_norm_g.reshape(1, vw), *outs, *lses, z, z, proj_gla, proj_attn)


def _cast_kernel(w_ref, o_ref):
    o_ref[...] = w_ref[...].astype(o_ref.dtype)


def cast_layer(w_stack, layer, tr=128):
    _, rows, cols = w_stack.shape
    return pl.pallas_call(
        _cast_kernel,
        grid=(rows // tr,),
        in_specs=[pl.BlockSpec((None, tr, cols), lambda i: (layer, i, 0))],
        out_specs=pl.BlockSpec((tr, cols), lambda i: (i, 0)),
        out_shape=jax.ShapeDtypeStruct((rows, cols), BF16),
        compiler_params=_cparams(("parallel",), 16 * tr * cols),
        name="cast_layer",
    )(w_stack)


def _pad_to(x, axis, size):
    pad = [(0, 0)] * x.ndim
    pad[axis] = (0, size - x.shape[axis])
    return jnp.pad(x, pad)


def kernel(x_prompt, x_sample, ffn1_pre_g, ffn1_w_gate, ffn1_w_up, ffn1_w_down, ffn1_post_g,
           mix_pre_g, w_in, gla_decay_up_fwd, gla_decay_bias_fwd, gla_decay_up_bwd,
           gla_decay_bias_bwd, gla_norm_g, proj_gla, proj_attn, w_out, mix_post_g,
           ffn2_pre_g, ffn2_w_gate, ffn2_w_up, ffn2_w_down, ffn2_post_g):
    depth, d, d_ff = ffn1_w_gate.shape
    bp, tp, _ = x_prompt.shape
    bs, ts, _ = x_sample.shape
    mp, ms = bp * tp, bs * ts
    seq_bounds = tuple(b * tp for b in range(bp)) + tuple(mp + b * ts for b in range(bs + 1))

    dk = d // 16
    dv = d // 8
    qk_w = GLA_HEADS * dk
    v_w = GLA_HEADS * dv
    q_w = N_DIL_GROUPS * ATTN_KV_W
    lr0 = 2 * qk_w + 2 * v_w
    a0 = lr0 + 2 * GLA_RANK
    g0 = a0 + q_w + 2 * ATTN_KV_W
    ff_tk = -(-d_ff // (4 * 256)) * 256

    x = (x_prompt.reshape(mp, d), x_sample.reshape(ms, d))
    xn = rmsnorm(x, ffn1_pre_g[0])

    m_blocks = (mp + ms) // 1024
    up_steps = m_blocks * pl.cdiv(d_ff, 512)
    down_steps = m_blocks * (d // 1024) * pl.cdiv(d_ff, ff_tk)
    mixer_in_steps = m_blocks * ((lr0 + 2 * d) // 1024)

    wg1, wu1 = cast_layer(ffn1_w_gate, 0), cast_layer(ffn1_w_up, 0)
    pending = None
    for l in range(depth):
        wd_ride = (Ride(ffn1_w_down, l, up_steps),)
        if pending is None:
            hmid, wd1 = ffn_up(xn, wg1, wu1, rides=wd_ride)
        else:
            y_prev, g_prev, scale_prev = pending
            hmid, x, wd1 = ffn_up_prenorm(y_prev, x, g_prev, ffn1_pre_g[l], scale_prev,
                                          wg1, wu1, rides=wd_ride)
        y, = ffn_down(hmid, wd1, ff_tk, skip_blocks=0 if pending is None else 1)
        x, xn = norm_residual(y, x, ffn1_post_g[l], mix_pre_g[l], 0.5)

        w_all = jnp.concatenate(
            [w_in[l][:, :lr0], w_in[l][:, a0:],
             _pad_to(w_in[l][:, lr0:lr0 + GLA_RANK], 1, LANES),
             _pad_to(w_in[l][:, lr0 + GLA_RANK:a0], 1, LANES)], axis=1).astype(BF16)
        att_tiles = (g0 - a0) // 1024
        z, wg2, wu2, pg, pa, wo = matmul(
            xn, w_all, BF16, name="mixer_in",
            w_cols=(lr0 + 2 * d, lambda j: jnp.where(j < lr0 // 1024, j, j + att_tiles)),
            rides=(Ride(ffn2_w_gate, l, mixer_in_steps), Ride(ffn2_w_up, l, mixer_in_steps),
                   Ride(proj_gla, l, mixer_in_steps), Ride(proj_attn, l, mixer_in_steps),
                   Ride(w_out, l, mixer_in_steps)))
        lr_blk = (w_all.shape[1] - 2 * LANES) // (2 * LANES)
        lr = matmul(xn, w_all, F32, tn=2 * LANES, w_cols=(2 * LANES, lambda j: lr_blk),
                    name="mixer_in_lr")
        o_f = gla_scan(z, lr, _pad_to(gla_decay_up_fwd[l], 0, LANES),
                       gla_decay_bias_fwd[l].reshape(1, qk_w), seq_bounds, dk, dv, False)
        o_b = gla_scan(z, lr, _pad_to(gla_decay_up_bwd[l], 0, LANES),
                       gla_decay_bias_bwd[l].reshape(1, qk_w), seq_bounds, dk, dv, True)

        qkv = attn_in(xn, w_all, lr0)
        outs, lses = [], []
        for g, (qb, n_res) in enumerate(((8, 1), (4, 4), (1, 4))):
            o_g, l_g = dilated_attention_group(qkv[g], g, seq_bounds, qb, n_res)
            outs.append(o_g)
            lses.append(l_g)

        merged = merge_proj(o_f, o_b, z, (2 * qk_w + v_w) // v_w, gla_norm_g[l], outs, lses, dv,
                            lr0, pg, pa)
        y = matmul(merged, wo, BF16, skip_blocks=1, name="mixer_out")

        hmid, x, wd2 = ffn_up_prenorm(y, x, mix_post_g[l], ffn2_pre_g[l], 1.0, wg2, wu2,
                                      rides=(Ride(ffn2_w_down, l, up_steps),))
        if l + 1 == depth:
            y, = ffn_down(hmid, wd2, ff_tk, skip_blocks=1)
        else:
            y, wg1, wu1 = ffn_down(hmid, wd2, ff_tk, skip_blocks=1,
                                   rides=(Ride(ffn1_w_gate, l + 1, down_steps),
                                          Ride(ffn1_w_up, l + 1, down_steps)))
        pending = (y, ffn2_post_g[l], 0.5)

    y, g_post, scale = pending
    x, _ = norm_residual(y, x, g_post, None, scale, out_rows=(mp, ms))
    return (x[0].reshape(bp, tp, d), x[1].reshape(bs, ts, d))
```

```python
import functools

import jax
import jax.numpy as jnp
from jax import lax
from jax.experimental import pallas as pl
from jax.experimental.pallas import tpu as pltpu

F32 = jnp.float32
BF16 = jnp.bfloat16

NORM_EPS = 1e-6
NEG_INF = -1e30

GLA_HEADS = 4
GLA_RANK = 16
GLA_TAU = 16.0
GLA_CHUNK = 64
ATTN_KV_HEADS = 8
ATTN_HEAD_DIM = 128
ATTN_KV_W = ATTN_KV_HEADS * ATTN_HEAD_DIM
DIL_WINDOWS = (128, 512, 2048)
DIL_RATES = (1, 4, 16)
N_DIL_GROUPS = 3
ALIBI_MAX_BIAS = 8.0

LANES = 128
VMEM_LIMIT_CAP = 60 * 1024 * 1024


def _cparams(dims, vmem_bytes):
    return pltpu.CompilerParams(dimension_semantics=dims,
                                vmem_limit_bytes=int(min(vmem_bytes * 5 // 4, VMEM_LIMIT_CAP)))


def _any_equal(idx, values):
    return functools.reduce(jnp.logical_or, [idx == v for v in values])


def _part_specs(n_first, tr, d):
    return [pl.BlockSpec((tr, d), lambda i: (jnp.minimum(i, n_first - 1), 0)),
            pl.BlockSpec((tr, d), lambda i: (jnp.maximum(i - n_first, 0), 0))]


def _rmsnorm_kernel(xa_ref, xb_ref, g_ref, o_ref, *, n_first):
    x = jnp.where(pl.program_id(0) < n_first, xa_ref[...], xb_ref[...])
    ms = jnp.mean(x * x, axis=-1, keepdims=True)
    o_ref[...] = (x * lax.rsqrt(ms + NORM_EPS) * g_ref[...]).astype(o_ref.dtype)


def rmsnorm(x_parts, g, tr=256):
    d = x_parts[0].shape[1]
    n_first = x_parts[0].shape[0] // tr
    m = x_parts[0].shape[0] + x_parts[1].shape[0]
    return pl.pallas_call(
        functools.partial(_rmsnorm_kernel, n_first=n_first),
        grid=(m // tr,),
        in_specs=_part_specs(n_first, tr, d) + [pl.BlockSpec((1, d), lambda i: (0, 0))],
        out_specs=pl.BlockSpec((tr, d), lambda i: (i, 0)),
        out_shape=jax.ShapeDtypeStruct((m, d), BF16),
        compiler_params=_cparams(("arbitrary",), 10 * tr * d * 4),
        name="rmsnorm",
    )(*x_parts, g.reshape(1, d))


def _norm_residual_kernel(y_ref, *refs, scale, with_next, n_in_first, n_out_first):
    i = pl.program_id(0)
    refs = list(refs)
    if n_in_first is None:
        x = refs.pop(0)[...]
    else:
        xa_ref, xb_ref = refs.pop(0), refs.pop(0)
        x = jnp.where(i < n_in_first, xa_ref[...], xb_ref[...])
    gp_ref = refs.pop(0)
    y = y_ref[...].astype(F32)
    ms = jnp.mean(y * y, axis=-1, keepdims=True)
    xo = x + scale * (y * lax.rsqrt(ms + NORM_EPS) * gp_ref[...])
    if with_next:
        gn_ref, xn_ref = refs.pop(0), refs.pop()
        ms2 = jnp.mean(xo * xo, axis=-1, keepdims=True)
        xn_ref[...] = (xo * lax.rsqrt(ms2 + NORM_EPS) * gn_ref[...]).astype(xn_ref.dtype)
    if n_out_first is None:
        refs[0][...] = xo
    else:
        @pl.when(i < n_out_first)
        def _():
            refs[0][...] = xo

        @pl.when(i >= n_out_first)
        def _():
            refs[1][...] = xo


def norm_residual(y, x, g_post, g_next, scale, out_rows=None, tr=256):
    m, d = y.shape
    with_next = g_next is not None
    row = pl.BlockSpec((tr, d), lambda i: (i, 0))
    vec = pl.BlockSpec((1, d), lambda i: (0, 0))
    gains = [g_post.reshape(1, d)] + ([g_next.reshape(1, d)] if with_next else [])
    x_parts = x if isinstance(x, (tuple, list)) else None
    n_in_first = None if x_parts is None else x_parts[0].shape[0] // tr
    n_out_first = None if out_rows is None else out_rows[0] // tr
    x_specs = [row] if x_parts is None else _part_specs(n_in_first, tr, d)
    if out_rows is None:
        xo_specs, xo_shapes = [row], [jax.ShapeDtypeStruct((m, d), F32)]
    else:
        xo_specs = _part_specs(n_out_first, tr, d)
        xo_shapes = [jax.ShapeDtypeStruct((rows, d), F32) for rows in out_rows]
    out = pl.pallas_call(
        functools.partial(_norm_residual_kernel, scale=scale, with_next=with_next,
                          n_in_first=n_in_first, n_out_first=n_out_first),
        grid=(m // tr,),
        in_specs=[row] + x_specs + [vec] * len(gains),
        out_specs=xo_specs + ([row] if with_next else []),
        out_shape=xo_shapes + ([jax.ShapeDtypeStruct((m, d), BF16)] if with_next else []),
        compiler_params=_cparams(("arbitrary",), 14 * tr * d * 4),
        name="norm_residual",
    )(y, *(x_parts if x_parts is not None else [x]), *gains)
    xo = out[0] if out_rows is None else tuple(out[:2])
    return xo, (out[-1] if with_next else None)


class Ride:
    def __init__(self, w_stack, layer, max_blocks, min_tr=16):
        _, self.rows, self.cols = w_stack.shape
        tr = min_tr
        while self.rows % tr or self.rows // tr > max_blocks:
            tr *= 2
            assert tr <= self.rows
        self.w_stack, self.layer, self.tr = w_stack, layer, tr
        self.n_blocks = self.rows // tr

    def specs(self, step_of):
        blk = lambda *g: jnp.minimum(step_of(*g), self.n_blocks - 1)
        src = pl.BlockSpec((None, self.tr, self.cols), lambda *g: (self.layer, blk(*g), 0))
        dst = pl.BlockSpec((self.tr, self.cols), lambda *g: (blk(*g), 0))
        return src, dst

    @property
    def out_shape(self):
        return jax.ShapeDtypeStruct((self.rows, self.cols), BF16)

    @property
    def vmem_bytes(self):
        return 12 * self.tr * self.cols


def _split_rides(refs, n_base_in, n_base_out, n_rides):
    a = n_base_in
    b = a + n_rides
    c = b + n_base_out
    d = c + n_rides
    return refs[:a], refs[a:b], refs[b:c], refs[c:d], refs[d:]


def _do_rides(step, rides_n_blocks, srcs, dsts):
    for n_blocks, src, dst in zip(rides_n_blocks, srcs, dsts):
        @pl.when(step < n_blocks)
        def _(src=src, dst=dst):
            dst[...] = src[...].astype(dst.dtype)


def _mm_kernel(*refs, ride_blocks):
    (a_ref, w_ref), srcs, (o_ref,), dsts, _ = _split_rides(refs, 2, 1, len(ride_blocks))
    o_ref[...] = jnp.dot(a_ref[...], w_ref[...],
                         preferred_element_type=F32).astype(o_ref.dtype)
    _do_rides(pl.program_id(0) * pl.num_programs(1) + pl.program_id(1), ride_blocks, srcs, dsts)


def matmul(a, w, out_dtype, tm=1024, tn=1024, w_cols=None, skip_blocks=0, rides=(),
           name="matmul"):
    m, kd = a.shape
    m -= skip_blocks * tm
    n, col_of = w_cols if w_cols is not None else (w.shape[1], lambda j: j)
    tn = min(tn, n)
    nj = n // tn
    osz = jnp.dtype(out_dtype).itemsize
    assert all(r.n_blocks <= (m // tm) * nj for r in rides)
    ride_specs = [r.specs(lambda i, j: i * nj + j) for r in rides]
    out = pl.pallas_call(
        functools.partial(_mm_kernel, ride_blocks=tuple(r.n_blocks for r in rides)),
        grid=(m // tm, nj),
        in_specs=[pl.BlockSpec((tm, kd), lambda i, j: (i + skip_blocks, 0)),
                  pl.BlockSpec((kd, tn), lambda i, j: (0, col_of(j)))] + [s for s, _ in ride_specs],
        out_specs=[pl.BlockSpec((tm, tn), lambda i, j: (i, j))] + [d for _, d in ride_specs],
        out_shape=[jax.ShapeDtypeStruct((m, n), out_dtype)] + [r.out_shape for r in rides],
        compiler_params=_cparams(("arbitrary", "arbitrary"),
                                 4 * (tm * kd + kd * tn) + 2 * tm * tn * osz + 3 * tm * tn * 4
                                 + sum(r.vmem_bytes for r in rides)),
        name=name,
    )(a, w, *[r.w_stack for r in rides])
    return out[0] if not rides else out


def _ffn_up_kernel(*refs, n_tail, ride_blocks):
    (x_ref, wg_ref, wu_ref), srcs, (o_ref,), dsts, _ = _split_rides(refs, 3, 1, len(ride_blocks))
    j = pl.program_id(1)
    tn = o_ref.shape[1]
    _do_rides(pl.program_id(0) * pl.num_programs(1) + j, ride_blocks, srcs, dsts)

    def tile(width):
        x = x_ref[...]
        g = jnp.dot(x, wg_ref[:, :width], preferred_element_type=F32)
        u = jnp.dot(x, wu_ref[:, :width], preferred_element_type=F32)
        o_ref[:, :width] = (g * jax.nn.sigmoid(g) * u).astype(o_ref.dtype)

    if n_tail == tn:
        tile(tn)
    else:
        pl.when(j == 0)(lambda: tile(n_tail))
        pl.when(j > 0)(lambda: tile(tn))


def ffn_up(xn, wg, wu, tm=1024, tn=512, rides=()):
    m, d = xn.shape
    f = wg.shape[1]
    nj = pl.cdiv(f, tn)
    col = lambda j: (j + nj - 1) % nj
    assert all(r.n_blocks <= (m // tm) * nj for r in rides)
    ride_specs = [r.specs(lambda i, j: i * nj + j) for r in rides]
    out = pl.pallas_call(
        functools.partial(_ffn_up_kernel, n_tail=f - (nj - 1) * tn,
                          ride_blocks=tuple(r.n_blocks for r in rides)),
        grid=(m // tm, nj),
        in_specs=[pl.BlockSpec((tm, d), lambda i, j: (i, 0)),
                  pl.BlockSpec((d, tn), lambda i, j: (0, col(j))),
                  pl.BlockSpec((d, tn), lambda i, j: (0, col(j)))] + [s for s, _ in ride_specs],
        out_specs=[pl.BlockSpec((tm, tn), lambda i, j: (i, col(j)))] + [d for _, d in ride_specs],
        out_shape=[jax.ShapeDtypeStruct((m, f), BF16)] + [r.out_shape for r in rides],
        compiler_params=_cparams(("arbitrary", "arbitrary"),
                                 4 * tm * d + 8 * d * tn + 4 * tm * tn + 16 * tm * tn
                                 + sum(r.vmem_bytes for r in rides)),
        name="ffn_up",
    )(xn, wg, wu, *[r.w_stack for r in rides])
    return out


def _ffn_up_prenorm_kernel(*refs, n_tail, ride_blocks, scale, n_chunks, n_row_blocks):
    ((y_ref, x_ref, gp_ref, gn_ref, wg_ref, wu_ref), srcs, (o_ref, xo_ref), dsts,
     (xn_buf,)) = _split_rides(refs, 6, 2, len(ride_blocks))
    i = pl.program_id(0)
    j = pl.program_id(1)
    tn = o_ref.shape[1]
    rc = y_ref.shape[0]
    _do_rides(i * pl.num_programs(1) + j, ride_blocks, srcs, dsts)

    def prenorm():
        c = jnp.minimum(j, n_chunks - 1)
        y = y_ref[...].astype(F32)
        ms = jnp.mean(y * y, axis=-1, keepdims=True)
        xo = x_ref[...] + scale * (y * lax.rsqrt(ms + NORM_EPS) * gp_ref[...])
        xo_ref[...] = xo
        ms2 = jnp.mean(xo * xo, axis=-1, keepdims=True)
        xn_buf[i % 2, pl.ds(pl.multiple_of(c * rc, rc), rc), :] = (
            xo * lax.rsqrt(ms2 + NORM_EPS) * gn_ref[...]).astype(xn_buf.dtype)

    def tile(width):
        x = xn_buf[(i + 1) % 2]
        g = jnp.dot(x, wg_ref[:, :width], preferred_element_type=F32)
        u = jnp.dot(x, wu_ref[:, :width], preferred_element_type=F32)
        o_ref[:, :width] = (g * jax.nn.sigmoid(g) * u).astype(o_ref.dtype)

    @pl.when(i == 0)
    def _():
        o_ref[...] = jnp.zeros_like(o_ref)
        prenorm()

    @pl.when(jnp.logical_and(i > 0, j == 0))
    def _():
        tile(n_tail)
        prenorm()

    @pl.when(jnp.logical_and(i > 0, jnp.logical_and(j > 0, j < n_chunks)))
    def _():
        tile(tn)
        prenorm()

    @pl.when(jnp.logical_and(i > 0, j >= n_chunks))
    def _():
        tile(tn)


def ffn_up_prenorm(y, x, g_post, g_next, scale, wg, wu, tm=1024, tn=512, rides=()):
    m, d = y.shape
    f = wg.shape[1]
    nj = pl.cdiv(f, tn)
    n_row_blocks = m // tm
    n_chunks = 1
    while n_chunks * 2 <= min(nj, 16):
        n_chunks *= 2
    rc = tm // n_chunks
    col = lambda j: (j + nj - 1) % nj
    chunk = lambda i, j: (jnp.minimum(i * n_chunks + jnp.minimum(j, n_chunks - 1),
                                      n_row_blocks * n_chunks - 1), 0)
    assert all(r.n_blocks <= (n_row_blocks + 1) * nj for r in rides)
    ride_specs = [r.specs(lambda i, j: i * nj + j) for r in rides]
    vec = pl.BlockSpec((1, d), lambda i, j: (0, 0))
    out = pl.pallas_call(
        functools.partial(_ffn_up_prenorm_kernel, n_tail=f - (nj - 1) * tn,
                          ride_blocks=tuple(r.n_blocks for r in rides), scale=scale,
                          n_chunks=n_chunks, n_row_blocks=n_row_blocks),
        grid=(n_row_blocks + 1, nj),
        in_specs=[pl.BlockSpec((rc, d), chunk), pl.BlockSpec((rc, d), chunk), vec, vec,
                  pl.BlockSpec((d, tn), lambda i, j: (0, col(j))),
                  pl.BlockSpec((d, tn), lambda i, j: (0, col(j)))] + [s for s, _ in ride_specs],
        out_specs=[pl.BlockSpec((tm, tn), lambda i, j: (i, col(j))),
                   pl.BlockSpec((rc, d), chunk)] + [dst for _, dst in ride_specs],
        out_shape=[jax.ShapeDtypeStruct((m + tm, f), BF16), jax.ShapeDtypeStruct((m, d), F32)]
        + [r.out_shape for r in rides],
        scratch_shapes=[pltpu.VMEM((2, tm, d), BF16)],
        compiler_params=_cparams(("arbitrary", "arbitrary"),
                                 4 * tm * d + 8 * d * tn + 4 * tm * tn + 16 * tm * tn
                                 + 20 * rc * d + 16 * rc * d + sum(r.vmem_bytes for r in rides)),
        name="ffn_up_prenorm",
    )(y, x, g_post.reshape(1, d), g_next.reshape(1, d), wg, wu, *[r.w_stack for r in rides])
    return out


def _ffn_down_kernel(*refs, k_tail, ride_blocks):
    (a_ref, w_ref), srcs, (o_ref,), dsts, (acc_ref,) = _split_rides(refs, 2, 1, len(ride_blocks))
    k = pl.program_id(2)
    last = pl.num_programs(2) - 1
    tk = a_ref.shape[1]
    step = (pl.program_id(0) * pl.num_programs(1) + pl.program_id(1)) * pl.num_programs(2) + k
    _do_rides(step, ride_blocks, srcs, dsts)

    @pl.when(k == 0)
    def _():
        acc_ref[...] = jnp.zeros_like(acc_ref)

    @pl.when(k < last)
    def _():
        acc_ref[...] += jnp.dot(a_ref[...], w_ref[...], preferred_element_type=F32)

    @pl.when(k == last)
    def _():
        if k_tail == tk:
            part = jnp.dot(a_ref[...], w_ref[...], preferred_element_type=F32)
        else:
            part = jnp.dot(a_ref[:, :k_tail], w_ref[:k_tail, :], preferred_element_type=F32)
        o_ref[...] = (acc_ref[...] + part).astype(o_ref.dtype)


def ffn_down(h, wd, tk, tm=1024, tn=1024, skip_blocks=0, rides=()):
    m, kd = h.shape
    m -= skip_blocks * tm
    n = wd.shape[1]
    nj = n // tn
    nk = pl.cdiv(kd, tk)
    assert all(r.n_blocks <= (m // tm) * nj * nk for r in rides)
    ride_specs = [r.specs(lambda i, j, k: (i * nj + j) * nk + k) for r in rides]
    out = pl.pallas_call(
        functools.partial(_ffn_down_kernel, k_tail=kd - (nk - 1) * tk,
                          ride_blocks=tuple(r.n_blocks for r in rides)),
        grid=(m // tm, nj, nk),
        in_specs=[pl.BlockSpec((tm, tk), lambda i, j, k: (i + skip_blocks, k)),
                  pl.BlockSpec((tk, tn), lambda i, j, k: (k, j))] + [s for s, _ in ride_specs],
        out_specs=[pl.BlockSpec((tm, tn), lambda i, j, k: (i, j))] + [d for _, d in ride_specs],
        out_shape=[jax.ShapeDtypeStruct((m, n), BF16)] + [r.out_shape for r in rides],
        scratch_shapes=[pltpu.VMEM((tm, tn), F32)],
        compiler_params=_cparams(("arbitrary", "arbitrary", "arbitrary"),
                                 4 * (tm * tk + tk * tn) + 4 * tm * tn + 16 * tm * tn
                                 + sum(r.vmem_bytes for r in rides)),
        name="ffn_down",
    )(h, wd, *[r.w_stack for r in rides])
    return out


def _gla_kernel(q_ref, k_ref, v_ref, lr_ref, up_ref, bias_ref, o_ref, st_ref, *,
                backward, n_chunks, reset_blocks, dk):
    i = pl.program_id(1)
    blk = pl.num_programs(1) - 1 - i if backward else i

    @pl.when(_any_equal(blk, reset_blocks))
    def _():
        st_ref[...] = jnp.zeros_like(st_ref)

    c = GLA_CHUNK
    nc = n_chunks
    row = lax.broadcasted_iota(jnp.int32, (c, c), 0)
    col = lax.broadcasted_iota(jnp.int32, (c, c), 1)
    if backward:
        tri, mask = row <= col, row < col
    else:
        tri, mask = row >= col, row >= col
    tri = jnp.broadcast_to(tri.astype(BF16)[None], (nc, c, c))

    def bmm(a, b, ca, cb):
        return lax.dot_general(a, b, (((ca,), (cb,)), ((0,), (0,))), preferred_element_type=F32)

    x = jnp.dot(lr_ref[...].astype(BF16), up_ref[...].astype(BF16),
                preferred_element_type=F32) + bias_ref[...]
    la = (jnp.minimum(x, 0.0) - jnp.log1p(jnp.exp(-jnp.abs(x)))) * (1.0 / GLA_TAU)
    la = la.reshape(nc, c, dk)
    la_hi = la.astype(BF16)
    la_lo = (la - la_hi.astype(F32)).astype(BF16)
    b = bmm(tri, la_hi, 2, 1) + bmm(tri, la_lo, 2, 1)
    tot = jnp.sum(la, axis=1, keepdims=True)
    q = q_ref[...].astype(F32).reshape(nc, c, dk)
    k = k_ref[...].astype(F32).reshape(nc, c, dk)
    v = v_ref[...].reshape(nc, c, v_ref.shape[1])
    qd = (q * (dk ** -0.5) * jnp.exp(b)).astype(BF16)
    kd = (k * jnp.exp(-b)).astype(BF16)
    ks = (k * jnp.exp(tot - b)).astype(BF16)
    scores = jnp.where(mask[None], bmm(qd, kd, 2, 2), 0.0).astype(BF16)
    o_intra = bmm(scores, v, 2, 1)
    st_inc = bmm(v, ks, 1, 1)
    chunk_decay = jnp.exp(tot)

    st = st_ref[...]
    nt = (((1,), (1,)), ((), ()))
    for ci in (reversed(range(nc)) if backward else range(nc)):
        o_ref[ci * c:(ci + 1) * c, :] = o_intra[ci] + lax.dot_general(
            qd[ci], st.astype(BF16), nt, preferred_element_type=F32)
        st = st * chunk_decay[ci] + st_inc[ci]
    st_ref[...] = st


def gla_scan(z, lr, up, bias, seq_bounds, dk, dv, backward, tb=1024):
    m = z.shape[0]
    h = GLA_HEADS
    nblk = m // tb
    if backward:
        resets = tuple(s // tb - 1 for s in seq_bounds[1:])
        tok = lambda i: nblk - 1 - i
    else:
        resets = tuple(s // tb for s in seq_bounds[:-1])
        tok = lambda i: i
    k_off = (h * dk) // dk
    v_off = (2 * h * dk) // dv
    lr_col = 1 if backward else 0
    kern = functools.partial(_gla_kernel, backward=backward, n_chunks=tb // GLA_CHUNK,
                             reset_blocks=resets, dk=dk)
    return pl.pallas_call(
        kern,
        grid=(h, nblk),
        in_specs=[pl.BlockSpec((tb, dk), lambda hh, i: (tok(i), hh)),
                  pl.BlockSpec((tb, dk), lambda hh, i: (tok(i), k_off + hh)),
                  pl.BlockSpec((tb, dv), lambda hh, i: (tok(i), v_off + hh)),
                  pl.BlockSpec((tb, LANES), lambda hh, i: (tok(i), lr_col)),
                  pl.BlockSpec((LANES, dk), lambda hh, i: (0, hh)),
                  pl.BlockSpec((1, dk), lambda hh, i: (0, hh))],
        out_specs=pl.BlockSpec((tb, dv), lambda hh, i: (tok(i), hh)),
        out_shape=jax.ShapeDtypeStruct((m, h * dv), F32),
        scratch_shapes=[pltpu.VMEM((dv, dk), F32)],
        compiler_params=_cparams(("arbitrary", "arbitrary"), 40 * 1024 * 1024),
        name="gla_scan_bwd" if backward else "gla_scan_fwd",
    )(z, z, z, lr, up, bias)


def _attn_in_kernel(x_ref, w_ref, *refs):
    out_refs, acc_ref = refs[:-1], refs[-1]
    j = pl.program_id(1)
    n_chunks, tm, _ = acc_ref.shape
    res = jnp.dot(x_ref[...], w_ref[...], preferred_element_type=F32)
    for c in range(n_chunks):
        acc_ref[c] = res[:, c * LANES:(c + 1) * LANES]
    for g, (o_ref, d) in enumerate(zip(out_refs, DIL_RATES)):
        @pl.when(jnp.logical_or(j == g, j >= N_DIL_GROUPS))
        def _(o_ref=o_ref, d=d):
            for r in range(d):
                for c in range(n_chunks):
                    rows = pl.ds(r, tm // d, stride=d) if d > 1 else slice(None)
                    o_ref[r, :, c * LANES:(c + 1) * LANES] = (
                        acc_ref[c, rows, :].astype(o_ref.dtype))


def attn_in(xn, w_att, col0, tm=512):
    m, kd = xn.shape
    w = ATTN_KV_W
    nj = N_DIL_GROUPS + 2
    blk0 = col0 // w
    assert blk0 * w == col0
    sect = lambda i, j: (jnp.clip(j - (N_DIL_GROUPS - 1), 0, 2), 0, i, 0)
    return pl.pallas_call(
        _attn_in_kernel,
        grid=(m // tm, nj),
        in_specs=[pl.BlockSpec((tm, kd), lambda i, j: (i, 0)),
                  pl.BlockSpec((kd, w), lambda i, j: (0, blk0 + j))],
        out_specs=[pl.BlockSpec((None, d, tm // d, w), sect) for d in DIL_RATES],
        out_shape=[jax.ShapeDtypeStruct((3, d, m // d, w), BF16) for d in DIL_RATES],
        scratch_shapes=[pltpu.VMEM((w // LANES, tm, LANES), F32)],
        compiler_params=_cparams(("parallel", "arbitrary"),
                                 4 * (tm * kd + kd * w) + 12 * tm * w + 8 * tm * w),
        name="attn_in",
    )(xn, w_att)


def _dil_kernel(q_ref, kp_ref, ko_ref, kn_ref, vp_ref, vo_ref, vn_ref, o_ref, l_ref, *,
                dilation, group, qb, start_tiles, end_tiles):
    t = pl.program_id(0)
    n_res = q_ref.shape[0]
    res0 = pl.program_id(1) * n_res
    has_prev = jnp.logical_not(_any_equal(t, start_tiles))
    has_next = jnp.logical_not(_any_equal(t + 1, end_tiles))
    r = DIL_WINDOWS[group] // (2 * dilation)
    e = ATTN_HEAD_DIM
    qi = lax.broadcasted_iota(jnp.int32, (r, 3 * r), 0)
    kj = lax.broadcasted_iota(jnp.int32, (r, 3 * r), 1)
    rel = jnp.abs(kj - r - qi)
    in_window = rel <= r
    rel_f = rel.astype(F32) * float(dilation)
    prev_ok = jnp.logical_or(kj >= r, has_prev)
    next_ok = jnp.logical_or(kj < 2 * r, has_next)

    def valid(j):
        v = in_window
        if j == 0:
            v = jnp.logical_and(v, prev_ok)
        if j == qb - 1:
            v = jnp.logical_and(v, next_ok)
        return v

    valids = {j: valid(j) for j in {0, qb - 1, min(1, qb - 1)}}
    n_heads = N_DIL_GROUPS * ATTN_KV_HEADS
    scale = e ** -0.5
    nh = ATTN_KV_HEADS

    def heads(ref, ri, rows):
        return jnp.stack([ref[ri, rows, h * e:(h + 1) * e] for h in range(nh)])

    slopes = [2.0 ** (-ALIBI_MAX_BIAS * (group * nh + h + 1) / n_heads) for h in range(nh)]
    bias = jnp.stack([slope * rel_f for slope in slopes])
    every = slice(None)
    qk_dims = (((2,), (2,)), ((0,), (0,)))
    pv_dims = (((2,), (1,)), ((0,), (0,)))
    for ri in range(n_res):
        k_all = jnp.concatenate([heads(kp_ref, ri, every), heads(ko_ref, ri, every),
                                 heads(kn_ref, ri, every)], axis=1)
        v_all = jnp.concatenate([heads(vp_ref, ri, every), heads(vo_ref, ri, every),
                                 heads(vn_ref, ri, every)], axis=1)
        for j in range(qb):
            q = heads(q_ref, ri, slice(j * r, (j + 1) * r))
            s = lax.dot_general(q, k_all[:, j * r:(j + 3) * r], qk_dims,
                                preferred_element_type=F32)
            s = jnp.where(valids.get(j, in_window)[None], s * scale - bias, NEG_INF)
            mx = jnp.max(s, axis=-1, keepdims=True)
            p = jnp.exp(s - mx)
            den = jnp.sum(p, axis=-1, keepdims=True)
            o = lax.dot_general(p.astype(BF16), v_all[:, j * r:(j + 3) * r], pv_dims,
                                preferred_element_type=F32) / den
            lse = jnp.broadcast_to(mx + jnp.log(den), (nh, r, e))
            if dilation == 1:
                dst = pl.ds(j * r, r)
            else:
                dst = pl.ds(j * r * dilation + res0 + ri, r, stride=dilation)
            for h in range(nh):
                o_ref[h, dst, :] = o[h]
                l_ref[h, dst, :] = lse[h]


def dilated_attention_group(a, group, seq_bounds, qb, n_res):
    d = DIL_RATES[group]
    r = DIL_WINDOWS[group] // (2 * d)
    w = ATTN_KV_W
    rows = a.shape[2]
    m = rows * d
    tile = qb * r * d
    n_tiles = m // tile
    n_blocks = rows // r
    starts = tuple(s // tile for s in seq_bounds[:-1])
    ends = tuple(s // tile for s in seq_bounds[1:])

    own = lambda sec: pl.BlockSpec((None, n_res, qb * r, w), lambda t, rr: (sec, rr, t, 0))
    prev = lambda sec: pl.BlockSpec(
        (None, n_res, r, w), lambda t, rr: (sec, rr, jnp.maximum(t * qb - 1, 0), 0))
    nxt = lambda sec: pl.BlockSpec(
        (None, n_res, r, w), lambda t, rr: (sec, rr, jnp.minimum((t + 1) * qb, n_blocks - 1), 0))
    hm = (ATTN_KV_HEADS, m, ATTN_HEAD_DIM)
    out = pl.BlockSpec((ATTN_KV_HEADS, tile, ATTN_HEAD_DIM), lambda t, rr: (0, t, 0))
    kern = functools.partial(_dil_kernel, dilation=d, group=group, qb=qb,
                             start_tiles=starts, end_tiles=ends)
    return pl.pallas_call(
        kern,
        grid=(n_tiles, d // n_res),
        in_specs=[own(0), prev(1), own(1), nxt(1), prev(2), own(2), nxt(2)],
        out_specs=[out, out],
        out_shape=[jax.ShapeDtypeStruct(hm, F32), jax.ShapeDtypeStruct(hm, F32)],
        compiler_params=_cparams(("parallel", "arbitrary"),
                                 16 * tile * w + n_res * (12 * qb + 16) * r * w + 16 * r * w * 4),
        name=f"dilated_attn_g{group}",
    )(a, a, a, a, a, a, a)


def _merge_proj_kernel(of_ref, ob_ref, r_ref, g_ref, o0_ref, o1_ref, o2_ref, l0_ref, l1_ref,
                       l2_ref, gg_ref, ga_ref, pg_ref, pa_ref, o_ref, ogla_buf, oatt_buf, *, dv):
    i = pl.program_id(0)
    j = pl.program_id(1)
    rc = of_ref.shape[0]
    rows = pl.ds(pl.multiple_of(j * rc, rc), rc)

    def branch_outputs():
        o = of_ref[...] + ob_ref[...]
        r = r_ref[...].astype(F32)
        gate = r * jax.nn.sigmoid(r) * g_ref[...]
        for h in range(GLA_HEADS):
            sl = slice(h * dv, (h + 1) * dv)
            oh = o[:, sl]
            ms = jnp.mean(oh * oh, axis=-1, keepdims=True)
            ogla_buf[i % 2, rows, sl] = (oh * lax.rsqrt(ms + NORM_EPS)
                                         * gate[:, sl]).astype(ogla_buf.dtype)
        e = ATTN_HEAD_DIM
        for h in range(ATTN_KV_HEADS):
            l0, l1, l2 = l0_ref[h], l1_ref[h], l2_ref[h]
            mx = jnp.maximum(jnp.maximum(l0, l1), l2)
            w0, w1, w2 = jnp.exp(l0 - mx), jnp.exp(l1 - mx), jnp.exp(l2 - mx)
            num = w0 * o0_ref[h] + w1 * o1_ref[h] + w2 * o2_ref[h]
            oatt_buf[i % 2, rows, h * e:(h + 1) * e] = (num / (w0 + w1 + w2)).astype(oatt_buf.dtype)

    @pl.when(i == 0)
    def _():
        o_ref[...] = jnp.zeros_like(o_ref)
        branch_outputs()

    @pl.when(i > 0)
    def _():
        yg = jnp.dot(ogla_buf[(i + 1) % 2], pg_ref[...], preferred_element_type=F32)
        ya = jnp.dot(oatt_buf[(i + 1) % 2], pa_ref[...], preferred_element_type=F32)
        gg = jax.nn.sigmoid(gg_ref[...].astype(F32))
        ga = jax.nn.sigmoid(ga_ref[...].astype(F32))
        o_ref[...] = (gg * yg + ga * ya).astype(o_ref.dtype)
        branch_outputs()


def merge_proj(o_f, o_b, z, r_col, gla_norm_g, outs, lses, dv, gate_start, proj_gla, proj_attn,
               tm=1024, tn=512):
    m = z.shape[0]
    vw = GLA_HEADS * dv
    kvw = ATTN_KV_W
    d = proj_gla.shape[1]
    nj = d // tn
    rc = tm // nj
    assert rc * nj == tm and rc % 16 == 0
    n_row_blocks = m // tm
    gate_col = gate_start // tn
    chunk = lambda i, j: jnp.minimum(i * nj + j, n_row_blocks * nj - 1)
    prev = lambda i: jnp.maximum(i - 1, 0)
    row = lambda w, col=0: pl.BlockSpec((rc, w), lambda i, j: (chunk(i, j), col))
    heads = pl.BlockSpec((ATTN_KV_HEADS, rc, ATTN_HEAD_DIM), lambda i, j: (0, chunk(i, j), 0))
    return pl.pallas_call(
        functools.partial(_merge_proj_kernel, dv=dv),
        grid=(n_row_blocks + 1, nj),
        in_specs=[row(vw), row(vw), row(vw, r_col),
                  pl.BlockSpec((1, vw), lambda i, j: (0, 0))] + [heads] * 6
        + [pl.BlockSpec((tm, tn), lambda i, j: (prev(i), gate_col + j)),
           pl.BlockSpec((tm, tn), lambda i, j: (prev(i), gate_col + nj + j)),
           pl.BlockSpec((vw, tn), lambda i, j: (0, j)),
           pl.BlockSpec((kvw, tn), lambda i, j: (0, j))],
        out_specs=pl.BlockSpec((tm, tn), lambda i, j: (i, j)),
        out_shape=jax.ShapeDtypeStruct((m + tm, d), BF16),
        scratch_shapes=[pltpu.VMEM((2, tm, vw), BF16), pltpu.VMEM((2, tm, kvw), BF16)],
        compiler_params=_cparams(("arbitrary", "arbitrary"),
                                 4 * tm * (vw + kvw) + 4 * tn * (vw + kvw) + 12 * tm * tn
                                 + 16 * tm * tn + rc * (20 * vw + 48 * kvw) + 16 * rc * vw),
        name="merge_proj",
    )(o_f, o_b, z, gla_norm_g.reshape(1, vw), *outs, *lses, z, z, proj_gla, proj_attn)


def _cast_kernel(w_ref, o_ref):
    o_ref[...] = w_ref[...].astype(o_ref.dtype)


def cast_layer(w_stack, layer, tr=128):
    _, rows, cols = w_stack.shape
    return pl.pallas_call(
        _cast_kernel,
        grid=(rows // tr,),
        in_specs=[pl.BlockSpec((None, tr, cols), lambda i: (layer, i, 0))],
        out_specs=pl.BlockSpec((tr, cols), lambda i: (i, 0)),
        out_shape=jax.ShapeDtypeStruct((rows, cols), BF16),
        compiler_params=_cparams(("parallel",), 16 * tr * cols),
        name="cast_layer",
    )(w_stack)


def _pad_to(x, axis, size):
    pad = [(0, 0)] * x.ndim
    pad[axis] = (0, size - x.shape[axis])
    return jnp.pad(x, pad)


def kernel(x_prompt, x_sample, ffn1_pre_g, ffn1_w_gate, ffn1_w_up, ffn1_w_down, ffn1_post_g,
           mix_pre_g, w_in, gla_decay_up_fwd, gla_decay_bias_fwd, gla_decay_up_bwd,
           gla_decay_bias_bwd, gla_norm_g, proj_gla, proj_attn, w_out, mix_post_g,
           ffn2_pre_g, ffn2_w_gate, ffn2_w_up, ffn2_w_down, ffn2_post_g):
    depth, d, d_ff = ffn1_w_gate.shape
    bp, tp, _ = x_prompt.shape
    bs, ts, _ = x_sample.shape
    mp, ms = bp * tp, bs * ts
    seq_bounds = tuple(b * tp for b in range(bp)) + tuple(mp + b * ts for b in range(bs + 1))

    dk = d // 16
    dv = d // 8
    qk_w = GLA_HEADS * dk
    v_w = GLA_HEADS * dv
    q_w = N_DIL_GROUPS * ATTN_KV_W
    lr0 = 2 * qk_w + 2 * v_w
    a0 = lr0 + 2 * GLA_RANK
    g0 = a0 + q_w + 2 * ATTN_KV_W
    ff_tk = -(-d_ff // (4 * 256)) * 256

    x = (x_prompt.reshape(mp, d), x_sample.reshape(ms, d))
    xn = rmsnorm(x, ffn1_pre_g[0])

    m_blocks = (mp + ms) // 1024
    up_steps = m_blocks * pl.cdiv(d_ff, 512)
    down_steps = m_blocks * (d // 1024) * pl.cdiv(d_ff, ff_tk)
    mixer_in_steps = m_blocks * ((lr0 + 2 * d) // 1024)

    wg1, wu1 = cast_layer(ffn1_w_gate, 0), cast_layer(ffn1_w_up, 0)
    pending = None
    for l in range(depth):
        wd_ride = (Ride(ffn1_w_down, l, up_steps),)
        if pending is None:
            hmid, wd1 = ffn_up(xn, wg1, wu1, rides=wd_ride)
        else:
            y_prev, g_prev, scale_prev = pending
            hmid, x, wd1 = ffn_up_prenorm(y_prev, x, g_prev, ffn1_pre_g[l], scale_prev,
                                          wg1, wu1, rides=wd_ride)
        y, = ffn_down(hmid, wd1, ff_tk, skip_blocks=0 if pending is None else 1)
        x, xn = norm_residual(y, x, ffn1_post_g[l], mix_pre_g[l], 0.5)

        w_all = jnp.concatenate(
            [w_in[l][:, :lr0], w_in[l][:, a0:],
             _pad_to(w_in[l][:, lr0:lr0 + GLA_RANK], 1, LANES),
             _pad_to(w_in[l][:, lr0 + GLA_RANK:a0], 1, LANES)], axis=1).astype(BF16)
        att_tiles = (g0 - a0) // 1024
        z, wg2, wu2, pg, pa, wo = matmul(
            xn, w_all, BF16, name="mixer_in",
            w_cols=(lr0 + 2 * d, lambda j: jnp.where(j < lr0 // 1024, j, j + att_tiles)),
            rides=(Ride(ffn2_w_gate, l, mixer_in_steps), Ride(ffn2_w_up, l, mixer_in_steps),
                   Ride(proj_gla, l, mixer_in_steps), Ride(proj_attn, l, mixer_in_steps),
                   Ride(w_out, l, mixer_in_steps)))
        lr_blk = (w_all.shape[1] - 2 * LANES) // (2 * LANES)
        lr = matmul(xn, w_all, F32, tn=2 * LANES, w_cols=(2 * LANES, lambda j: lr_blk),
                    name="mixer_in_lr")
        o_f = gla_scan(z, lr, _pad_to(gla_decay_up_fwd[l], 0, LANES),
                       gla_decay_bias_fwd[l].reshape(1, qk_w), seq_bounds, dk, dv, False)
        o_b = gla_scan(z, lr, _pad_to(gla_decay_up_bwd[l], 0, LANES),
                       gla_decay_bias_bwd[l].reshape(1, qk_w), seq_bounds, dk, dv, True)

        qkv = attn_in(xn, w_all, lr0)
        outs, lses = [], []
        for g, (qb, n_res) in enumerate(((16, 1), (4, 4), (1, 8))):
            o_g, l_g = dilated_attention_group(qkv[g], g, seq_bounds, qb, n_res)
            outs.append(o_g)
            lses.append(l_g)

        merged = merge_proj(o_f, o_b, z, (2 * qk_w + v_w) // v_w, gla_norm_g[l], outs, lses, dv,
                            lr0, pg, pa)
        y = matmul(merged, wo, BF16, skip_blocks=1, name="mixer_out")

        hmid, x, wd2 = ffn_up_prenorm(y, x, mix_post_g[l], ffn2_pre_g[l], 1.0, wg2, wu2,
                                      rides=(Ride(ffn2_w_down, l, up_steps),))
        if l + 1 == depth:
            y, = ffn_down(hmid, wd2, ff_tk, skip_blocks=1)
        else:
            y, wg1, wu1 = ffn_down(hmid, wd2, ff_tk, skip_blocks=1,
                                   rides=(Ride(ffn1_w_gate, l + 1, down_steps),
                                          Ride(ffn1_w_up, l + 1, down_steps)))
        pending = (y, ffn2_post_g[l], 0.5)

    y, g_post, scale = pending
    x, _ = norm_residual(y, x, g_post, None, scale, out_rows=(mp, ms))
    return (x[0].reshape(bp, tp, d), x[1].reshape(bs, ts, d))
```

```python
import functools

import jax
import jax.numpy as jnp
from jax import lax
from jax.experimental import pallas as pl
from jax.experimental.pallas import tpu as pltpu

F32 = jnp.float32
BF16 = jnp.bfloat16

NORM_EPS = 1e-6
NEG_INF = -1e30

GLA_HEADS = 4
GLA_RANK = 16
GLA_TAU = 16.0
GLA_CHUNK = 64
ATTN_KV_HEADS = 8
ATTN_HEAD_DIM = 128
ATTN_KV_W = ATTN_KV_HEADS * ATTN_HEAD_DIM
DIL_WINDOWS = (128, 512, 2048)
DIL_RATES = (1, 4, 16)
N_DIL_GROUPS = 3
ALIBI_MAX_BIAS = 8.0

LANES = 128
VMEM_LIMIT_CAP = 60 * 1024 * 1024


def _cparams(dims, vmem_bytes):
    return pltpu.CompilerParams(dimension_semantics=dims,
                                vmem_limit_bytes=int(min(vmem_bytes * 5 // 4, VMEM_LIMIT_CAP)))


def _any_equal(idx, values):
    return functools.reduce(jnp.logical_or, [idx == v for v in values])


def _part_specs(n_first, tr, d):
    return [pl.BlockSpec((tr, d), lambda i: (jnp.minimum(i, n_first - 1), 0)),
            pl.BlockSpec((tr, d), lambda i: (jnp.maximum(i - n_first, 0), 0))]


def _rmsnorm_kernel(xa_ref, xb_ref, g_ref, o_ref, *, n_first):
    x = jnp.where(pl.program_id(0) < n_first, xa_ref[...], xb_ref[...])
    ms = jnp.mean(x * x, axis=-1, keepdims=True)
    o_ref[...] = (x * lax.rsqrt(ms + NORM_EPS) * g_ref[...]).astype(o_ref.dtype)


def rmsnorm(x_parts, g, tr=256):
    d = x_parts[0].shape[1]
    n_first = x_parts[0].shape[0] // tr
    m = x_parts[0].shape[0] + x_parts[1].shape[0]
    return pl.pallas_call(
        functools.partial(_rmsnorm_kernel, n_first=n_first),
        grid=(m // tr,),
        in_specs=_part_specs(n_first, tr, d) + [pl.BlockSpec((1, d), lambda i: (0, 0))],
        out_specs=pl.BlockSpec((tr, d), lambda i: (i, 0)),
        out_shape=jax.ShapeDtypeStruct((m, d), BF16),
        compiler_params=_cparams(("arbitrary",), 10 * tr * d * 4),
        name="rmsnorm",
    )(*x_parts, g.reshape(1, d))


def _norm_residual_kernel(y_ref, *refs, scale, with_next, n_in_first, n_out_first):
    i = pl.program_id(0)
    refs = list(refs)
    if n_in_first is None:
        x = refs.pop(0)[...]
    else:
        xa_ref, xb_ref = refs.pop(0), refs.pop(0)
        x = jnp.where(i < n_in_first, xa_ref[...], xb_ref[...])
    gp_ref = refs.pop(0)
    y = y_ref[...].astype(F32)
    ms = jnp.mean(y * y, axis=-1, keepdims=True)
    xo = x + scale * (y * lax.rsqrt(ms + NORM_EPS) * gp_ref[...])
    if with_next:
        gn_ref, xn_ref = refs.pop(0), refs.pop()
        ms2 = jnp.mean(xo * xo, axis=-1, keepdims=True)
        xn_ref[...] = (xo * lax.rsqrt(ms2 + NORM_EPS) * gn_ref[...]).astype(xn_ref.dtype)
    if n_out_first is None:
        refs[0][...] = xo
    else:
        @pl.when(i < n_out_first)
        def _():
            refs[0][...] = xo

        @pl.when(i >= n_out_first)
        def _():
            refs[1][...] = xo


def norm_residual(y, x, g_post, g_next, scale, out_rows=None, tr=256):
    m, d = y.shape
    with_next = g_next is not None
    row = pl.BlockSpec((tr, d), lambda i: (i, 0))
    vec = pl.BlockSpec((1, d), lambda i: (0, 0))
    gains = [g_post.reshape(1, d)] + ([g_next.reshape(1, d)] if with_next else [])
    x_parts = x if isinstance(x, (tuple, list)) else None
    n_in_first = None if x_parts is None else x_parts[0].shape[0] // tr
    n_out_first = None if out_rows is None else out_rows[0] // tr
    x_specs = [row] if x_parts is None else _part_specs(n_in_first, tr, d)
    if out_rows is None:
        xo_specs, xo_shapes = [row], [jax.ShapeDtypeStruct((m, d), F32)]
    else:
        xo_specs = _part_specs(n_out_first, tr, d)
        xo_shapes = [jax.ShapeDtypeStruct((rows, d), F32) for rows in out_rows]
    out = pl.pallas_call(
        functools.partial(_norm_residual_kernel, scale=scale, with_next=with_next,
                          n_in_first=n_in_first, n_out_first=n_out_first),
        grid=(m // tr,),
        in_specs=[row] + x_specs + [vec] * len(gains),
        out_specs=xo_specs + ([row] if with_next else []),
        out_shape=xo_shapes + ([jax.ShapeDtypeStruct((m, d), BF16)] if with_next else []),
        compiler_params=_cparams(("arbitrary",), 14 * tr * d * 4),
        name="norm_residual",
    )(y, *(x_parts if x_parts is not None else [x]), *gains)
    xo = out[0] if out_rows is None else tuple(out[:2])
    return xo, (out[-1] if with_next else None)


class Ride:
    def __init__(self, w_stack, layer, max_blocks, min_tr=16):
        _, self.rows, self.cols = w_stack.shape
        tr = min_tr
        while self.rows % tr or self.rows // tr > max_blocks:
            tr *= 2
            assert tr <= self.rows
        self.w_stack, self.layer, self.tr = w_stack, layer, tr
        self.n_blocks = self.rows // tr

    def specs(self, step_of):
        blk = lambda *g: jnp.minimum(step_of(*g), self.n_blocks - 1)
        src = pl.BlockSpec((None, self.tr, self.cols), lambda *g: (self.layer, blk(*g), 0))
        dst = pl.BlockSpec((self.tr, self.cols), lambda *g: (blk(*g), 0))
        return src, dst

    @property
    def out_shape(self):
        return jax.ShapeDtypeStruct((self.rows, self.cols), BF16)

    @property
    def vmem_bytes(self):
        return 12 * self.tr * self.cols


def _split_rides(refs, n_base_in, n_base_out, n_rides):
    a = n_base_in
    b = a + n_rides
    c = b + n_base_out
    d = c + n_rides
    return refs[:a], refs[a:b], refs[b:c], refs[c:d], refs[d:]


def _do_rides(step, rides_n_blocks, srcs, dsts):
    for n_blocks, src, dst in zip(rides_n_blocks, srcs, dsts):
        @pl.when(step < n_blocks)
        def _(src=src, dst=dst):
            dst[...] = src[...].astype(dst.dtype)


def _mm_kernel(*refs, ride_blocks):
    (a_ref, w_ref), srcs, (o_ref,), dsts, _ = _split_rides(refs, 2, 1, len(ride_blocks))
    o_ref[...] = jnp.dot(a_ref[...], w_ref[...],
                         preferred_element_type=F32).astype(o_ref.dtype)
    _do_rides(pl.program_id(0) * pl.num_programs(1) + pl.program_id(1), ride_blocks, srcs, dsts)


def matmul(a, w, out_dtype, tm=1024, tn=1024, w_cols=None, skip_blocks=0, rides=(),
           name="matmul"):
    m, kd = a.shape
    m -= skip_blocks * tm
    n, col_of = w_cols if w_cols is not None else (w.shape[1], lambda j: j)
    tn = min(tn, n)
    nj = n // tn
    osz = jnp.dtype(out_dtype).itemsize
    assert all(r.n_blocks <= (m // tm) * nj for r in rides)
    ride_specs = [r.specs(lambda i, j: i * nj + j) for r in rides]
    out = pl.pallas_call(
        functools.partial(_mm_kernel, ride_blocks=tuple(r.n_blocks for r in rides)),
        grid=(m // tm, nj),
        in_specs=[pl.BlockSpec((tm, kd), lambda i, j: (i + skip_blocks, 0)),
                  pl.BlockSpec((kd, tn), lambda i, j: (0, col_of(j)))] + [s for s, _ in ride_specs],
        out_specs=[pl.BlockSpec((tm, tn), lambda i, j: (i, j))] + [d for _, d in ride_specs],
        out_shape=[jax.ShapeDtypeStruct((m, n), out_dtype)] + [r.out_shape for r in rides],
        compiler_params=_cparams(("arbitrary", "arbitrary"),
                                 4 * (tm * kd + kd * tn) + 2 * tm * tn * osz + 3 * tm * tn * 4
                                 + sum(r.vmem_bytes for r in rides)),
        name=name,
    )(a, w, *[r.w_stack for r in rides])
    return out[0] if not rides else out


def _ffn_up_kernel(*refs, n_tail, ride_blocks):
    (x_ref, wg_ref, wu_ref), srcs, (o_ref,), dsts, _ = _split_rides(refs, 3, 1, len(ride_blocks))
    j = pl.program_id(1)
    tn = o_ref.shape[1]
    _do_rides(pl.program_id(0) * pl.num_programs(1) + j, ride_blocks, srcs, dsts)

    def tile(width):
        x = x_ref[...]
        g = jnp.dot(x, wg_ref[:, :width], preferred_element_type=F32)
        u = jnp.dot(x, wu_ref[:, :width], preferred_element_type=F32)
        o_ref[:, :width] = (g * jax.nn.sigmoid(g) * u).astype(o_ref.dtype)

    if n_tail == tn:
        tile(tn)
    else:
        pl.when(j == 0)(lambda: tile(n_tail))
        pl.when(j > 0)(lambda: tile(tn))


def ffn_up(xn, wg, wu, tm=1024, tn=512, rides=()):
    m, d = xn.shape
    f = wg.shape[1]
    nj = pl.cdiv(f, tn)
    col = lambda j: (j + nj - 1) % nj
    assert all(r.n_blocks <= (m // tm) * nj for r in rides)
    ride_specs = [r.specs(lambda i, j: i * nj + j) for r in rides]
    out = pl.pallas_call(
        functools.partial(_ffn_up_kernel, n_tail=f - (nj - 1) * tn,
                          ride_blocks=tuple(r.n_blocks for r in rides)),
        grid=(m // tm, nj),
        in_specs=[pl.BlockSpec((tm, d), lambda i, j: (i, 0)),
                  pl.BlockSpec((d, tn), lambda i, j: (0, col(j))),
                  pl.BlockSpec((d, tn), lambda i, j: (0, col(j)))] + [s for s, _ in ride_specs],
        out_specs=[pl.BlockSpec((tm, tn), lambda i, j: (i, col(j)))] + [d for _, d in ride_specs],
        out_shape=[jax.ShapeDtypeStruct((m, f), BF16)] + [r.out_shape for r in rides],
        compiler_params=_cparams(("arbitrary", "arbitrary"),
                                 4 * tm * d + 8 * d * tn + 4 * tm * tn + 16 * tm * tn
                                 + sum(r.vmem_bytes for r in rides)),
        name="ffn_up",
    )(xn, wg, wu, *[r.w_stack for r in rides])
    return out


def _ffn_up_prenorm_kernel(*refs, n_tail, ride_blocks, scale, n_chunks, n_row_blocks):
    ((y_ref, x_ref, gp_ref, gn_ref, wg_ref, wu_ref), srcs, (o_ref, xo_ref), dsts,
     (xn_buf,)) = _split_rides(refs, 6, 2, len(ride_blocks))
    i = pl.program_id(0)
    j = pl.program_id(1)
    tn = o_ref.shape[1]
    rc = y_ref.shape[0]
    _do_rides(i * pl.num_programs(1) + j, ride_blocks, srcs, dsts)

    def prenorm():
        c = jnp.minimum(j, n_chunks - 1)
        y = y_ref[...].astype(F32)
        ms = jnp.mean(y * y, axis=-1, keepdims=True)
        xo = x_ref[...] + scale * (y * lax.rsqrt(ms + NORM_EPS) * gp_ref[...])
        xo_ref[...] = xo
        ms2 = jnp.mean(xo * xo, axis=-1, keepdims=True)
        xn_buf[i % 2, pl.ds(pl.multiple_of(c * rc, rc), rc), :] = (
            xo * lax.rsqrt(ms2 + NORM_EPS) * gn_ref[...]).astype(xn_buf.dtype)

    def tile(width):
        x = xn_buf[(i + 1) % 2]
        g = jnp.dot(x, wg_ref[:, :width], preferred_element_type=F32)
        u = jnp.dot(x, wu_ref[:, :width], preferred_element_type=F32)
        o_ref[:, :width] = (g * jax.nn.sigmoid(g) * u).astype(o_ref.dtype)

    @pl.when(i == 0)
    def _():
        o_ref[...] = jnp.zeros_like(o_ref)
        prenorm()

    @pl.when(jnp.logical_and(i > 0, j == 0))
    def _():
        tile(n_tail)
        prenorm()

    @pl.when(jnp.logical_and(i > 0, jnp.logical_and(j > 0, j < n_chunks)))
    def _():
        tile(tn)
        prenorm()

    @pl.when(jnp.logical_and(i > 0, j >= n_chunks))
    def _():
        tile(tn)


def ffn_up_prenorm(y, x, g_post, g_next, scale, wg, wu, tm=1024, tn=512, rides=()):
    m, d = y.shape
    f = wg.shape[1]
    nj = pl.cdiv(f, tn)
    n_row_blocks = m // tm
    n_chunks = 1
    while n_chunks * 2 <= min(nj, 16):
        n_chunks *= 2
    rc = tm // n_chunks
    col = lambda j: (j + nj - 1) % nj
    chunk = lambda i, j: (jnp.minimum(i * n_chunks + jnp.minimum(j, n_chunks - 1),
                                      n_row_blocks * n_chunks - 1), 0)
    assert all(r.n_blocks <= (n_row_blocks + 1) * nj for r in rides)
    ride_specs = [r.specs(lambda i, j: i * nj + j) for r in rides]
    vec = pl.BlockSpec((1, d), lambda i, j: (0, 0))
    out = pl.pallas_call(
        functools.partial(_ffn_up_prenorm_kernel, n_tail=f - (nj - 1) * tn,
                          ride_blocks=tuple(r.n_blocks for r in rides), scale=scale,
                          n_chunks=n_chunks, n_row_blocks=n_row_blocks),
        grid=(n_row_blocks + 1, nj),
        in_specs=[pl.BlockSpec((rc, d), chunk), pl.BlockSpec((rc, d), chunk), vec, vec,
                  pl.BlockSpec((d, tn), lambda i, j: (0, col(j))),
                  pl.BlockSpec((d, tn), lambda i, j: (0, col(j)))] + [s for s, _ in ride_specs],
        out_specs=[pl.BlockSpec((tm, tn), lambda i, j: (i, col(j))),
                   pl.BlockSpec((rc, d), chunk)] + [dst for _, dst in ride_specs],
        out_shape=[jax.ShapeDtypeStruct((m + tm, f), BF16), jax.ShapeDtypeStruct((m, d), F32)]
        + [r.out_shape for r in rides],
        scratch_shapes=[pltpu.VMEM((2, tm, d), BF16)],
        compiler_params=_cparams(("arbitrary", "arbitrary"),
                                 4 * tm * d + 8 * d * tn + 4 * tm * tn + 16 * tm * tn
                                 + 20 * rc * d + 16 * rc * d + sum(r.vmem_bytes for r in rides)),
        name="ffn_up_prenorm",
    )(y, x, g_post.reshape(1, d), g_next.reshape(1, d), wg, wu, *[r.w_stack for r in rides])
    return out


def _ffn_down_kernel(*refs, k_tail, ride_blocks):
    (a_ref, w_ref), srcs, (o_ref,), dsts, (acc_ref,) = _split_rides(refs, 2, 1, len(ride_blocks))
    k = pl.program_id(2)
    last = pl.num_programs(2) - 1
    tk = a_ref.shape[1]
    step = (pl.program_id(0) * pl.num_programs(1) + pl.program_id(1)) * pl.num_programs(2) + k
    _do_rides(step, ride_blocks, srcs, dsts)

    @pl.when(k == 0)
    def _():
        acc_ref[...] = jnp.zeros_like(acc_ref)

    @pl.when(k < last)
    def _():
        acc_ref[...] += jnp.dot(a_ref[...], w_ref[...], preferred_element_type=F32)

    @pl.when(k == last)
    def _():
        if k_tail == tk:
            part = jnp.dot(a_ref[...], w_ref[...], preferred_element_type=F32)
        else:
            part = jnp.dot(a_ref[:, :k_tail], w_ref[:k_tail, :], preferred_element_type=F32)
        o_ref[...] = (acc_ref[...] + part).astype(o_ref.dtype)


def ffn_down(h, wd, tk, tm=1024, tn=1024, skip_blocks=0, rides=()):
    m, kd = h.shape
    m -= skip_blocks * tm
    n = wd.shape[1]
    nj = n // tn
    nk = pl.cdiv(kd, tk)
    assert all(r.n_blocks <= (m // tm) * nj * nk for r in rides)
    ride_specs = [r.specs(lambda i, j, k: (i * nj + j) * nk + k) for r in rides]
    out = pl.pallas_call(
        functools.partial(_ffn_down_kernel, k_tail=kd - (nk - 1) * tk,
                          ride_blocks=tuple(r.n_blocks for r in rides)),
        grid=(m // tm, nj, nk),
        in_specs=[pl.BlockSpec((tm, tk), lambda i, j, k: (i + skip_blocks, k)),
                  pl.BlockSpec((tk, tn), lambda i, j, k: (k, j))] + [s for s, _ in ride_specs],
        out_specs=[pl.BlockSpec((tm, tn), lambda i, j, k: (i, j))] + [d for _, d in ride_specs],
        out_shape=[jax.ShapeDtypeStruct((m, n), BF16)] + [r.out_shape for r in rides],
        scratch_shapes=[pltpu.VMEM((tm, tn), F32)],
        compiler_params=_cparams(("arbitrary", "arbitrary", "arbitrary"),
                                 4 * (tm * tk + tk * tn) + 4 * tm * tn + 16 * tm * tn
                                 + sum(r.vmem_bytes for r in rides)),
        name="ffn_down",
    )(h, wd, *[r.w_stack for r in rides])
    return out


def _gla_kernel(q_ref, k_ref, v_ref, lr_ref, up_ref, bias_ref, o_ref, st_ref, *,
                backward, n_chunks, reset_blocks, dk):
    i = pl.program_id(1)
    blk = pl.num_programs(1) - 1 - i if backward else i

    @pl.when(_any_equal(blk, reset_blocks))
    def _():
        st_ref[...] = jnp.zeros_like(st_ref)

    c = GLA_CHUNK
    nc = n_chunks
    row = lax.broadcasted_iota(jnp.int32, (c, c), 0)
    col = lax.broadcasted_iota(jnp.int32, (c, c), 1)
    if backward:
        tri, mask = row <= col, row < col
    else:
        tri, mask = row >= col, row >= col
    tri = jnp.broadcast_to(tri.astype(BF16)[None], (nc, c, c))

    def bmm(a, b, ca, cb):
        return lax.dot_general(a, b, (((ca,), (cb,)), ((0,), (0,))), preferred_element_type=F32)

    x = jnp.dot(lr_ref[...].astype(BF16), up_ref[...].astype(BF16),
                preferred_element_type=F32) + bias_ref[...]
    la = (jnp.minimum(x, 0.0) - jnp.log1p(jnp.exp(-jnp.abs(x)))) * (1.0 / GLA_TAU)
    la = la.reshape(nc, c, dk)
    la_hi = la.astype(BF16)
    la_lo = (la - la_hi.astype(F32)).astype(BF16)
    b = bmm(tri, la_hi, 2, 1) + bmm(tri, la_lo, 2, 1)
    tot = jnp.sum(la, axis=1, keepdims=True)
    q = q_ref[...].astype(F32).reshape(nc, c, dk)
    k = k_ref[...].astype(F32).reshape(nc, c, dk)
    v = v_ref[...].reshape(nc, c, v_ref.shape[1])
    qd = (q * (dk ** -0.5) * jnp.exp(b)).astype(BF16)
    kd = (k * jnp.exp(-b)).astype(BF16)
    ks = (k * jnp.exp(tot - b)).astype(BF16)
    scores = jnp.where(mask[None], bmm(qd, kd, 2, 2), 0.0).astype(BF16)
    o_intra = bmm(scores, v, 2, 1)
    st_inc = bmm(v, ks, 1, 1)
    chunk_decay = jnp.exp(tot)

    st = st_ref[...]
    nt = (((1,), (1,)), ((), ()))
    for ci in (reversed(range(nc)) if backward else range(nc)):
        o_ref[ci * c:(ci + 1) * c, :] = o_intra[ci] + lax.dot_general(
            qd[ci], st.astype(BF16), nt, preferred_element_type=F32)
        st = st * chunk_decay[ci] + st_inc[ci]
    st_ref[...] = st


def gla_scan(z, lr, up, bias, seq_bounds, dk, dv, backward, tb=2048):
    m = z.shape[0]
    h = GLA_HEADS
    assert all(s % tb == 0 for s in seq_bounds), "sequences must be whole blocks"
    nblk = m // tb
    if backward:
        resets = tuple(s // tb - 1 for s in seq_bounds[1:])
        tok = lambda i: nblk - 1 - i
    else:
        resets = tuple(s // tb for s in seq_bounds[:-1])
        tok = lambda i: i
    k_off = (h * dk) // dk
    v_off = (2 * h * dk) // dv
    lr_col = 1 if backward else 0
    kern = functools.partial(_gla_kernel, backward=backward, n_chunks=tb // GLA_CHUNK,
                             reset_blocks=resets, dk=dk)
    return pl.pallas_call(
        kern,
        grid=(h, nblk),
        in_specs=[pl.BlockSpec((tb, dk), lambda hh, i: (tok(i), hh)),
                  pl.BlockSpec((tb, dk), lambda hh, i: (tok(i), k_off + hh)),
                  pl.BlockSpec((tb, dv), lambda hh, i: (tok(i), v_off + hh)),
                  pl.BlockSpec((tb, LANES), lambda hh, i: (tok(i), lr_col)),
                  pl.BlockSpec((LANES, dk), lambda hh, i: (0, hh)),
                  pl.BlockSpec((1, dk), lambda hh, i: (0, hh))],
        out_specs=pl.BlockSpec((tb, dv), lambda hh, i: (tok(i), hh)),
        out_shape=jax.ShapeDtypeStruct((m, h * dv), F32),
        scratch_shapes=[pltpu.VMEM((dv, dk), F32)],
        compiler_params=_cparams(("arbitrary", "arbitrary"), 40 * 1024 * 1024),
        name="gla_scan_bwd" if backward else "gla_scan_fwd",
    )(z, z, z, lr, up, bias)


def _attn_in_kernel(x_ref, w_ref, *refs):
    out_refs, acc_ref = refs[:-1], refs[-1]
    j = pl.program_id(1)
    n_chunks, tm, _ = acc_ref.shape
    res = jnp.dot(x_ref[...], w_ref[...], preferred_element_type=F32)
    for c in range(n_chunks):
        acc_ref[c] = res[:, c * LANES:(c + 1) * LANES]
    for g, (o_ref, d) in enumerate(zip(out_refs, DIL_RATES)):
        @pl.when(jnp.logical_or(j == g, j >= N_DIL_GROUPS))
        def _(o_ref=o_ref, d=d):
            for r in range(d):
                for c in range(n_chunks):
                    rows = pl.ds(r, tm // d, stride=d) if d > 1 else slice(None)
                    o_ref[r, :, c * LANES:(c + 1) * LANES] = (
                        acc_ref[c, rows, :].astype(o_ref.dtype))


def attn_in(xn, w_att, col0, tm=1024):
    m, kd = xn.shape
    w = ATTN_KV_W
    nj = N_DIL_GROUPS + 2
    blk0 = col0 // w
    assert blk0 * w == col0
    sect = lambda i, j: (jnp.clip(j - (N_DIL_GROUPS - 1), 0, 2), 0, i, 0)
    return pl.pallas_call(
        _attn_in_kernel,
        grid=(m // tm, nj),
        in_specs=[pl.BlockSpec((tm, kd), lambda i, j: (i, 0)),
                  pl.BlockSpec((kd, w), lambda i, j: (0, blk0 + j))],
        out_specs=[pl.BlockSpec((None, d, tm // d, w), sect) for d in DIL_RATES],
        out_shape=[jax.ShapeDtypeStruct((3, d, m // d, w), BF16) for d in DIL_RATES],
        scratch_shapes=[pltpu.VMEM((w // LANES, tm, LANES), F32)],
        compiler_params=_cparams(("parallel", "arbitrary"),
                                 4 * (tm * kd + kd * w) + 12 * tm * w + 8 * tm * w),
        name="attn_in",
    )(xn, w_att)


def _dil_kernel(q_ref, kp_ref, ko_ref, kn_ref, vp_ref, vo_ref, vn_ref, o_ref, l_ref, *,
                dilation, group, qb, start_tiles, end_tiles):
    t = pl.program_id(0)
    n_res = q_ref.shape[0]
    res0 = pl.program_id(1) * n_res
    has_prev = jnp.logical_not(_any_equal(t, start_tiles))
    has_next = jnp.logical_not(_any_equal(t + 1, end_tiles))
    r = DIL_WINDOWS[group] // (2 * dilation)
    e = ATTN_HEAD_DIM
    qi = lax.broadcasted_iota(jnp.int32, (r, 3 * r), 0)
    kj = lax.broadcasted_iota(jnp.int32, (r, 3 * r), 1)
    rel = jnp.abs(kj - r - qi)
    in_window = rel <= r
    rel_f = rel.astype(F32) * float(dilation)
    prev_ok = jnp.logical_or(kj >= r, has_prev)
    next_ok = jnp.logical_or(kj < 2 * r, has_next)

    def valid(j):
        v = in_window
        if j == 0:
            v = jnp.logical_and(v, prev_ok)
        if j == qb - 1:
            v = jnp.logical_and(v, next_ok)
        return v

    valids = {j: valid(j) for j in {0, qb - 1, min(1, qb - 1)}}
    n_heads = N_DIL_GROUPS * ATTN_KV_HEADS
    scale = e ** -0.5
    nh = ATTN_KV_HEADS

    def heads(ref, ri, rows):
        return jnp.stack([ref[ri, rows, h * e:(h + 1) * e] for h in range(nh)])

    slopes = [2.0 ** (-ALIBI_MAX_BIAS * (group * nh + h + 1) / n_heads) for h in range(nh)]
    bias = jnp.stack([slope * rel_f for slope in slopes])
    every = slice(None)
    qk_dims = (((2,), (2,)), ((0,), (0,)))
    pv_dims = (((2,), (1,)), ((0,), (0,)))
    for ri in range(n_res):
        k_all = jnp.concatenate([heads(kp_ref, ri, every), heads(ko_ref, ri, every),
                                 heads(kn_ref, ri, every)], axis=1)
        v_all = jnp.concatenate([heads(vp_ref, ri, every), heads(vo_ref, ri, every),
                                 heads(vn_ref, ri, every)], axis=1)
        for j in range(qb):
            q = heads(q_ref, ri, slice(j * r, (j + 1) * r))
            s = lax.dot_general(q, k_all[:, j * r:(j + 3) * r], qk_dims,
                                preferred_element_type=F32)
            s = jnp.where(valids.get(j, in_window)[None], s * scale - bias, NEG_INF)
            mx = jnp.max(s, axis=-1, keepdims=True)
            p = jnp.exp(s - mx)
            den = jnp.sum(p, axis=-1, keepdims=True)
            o = lax.dot_general(p.astype(BF16), v_all[:, j * r:(j + 3) * r], pv_dims,
                                preferred_element_type=F32) / den
            lse = jnp.broadcast_to(mx + jnp.log(den), (nh, r, e))
            if dilation == 1:
                dst = pl.ds(j * r, r)
            else:
                dst = pl.ds(j * r * dilation + res0 + ri, r, stride=dilation)
            for h in range(nh):
                o_ref[h, dst, :] = o[h]
                l_ref[h, dst, :] = lse[h]


def dilated_attention_group(a, group, seq_bounds, qb, n_res):
    d = DIL_RATES[group]
    r = DIL_WINDOWS[group] // (2 * d)
    w = ATTN_KV_W
    rows = a.shape[2]
    m = rows * d
    tile = qb * r * d
    n_tiles = m // tile
    n_blocks = rows // r
    starts = tuple(s // tile for s in seq_bounds[:-1])
    ends = tuple(s // tile for s in seq_bounds[1:])

    own = lambda sec: pl.BlockSpec((None, n_res, qb * r, w), lambda t, rr: (sec, rr, t, 0))
    prev = lambda sec: pl.BlockSpec(
        (None, n_res, r, w), lambda t, rr: (sec, rr, jnp.maximum(t * qb - 1, 0), 0))
    nxt = lambda sec: pl.BlockSpec(
        (None, n_res, r, w), lambda t, rr: (sec, rr, jnp.minimum((t + 1) * qb, n_blocks - 1), 0))
    hm = (ATTN_KV_HEADS, m, ATTN_HEAD_DIM)
    out = pl.BlockSpec((ATTN_KV_HEADS, tile, ATTN_HEAD_DIM), lambda t, rr: (0, t, 0))
    kern = functools.partial(_dil_kernel, dilation=d, group=group, qb=qb,
                             start_tiles=starts, end_tiles=ends)
    return pl.pallas_call(
        kern,
        grid=(n_tiles, d // n_res),
        in_specs=[own(0), prev(1), own(1), nxt(1), prev(2), own(2), nxt(2)],
        out_specs=[out, out],
        out_shape=[jax.ShapeDtypeStruct(hm, F32), jax.ShapeDtypeStruct(hm, F32)],
        compiler_params=_cparams(("parallel", "arbitrary"),
                                 16 * tile * w + n_res * (12 * qb + 16) * r * w + 16 * r * w * 4),
        name=f"dilated_attn_g{group}",
    )(a, a, a, a, a, a, a)


def _merge_proj_kernel(of_ref, ob_ref, r_ref, g_ref, o0_ref, o1_ref, o2_ref, l0_ref, l1_ref,
                       l2_ref, gg_ref, ga_ref, pg_ref, pa_ref, o_ref, ogla_buf, oatt_buf, *, dv):
    i = pl.program_id(0)
    j = pl.program_id(1)
    rc = of_ref.shape[0]
    rows = pl.ds(pl.multiple_of(j * rc, rc), rc)

    def branch_outputs():
        o = of_ref[...] + ob_ref[...]
        r = r_ref[...].astype(F32)
        gate = r * jax.nn.sigmoid(r) * g_ref[...]
        for h in range(GLA_HEADS):
            sl = slice(h * dv, (h + 1) * dv)
            oh = o[:, sl]
            ms = jnp.mean(oh * oh, axis=-1, keepdims=True)
            ogla_buf[i % 2, rows, sl] = (oh * lax.rsqrt(ms + NORM_EPS)
                                         * gate[:, sl]).astype(ogla_buf.dtype)
        e = ATTN_HEAD_DIM
        for h in range(ATTN_KV_HEADS):
            l0, l1, l2 = l0_ref[h], l1_ref[h], l2_ref[h]
            mx = jnp.maximum(jnp.maximum(l0, l1), l2)
            w0, w1, w2 = jnp.exp(l0 - mx), jnp.exp(l1 - mx), jnp.exp(l2 - mx)
            num = w0 * o0_ref[h] + w1 * o1_ref[h] + w2 * o2_ref[h]
            oatt_buf[i % 2, rows, h * e:(h + 1) * e] = (num / (w0 + w1 + w2)).astype(oatt_buf.dtype)

    @pl.when(i == 0)
    def _():
        o_ref[...] = jnp.zeros_like(o_ref)
        branch_outputs()

    @pl.when(i > 0)
    def _():
        yg = jnp.dot(ogla_buf[(i + 1) % 2], pg_ref[...], preferred_element_type=F32)
        ya = jnp.dot(oatt_buf[(i + 1) % 2], pa_ref[...], preferred_element_type=F32)
        gg = jax.nn.sigmoid(gg_ref[...].astype(F32))
        ga = jax.nn.sigmoid(ga_ref[...].astype(F32))
        o_ref[...] = (gg * yg + ga * ya).astype(o_ref.dtype)
        branch_outputs()


def merge_proj(o_f, o_b, z, r_col, gla_norm_g, outs, lses, dv, gate_start, proj_gla, proj_attn,
               tm=1024, tn=512):
    m = z.shape[0]
    vw = GLA_HEADS * dv
    kvw = ATTN_KV_W
    d = proj_gla.shape[1]
    nj = d // tn
    rc = tm // nj
    assert rc * nj == tm and rc % 16 == 0
    n_row_blocks = m // tm
    gate_col = gate_start // tn
    chunk = lambda i, j: jnp.minimum(i * nj + j, n_row_blocks * nj - 1)
    prev = lambda i: jnp.maximum(i - 1, 0)
    row = lambda w, col=0: pl.BlockSpec((rc, w), lambda i, j: (chunk(i, j), col))
    heads = pl.BlockSpec((ATTN_KV_HEADS, rc, ATTN_HEAD_DIM), lambda i, j: (0, chunk(i, j), 0))
    return pl.pallas_call(
        functools.partial(_merge_proj_kernel, dv=dv),
        grid=(n_row_blocks + 1, nj),
        in_specs=[row(vw), row(vw), row(vw, r_col),
                  pl.BlockSpec((1, vw), lambda i, j: (0, 0))] + [heads] * 6
        + [pl.BlockSpec((tm, tn), lambda i, j: (prev(i), gate_col + j)),
           pl.BlockSpec((tm, tn), lambda i, j: (prev(i), gate_col + nj + j)),
           pl.BlockSpec((vw, tn), lambda i, j: (0, j)),
           pl.BlockSpec((kvw, tn), lambda i, j: (0, j))],
        out_specs=pl.BlockSpec((tm, tn), lambda i, j: (i, j)),
        out_shape=jax.ShapeDtypeStruct((m + tm, d), BF16),
        scratch_shapes=[pltpu.VMEM((2, tm, vw), BF16), pltpu.VMEM((2, tm, kvw), BF16)],
        compiler_params=_cparams(("arbitrary", "arbitrary"),
                                 4 * tm * (vw + kvw) + 4 * tn * (vw + kvw) + 12 * tm * tn
                                 + 16 * tm * tn + rc * (20 * vw + 48 * kvw) + 16 * rc * vw),
        name="merge_proj",
    )(o_f, o_b, z, gla_norm_g.reshape(1, vw), *outs, *lses, z, z, proj_gla, proj_attn)


def _cast_kernel(w_ref, o_ref):
    o_ref[...] = w_ref[...].astype(o_ref.dtype)


def cast_layer(w_stack, layer, tr=128):
    _, rows, cols = w_stack.shape
    return pl.pallas_call(
        _cast_kernel,
        grid=(rows // tr,),
        in_specs=[pl.BlockSpec((None, tr, cols), lambda i: (layer, i, 0))],
        out_specs=pl.BlockSpec((tr, cols), lambda i: (i, 0)),
        out_shape=jax.ShapeDtypeStruct((rows, cols), BF16),
        compiler_params=_cparams(("parallel",), 16 * tr * cols),
        name="cast_layer",
    )(w_stack)


def _pad_to(x, axis, size):
    pad = [(0, 0)] * x.ndim
    pad[axis] = (0, size - x.shape[axis])
    return jnp.pad(x, pad)


def kernel(x_prompt, x_sample, ffn1_pre_g, ffn1_w_gate, ffn1_w_up, ffn1_w_down, ffn1_post_g,
           mix_pre_g, w_in, gla_decay_up_fwd, gla_decay_bias_fwd, gla_decay_up_bwd,
           gla_decay_bias_bwd, gla_norm_g, proj_gla, proj_attn, w_out, mix_post_g,
           ffn2_pre_g, ffn2_w_gate, ffn2_w_up, ffn2_w_down, ffn2_post_g):
    depth, d, d_ff = ffn1_w_gate.shape
    bp, tp, _ = x_prompt.shape
    bs, ts, _ = x_sample.shape
    mp, ms = bp * tp, bs * ts
    seq_bounds = tuple(b * tp for b in range(bp)) + tuple(mp + b * ts for b in range(bs + 1))

    dk = d // 16
    dv = d // 8
    qk_w = GLA_HEADS * dk
    v_w = GLA_HEADS * dv
    q_w = N_DIL_GROUPS * ATTN_KV_W
    lr0 = 2 * qk_w + 2 * v_w
    a0 = lr0 + 2 * GLA_RANK
    g0 = a0 + q_w + 2 * ATTN_KV_W
    ff_tk = -(-d_ff // (4 * 256)) * 256

    x = (x_prompt.reshape(mp, d), x_sample.reshape(ms, d))
    xn = rmsnorm(x, ffn1_pre_g[0])

    m_blocks = (mp + ms) // 1024
    up_steps = m_blocks * pl.cdiv(d_ff, 512)
    down_steps = m_blocks * (d // 1024) * pl.cdiv(d_ff, ff_tk)
    mixer_in_steps = m_blocks * ((lr0 + 2 * d) // 1024)

    wg1, wu1 = cast_layer(ffn1_w_gate, 0), cast_layer(ffn1_w_up, 0)
    pending = None
    for l in range(depth):
        wd_ride = (Ride(ffn1_w_down, l, up_steps),)
        if pending is None:
            hmid, wd1 = ffn_up(xn, wg1, wu1, rides=wd_ride)
        else:
            y_prev, g_prev, scale_prev = pending
            hmid, x, wd1 = ffn_up_prenorm(y_prev, x, g_prev, ffn1_pre_g[l], scale_prev,
                                          wg1, wu1, rides=wd_ride)
        y, = ffn_down(hmid, wd1, ff_tk, skip_blocks=0 if pending is None else 1)
        x, xn = norm_residual(y, x, ffn1_post_g[l], mix_pre_g[l], 0.5)

        w_all = jnp.concatenate(
            [w_in[l][:, :lr0], w_in[l][:, a0:],
             _pad_to(w_in[l][:, lr0:lr0 + GLA_RANK], 1, LANES),
             _pad_to(w_in[l][:, lr0 + GLA_RANK:a0], 1, LANES)], axis=1).astype(BF16)
        att_tiles = (g0 - a0) // 1024
        z, wg2, wu2, pg, pa, wo = matmul(
            xn, w_all, BF16, name="mixer_in",
            w_cols=(lr0 + 2 * d, lambda j: jnp.where(j < lr0 // 1024, j, j + att_tiles)),
            rides=(Ride(ffn2_w_gate, l, mixer_in_steps), Ride(ffn2_w_up, l, mixer_in_steps),
                   Ride(proj_gla, l, mixer_in_steps), Ride(proj_attn, l, mixer_in_steps),
                   Ride(w_out, l, mixer_in_steps)))
        lr_blk = (w_all.shape[1] - 2 * LANES) // (2 * LANES)
        lr = matmul(xn, w_all, F32, tn=2 * LANES, w_cols=(2 * LANES, lambda j: lr_blk),
                    name="mixer_in_lr")
        o_f = gla_scan(z, lr, _pad_to(gla_decay_up_fwd[l], 0, LANES),
                       gla_decay_bias_fwd[l].reshape(1, qk_w), seq_bounds, dk, dv, False)
        o_b = gla_scan(z, lr, _pad_to(gla_decay_up_bwd[l], 0, LANES),
                       gla_decay_bias_bwd[l].reshape(1, qk_w), seq_bounds, dk, dv, True)

        qkv = attn_in(xn, w_all, lr0)
        outs, lses = [], []
        for g, (qb, n_res) in enumerate(((16, 1), (4, 4), (1, 8))):
            o_g, l_g = dilated_attention_group(qkv[g], g, seq_bounds, qb, n_res)
            outs.append(o_g)
            lses.append(l_g)

        merged = merge_proj(o_f, o_b, z, (2 * qk_w + v_w) // v_w, gla_norm_g[l], outs, lses, dv,
                            lr0, pg, pa)
        y = matmul(merged, wo, BF16, skip_blocks=1, name="mixer_out")

        hmid, x, wd2 = ffn_up_prenorm(y, x, mix_post_g[l], ffn2_pre_g[l], 1.0, wg2, wu2,
                                      rides=(Ride(ffn2_w_down, l, up_steps),))
        if l + 1 == depth:
            y, = ffn_down(hmid, wd2, ff_tk, skip_blocks=1)
        else:
            y, wg1, wu1 = ffn_down(hmid, wd2, ff_tk, skip_blocks=1,
                                   rides=(Ride(ffn1_w_gate, l + 1, down_steps),
                                          Ride(ffn1_w_up, l + 1, down_steps)))
        pending = (y, ffn2_post_g[l], 0.5)

    y, g_post, scale = pending
    x, _ = norm_residual(y, x, g_post, None, scale, out_rows=(mp, ms))
    return (x[0].reshape(bp, tp, d), x[1].reshape(bs, ts, d))
```
